```python
import math
import jax, jax.numpy as jnp
from jax import lax
import numpy as np

D_MODEL = 4096
BATCH = 4
SEQ = 4096
DEPTH = 2

CTX_LEN = 256
GRID_W = 64
D_S5 = D_MODEL // 4
S5_GROUP_DIM = 16
S5_GROUPS = D_S5 // S5_GROUP_DIM
S5_STATE = 64
D_FT = D_MODEL // 4
FT_GROUPS = 4
FT_GROUP_DIM = D_FT // FT_GROUPS
D_IN = D_S5 + D_FT + 2 * D_MODEL
N_EXPERTS = 16
CAPACITY_FACTOR = 2
D_EXPERT = D_MODEL // 4
N_MOD = 6
RMS_EPS = 1e-6
POS_BASE = 10000.0
DT_MIN = 1e-3
DT_MAX = 1e-1
LAMBDA_RE_MAX = -1e-4

kernel_name = "hybrid_s5_fnet_ec_moe_dit"


def rmsnorm(x, g):
    xf = x.astype(jnp.float32)
    y = xf * lax.rsqrt(jnp.mean(xf * xf, axis=-1, keepdims=True) + RMS_EPS)
    return (y * g.astype(jnp.float32)).astype(x.dtype)


def modulate(h, shift, scale):
    return h * (1 + scale) + shift


def grid_posembed(n_tok, d):
    rows = n_tok // GRID_W
    row = jnp.broadcast_to(jnp.arange(rows, dtype=jnp.float32)[:, None], (rows, GRID_W)).reshape(-1)
    col = jnp.broadcast_to(jnp.arange(GRID_W, dtype=jnp.float32)[None, :], (rows, GRID_W)).reshape(-1)
    quarter = d // 4
    inv_freq = POS_BASE ** (-jnp.arange(quarter, dtype=jnp.float32) / quarter)
    ang_r = row[:, None] * inv_freq
    ang_c = col[:, None] * inv_freq
    return jnp.concatenate([jnp.sin(ang_r), jnp.cos(ang_r), jnp.sin(ang_c), jnp.cos(ang_c)], axis=-1)


def s5_discretise(lam_re, lam_im, log_dt, b_re, b_im):
    lr = jnp.minimum(lam_re.astype(jnp.float32), LAMBDA_RE_MAX)
    li = lam_im.astype(jnp.float32)
    dt = jnp.exp(log_dt.astype(jnp.float32))[:, None]
    mag = jnp.exp(lr * dt)
    a_re = mag * jnp.cos(li * dt)
    a_im = mag * jnp.sin(li * dt)
    num_re, num_im = a_re - 1.0, a_im
    den = lr * lr + li * li
    f_re = (num_re * lr + num_im * li) / den
    f_im = (num_im * lr - num_re * li) / den
    br, bi = b_re.astype(jnp.float32), b_im.astype(jnp.float32)
    bb_re = f_re[..., None] * br - f_im[..., None] * bi
    bb_im = f_re[..., None] * bi + f_im[..., None] * br
    return a_re, a_im, bb_re, bb_im


def _linear_recurrence_combine(e1, e2):
    a1r, a1i, b1r, b1i = e1
    a2r, a2i, b2r, b2i = e2
    return (a1r * a2r - a1i * a2i,
            a1r * a2i + a1i * a2r,
            a2r * b1r - a2i * b1i + b2r,
            a2r * b1i + a2i * b1r + b2i)


def diag_scan(ug, a_re, a_im, bb_re, bb_im, h0):
    bu_re = jnp.einsum('gpn,blgn->blgp', bb_re, ug)
    bu_im = jnp.einsum('gpn,blgn->blgp', bb_im, ug)
    if h0 is not None:
        h_re, h_im = h0
        bu_re = bu_re.at[:, 0].add(a_re * h_re - a_im * h_im)
        bu_im = bu_im.at[:, 0].add(a_re * h_im + a_im * h_re)
    l = ug.shape[1]
    ar = jnp.broadcast_to(a_re[None, None], (1, l) + a_re.shape)
    ai = jnp.broadcast_to(a_im[None, None], (1, l) + a_im.shape)
    _, _, h_re, h_im = lax.associative_scan(_linear_recurrence_combine, (ar, ai, bu_re, bu_im), axis=1)
    return h_re, h_im


def s5_bidirectional(u, disc_f, disc_b, h0_f, h0_b):
    b, l, _ = u.shape
    ug = u.astype(jnp.float32).reshape(b, l, S5_GROUPS, S5_GROUP_DIM)
    st_f = diag_scan(ug, *disc_f, h0_f)
    st_b = diag_scan(ug[:, ::-1], *disc_b, h0_b)
    return ug, st_f, st_b


def s5_readout(h_re, h_im, c_re, c_im):
    return (jnp.einsum('gnp,blgp->blgn', c_re.astype(jnp.float32), h_re)
            - jnp.einsum('gnp,blgp->blgn', c_im.astype(jnp.float32), h_im))


def fourier_mix(v):
    b, l, _ = v.shape
    vg = v.astype(jnp.float32).reshape(b, l, FT_GROUPS, FT_GROUP_DIM)
    y = jnp.fft.fft2(vg, axes=(1, 3), norm='ortho').real
    return y.reshape(b, l, D_FT).astype(v.dtype)


def hybrid_mixer(p, ug, st_f, st_b, c_f, c_b, s5_d, w_glu, w_s5_out, w_ft_out, w_out):
    b, l, _ = p.shape
    dtype = p.dtype
    d_skip = s5_d.astype(jnp.float32).reshape(S5_GROUPS, S5_GROUP_DIM)
    y = s5_readout(*st_f, *c_f) + s5_readout(*st_b, *c_b)[:, ::-1] + d_skip * ug
    y = jax.nn.gelu(y.reshape(b, l, D_S5).astype(dtype))
    y_s5 = (y * jax.nn.sigmoid(y @ w_glu)) @ w_s5_out
    y_ft = fourier_mix(p[..., D_S5:D_S5 + D_FT]) @ w_ft_out
    g_s5 = jax.nn.sigmoid(p[..., D_S5 + D_FT:D_S5 + D_FT + D_MODEL])
    g_ft = jax.nn.sigmoid(p[..., D_S5 + D_FT + D_MODEL:])
    return (g_s5 * y_s5 + g_ft * y_ft) @ w_out


def expert_choice_ffn(h, w_router, w_gate, w_up, w_down):
    b, n, _ = h.shape
    cap = CAPACITY_FACTOR * n // N_EXPERTS
    aff = jax.nn.softmax((h @ w_router).astype(jnp.float32), axis=-1)
    gates, idx = lax.top_k(jnp.swapaxes(aff, 1, 2), cap)
    bidx = jnp.arange(b)[:, None, None]
    xg = h[bidx, idx]
    hid = jax.nn.silu(jnp.einsum('becd,edf->becf', xg, w_gate)) * jnp.einsum('becd,edf->becf', xg, w_up)
    ye = jnp.einsum('becf,efd->becd', hid, w_down) * gates[..., None].astype(h.dtype)
    return jnp.zeros_like(h).at[bidx, idx].add(ye)


def setup_inputs(seed: int = 0) -> dict:
    key = jax.random.key(seed)
    k = jax.random.split(key, 26)
    d, g, p, n, e, f = D_MODEL, S5_GROUPS, S5_STATE, S5_GROUP_DIM, N_EXPERTS, D_EXPERT

    def nrm(i, shape, std):
        return std * jax.random.normal(k[i], shape, jnp.float32)

    state_idx = jnp.arange(p, dtype=jnp.float32)
    return {
        'x': nrm(0, (BATCH, SEQ, d), 1.0),
        'c': nrm(1, (BATCH, d), 1.0),
        'ctx': nrm(2, (BATCH, CTX_LEN, d), 1.0),
        'c_ctx': nrm(3, (d,), 1.0),
        'ada_w': nrm(4, (DEPTH, d, N_MOD * d), 0.5 * d ** -0.5),
        'ada_b': nrm(5, (DEPTH, N_MOD * d), 0.01),
        'norm_mix_g': 1.0 + nrm(6, (DEPTH, d), 0.02),
        'norm_ffn_g': 1.0 + nrm(7, (DEPTH, d), 0.02),
        'w_in': nrm(8, (DEPTH, d, D_IN), d ** -0.5),
        's5_lam_re': -0.5 + nrm(9, (DEPTH, 2, g, p), 0.01),
        's5_lam_im': math.pi * state_idx + nrm(10, (DEPTH, 2, g, p), 0.01),
        's5_log_dt': jax.random.uniform(k[11], (DEPTH, 2, g), jnp.float32, math.log(DT_MIN), math.log(DT_MAX)),
        's5_b_re': nrm(12, (DEPTH, 2, g, p, n), (2 * n) ** -0.5),
        's5_b_im': nrm(13, (DEPTH, 2, g, p, n), (2 * n) ** -0.5),
        's5_c_re': nrm(14, (DEPTH, 2, g, n, p), (2 * p) ** -0.5),
        's5_c_im': nrm(15, (DEPTH, 2, g, n, p), (2 * p) ** -0.5),
        's5_d': nrm(16, (DEPTH, D_S5), 1.0),
        'w_glu': nrm(17, (DEPTH, D_S5, D_S5), D_S5 ** -0.5),
        'w_s5_out': nrm(18, (DEPTH, D_S5, d), D_S5 ** -0.5),
        'w_ft_out': nrm(19, (DEPTH, D_FT, d), D_FT ** -0.5),
        'w_out': nrm(20, (DEPTH, d, d), d ** -0.5),
        'w_router': nrm(21, (DEPTH, d, e), d ** -0.5),
        'w_gate': nrm(22, (DEPTH, e, d, f), d ** -0.5),
        'w_up': nrm(23, (DEPTH, e, d, f), d ** -0.5),
        'w_down': nrm(24, (DEPTH, e, f, d), f ** -0.5),
        'norm_final_g': 1.0 + nrm(25, (d,), 0.02),
    }


def reference(x, c, ctx, c_ctx, ada_w, ada_b, norm_mix_g, norm_ffn_g, w_in,
              s5_lam_re, s5_lam_im, s5_log_dt, s5_b_re, s5_b_im, s5_c_re, s5_c_im, s5_d,
              w_glu, w_s5_out, w_ft_out, w_out, w_router, w_gate, w_up, w_down, norm_final_g):
    x = x + grid_posembed(x.shape[1], x.shape[2]).astype(x.dtype)[None]
    silu_c = jax.nn.silu(c)
    silu_cc = jax.nn.silu(c_ctx)
    for i in range(DEPTH):
        last = i == DEPTH - 1
        mod_x = (silu_c @ ada_w[i] + ada_b[i])[:, None, :]
        mod_c = (silu_cc @ ada_w[i] + ada_b[i])[None, None, :]
        sh_mx, sc_mx, g_mx, sh_fx, sc_fx, g_fx = jnp.split(mod_x, N_MOD, axis=-1)
        sh_mc, sc_mc, g_mc, sh_fc, sc_fc, g_fc = jnp.split(mod_c, N_MOD, axis=-1)
        disc_f = s5_discretise(s5_lam_re[i, 0], s5_lam_im[i, 0], s5_log_dt[i, 0], s5_b_re[i, 0], s5_b_im[i, 0])
        disc_b = s5_discretise(s5_lam_re[i, 1], s5_lam_im[i, 1], s5_log_dt[i, 1], s5_b_re[i, 1], s5_b_im[i, 1])
        c_f = (s5_c_re[i, 0], s5_c_im[i, 0])
        c_b = (s5_c_re[i, 1], s5_c_im[i, 1])
        mixer_w = (s5_d[i], w_glu[i], w_s5_out[i], w_ft_out[i], w_out[i])

        hc = modulate(rmsnorm(ctx, norm_mix_g[i]), sh_mc, sc_mc)
        pc = hc @ w_in[i][:, :D_S5] if last else hc @ w_in[i]
        ug_c, st_fc, st_bc = s5_bidirectional(pc[..., :D_S5], disc_f, disc_b, None, None)
        h0_f = (st_fc[0][:, -1], st_fc[1][:, -1])
        h0_b = (st_bc[0][:, -1], st_bc[1][:, -1])

        hx = modulate(rmsnorm(x, norm_mix_g[i]), sh_mx, sc_mx)
        px = hx @ w_in[i]
        ug_x, st_fx, st_bx = s5_bidirectional(px[..., :D_S5], disc_f, disc_b, h0_f, h0_b)
        x = x + g_mx * hybrid_mixer(px, ug_x, st_fx, st_bx, c_f, c_b, *mixer_w)
        hx2 = modulate(rmsnorm(x, norm_ffn_g[i]), sh_fx, sc_fx)
        x = x + g_fx * expert_choice_ffn(hx2, w_router[i], w_gate[i], w_up[i], w_down[i])

        if not last:
            ctx = ctx + g_mc * hybrid_mixer(pc, ug_c, st_fc, st_bc, c_f, c_b, *mixer_w)
            hc2 = modulate(rmsnorm(ctx, norm_ffn_g[i]), sh_fc, sc_fc)
            ctx = ctx + g_fc * expert_choice_ffn(hc2, w_router[i], w_gate[i], w_up[i], w_down[i])
    return rmsnorm(x, norm_final_g)
```

```python
import functools
import math

import jax
import jax.numpy as jnp
from jax import lax
from jax.experimental import pallas as pl
from jax.experimental.pallas import tpu as pltpu

BF16 = jnp.bfloat16
F32 = jnp.float32
HI = lax.Precision.HIGHEST

OP = dict(
    s5_group_dim=16,
    s5_state=64,
    ft_groups=4,
    capacity_factor=2,
    n_mod=6,
    rms_eps=1e-6,
    pos_base=10000.0,
    grid_w=64,
    lambda_re_max=-1e-4,
)

V7X_VMEM_BYTES = 64 * 1024 * 1024
V7X_LANES = 128
S5_TILE_GROUPS = V7X_LANES // 16
S5_CHUNK = 16
CUMSUM_BLOCK = 256


def _cparams(sem, vmem_est):
    limit = int(min(max(vmem_est * 5 // 4 + (4 << 20), 32 << 20), V7X_VMEM_BYTES - (6 << 20)))
    return pltpu.CompilerParams(dimension_semantics=sem, vmem_limit_bytes=limit)


def _nbytes(shape, dtype):
    return math.prod(shape) * jnp.dtype(dtype).itemsize


def _adaln_kernel(c_ref, w_ref, b_ref, o_ref):
    cv = c_ref[...]
    a = (cv * jax.nn.sigmoid(cv)).astype(BF16)
    o_ref[0] = jnp.dot(a, w_ref[0].astype(BF16), preferred_element_type=F32) + b_ref[0]


def _adaln(c8, ada_w, ada_b, tn=512):
    depth, d, n6 = ada_w.shape
    rows = c8.shape[0]
    tn = min(tn, n6)
    est = 2 * _nbytes((d, tn), F32) + 4 * _nbytes((rows, tn), F32) + _nbytes((rows, d), F32) * 2
    return pl.pallas_call(
        _adaln_kernel,
        grid=(depth, n6 // tn),
        in_specs=[
            pl.BlockSpec((rows, d), lambda l, j: (0, 0)),
            pl.BlockSpec((1, d, tn), lambda l, j: (l, 0, j)),
            pl.BlockSpec((1, 1, tn), lambda l, j: (l, 0, j)),
        ],
        out_specs=pl.BlockSpec((1, rows, tn), lambda l, j: (l, 0, j)),
        out_shape=jax.ShapeDtypeStruct((depth, rows, n6), F32),
        compiler_params=_cparams(("parallel", "parallel"), est),
        name="adaln",
    )(c8, ada_w, ada_b.reshape(depth, 1, n6))


def _norm_kernel(*refs, eps, modulate, router):
    x_ref, g_ref = refs[0], refs[1]
    k = 2
    xf = x_ref[...]
    ms = jnp.mean(xf * xf, axis=-1, keepdims=True)
    y = xf * lax.rsqrt(ms + eps) * g_ref[...]
    if modulate:
        sh_ref, sc_ref = refs[k], refs[k + 1]
        k += 2
        y = y * (1.0 + sc_ref[0]) + sh_ref[0]
    if router:
        wr_ref = refs[k]
        k += 1
    o_ref = refs[k]
    o_ref[...] = y.astype(o_ref.dtype)
    if router:
        lg_ref = refs[k + 1]
        lg_ref[...] = lax.dot_general(
            wr_ref[...], y.astype(BF16), (((1,), (1,)), ((), ())), preferred_element_type=F32)


def _norm(x2, g, shift=None, scale=None, rows_per_batch=None, wr_t=None, out_dtype=BF16, tm=256):
    m, d = x2.shape
    modulate = shift is not None
    router = wr_t is not None
    rpb = rows_per_batch if rows_per_batch is not None else m
    tm = min(tm, rpb)
    tpb = rpb // tm
    in_specs = [pl.BlockSpec((tm, d), lambda i: (i, 0)), pl.BlockSpec((1, d), lambda i: (0, 0))]
    args = [x2, g.reshape(1, d)]
    if modulate:
        in_specs += [pl.BlockSpec((1, 1, d), lambda i: (i // tpb, 0, 0))] * 2
        args += [shift, scale]
    out_specs = [pl.BlockSpec((tm, d), lambda i: (i, 0))]
    out_shape = [jax.ShapeDtypeStruct((m, d), out_dtype)]
    if router:
        e = wr_t.shape[0]
        in_specs.append(pl.BlockSpec((e, d), lambda i: (0, 0)))
        args.append(wr_t)
        out_specs.append(pl.BlockSpec((e, tm), lambda i: (0, i)))
        out_shape.append(jax.ShapeDtypeStruct((e, m), F32))
    est = 2 * _nbytes((tm, d), F32) + 2 * _nbytes((tm, d), out_dtype) + 3 * _nbytes((tm, d), F32)
    res = pl.pallas_call(
        functools.partial(_norm_kernel, eps=OP["rms_eps"], modulate=modulate, router=router),
        grid=(m // tm,),
        in_specs=in_specs,
        out_specs=out_specs,
        out_shape=out_shape,
        compiler_params=_cparams(("parallel",), est),
        name="rmsnorm",
    )(*args)
    return res if router else res[0]


def _mm_kernel(*refs, n_pairs, kinds, epilogue):
    accs = []
    for k in range(n_pairs):
        a = refs[2 * k][...].astype(BF16)
        w = refs[2 * k + 1][...].astype(BF16)
        accs.append(jnp.dot(a, w, preferred_element_type=F32))
    ex = []
    for k, kind in enumerate(kinds):
        r = refs[2 * n_pairs + k]
        ex.append(r[0] if kind == "row" else r[...])
    o_ref = refs[-1]
    o_ref[...] = epilogue(accs, ex).astype(o_ref.dtype)


def _mm(pairs, extras, epilogue, n_out, out_dtype, tm, tn, rows_per_batch=None, alias_extra=None,
        name="mm"):
    m = pairs[0][0].shape[0]
    rpb = rows_per_batch if rows_per_batch is not None else m
    tm = min(tm, rpb)
    tn = min(tn, n_out)
    tpb = rpb // tm
    in_specs, args = [], []
    est = 2 * _nbytes((tm, tn), out_dtype) + 2 * _nbytes((tm, tn), F32) * max(1, len(pairs))
    for a, w, off in pairs:
        kdim = a.shape[1]
        in_specs.append(pl.BlockSpec((tm, kdim), lambda i, j: (i, 0)))
        in_specs.append(pl.BlockSpec((kdim, tn), lambda i, j, off=off: (0, j + off)))
        args += [a, w]
        est += 2 * _nbytes((tm, kdim), a.dtype) + 2 * _nbytes((kdim, tn), w.dtype)
        if w.dtype != BF16:
            est += _nbytes((kdim, tn), BF16)
    kinds = []
    for kind, arr, off in extras:
        kinds.append(kind)
        if kind == "tile":
            in_specs.append(pl.BlockSpec((tm, tn), lambda i, j, off=off: (i, j + off)))
            est += 2 * _nbytes((tm, tn), arr.dtype)
        else:
            in_specs.append(pl.BlockSpec((1, 1, tn), lambda i, j, off=off: (i // tpb, 0, j + off)))
        args.append(arr)
    aliases = {}
    if alias_extra is not None:
        aliases = {2 * len(pairs) + alias_extra: 0}
    return pl.pallas_call(
        functools.partial(_mm_kernel, n_pairs=len(pairs), kinds=tuple(kinds), epilogue=epilogue),
        grid=(m // tm, n_out // tn),
        in_specs=in_specs,
        out_specs=pl.BlockSpec((tm, tn), lambda i, j: (i, j)),
        out_shape=jax.ShapeDtypeStruct((m, n_out), out_dtype),
        input_output_aliases=aliases,
        compiler_params=_cparams(("parallel", "parallel"), est),
        name=name,
    )(*args)


def _s5_operators(lam_re, lam_im, log_dt, b_re, b_im, c_re, c_im, d_skip, t):
    n = OP["s5_group_dim"]
    lr = jnp.minimum(lam_re.astype(F32), OP["lambda_re_max"])
    li = lam_im.astype(F32)
    dt = jnp.exp(log_dt.astype(F32))[..., None]
    g, p = lr.shape[1], lr.shape[2]
    gt = S5_TILE_GROUPS
    j = g // gt
    mag = jnp.exp(lr * dt)
    a_re = mag * jnp.cos(li * dt)
    a_im = mag * jnp.sin(li * dt)
    num_re, num_im = a_re - 1.0, a_im
    den = lr * lr + li * li
    f_re = (num_re * lr + num_im * li) / den
    f_im = (num_im * lr - num_re * li) / den
    br, bi = b_re.astype(F32), b_im.astype(F32)
    bb_re = f_re[..., None] * br - f_im[..., None] * bi
    bb_im = f_re[..., None] * bi + f_im[..., None] * br
    cr, ci = c_re.astype(F32), c_im.astype(F32)

    k = jnp.arange(t + 1, dtype=F32)[:, None, None, None]
    pw_mag = jnp.exp(lr[None] * dt[None] * k)
    pw_re = pw_mag * jnp.cos(li[None] * dt[None] * k)
    pw_im = pw_mag * jnp.sin(li[None] * dt[None] * k)

    ca_re = cr[None] * pw_re[:t, :, :, None, :] - ci[None] * pw_im[:t, :, :, None, :]
    ca_im = cr[None] * pw_im[:t, :, :, None, :] + ci[None] * pw_re[:t, :, :, None, :]
    kk = (jnp.einsum("tdgnp,dgpm->tdgnm", ca_re, bb_re, precision=HI)
          - jnp.einsum("tdgnp,dgpm->tdgnm", ca_im, bb_im, precision=HI))
    kf, kb = kk[:, 0], kk[:, 1]
    k0 = kf[0] + kb[0] + d_skip.astype(F32).reshape(g, n)[:, :, None] * jnp.eye(n, dtype=F32)
    kall = jnp.concatenate([kb[1:][::-1], k0[None], kf[1:]], axis=0)
    idx = jnp.arange(t)[None, :] - jnp.arange(t)[:, None] + (t - 1)
    m5 = kall[idx]
    eye = jnp.eye(gt, dtype=F32)
    m5 = m5.reshape(t, t, j, gt, n, n)
    mt = jnp.einsum("stjgnm,gh->jsgmthn", m5, eye).reshape(j, t * gt * n, t * gt * n)

    ps_re = jnp.stack([pw_re[:t][::-1, 0], pw_re[:t, 1]], axis=0)
    ps_im = jnp.stack([pw_im[:t][::-1, 0], pw_im[:t, 1]], axis=0)
    w_re = ps_re[..., None] * bb_re[:, None] - ps_im[..., None] * bb_im[:, None]
    w_im = ps_re[..., None] * bb_im[:, None] + ps_im[..., None] * bb_re[:, None]
    w6 = jnp.stack([w_re, w_im], axis=1).reshape(2, 2, t, j, gt, p, n)
    wst = jnp.einsum("drsjgpm,gh->jsgmdrhp", w6, eye).reshape(j, t * gt * n, 4 * gt * p)

    po_re = jnp.stack([pw_re[1:, 0], pw_re[1:][::-1, 1]], axis=0)
    po_im = jnp.stack([pw_im[1:, 0], pw_im[1:][::-1, 1]], axis=0)
    co_re = cr[:, None] * po_re[:, :, :, None, :] - ci[:, None] * po_im[:, :, :, None, :]
    co_im = cr[:, None] * po_im[:, :, :, None, :] + ci[:, None] * po_re[:, :, :, None, :]
    o6 = jnp.stack([co_re, -co_im], axis=1).reshape(2, 2, t, j, gt, n, p)
    wout = jnp.einsum("drtjgnp,gh->jdrgpthn", o6, eye).reshape(j, 4 * gt * p, t * gt * n)

    at = jnp.stack([pw_re[t, 0], pw_im[t, 0], pw_re[t, 1], pw_im[t, 1]], axis=0)
    at = at.reshape(4, j, gt * p).transpose(1, 0, 2).reshape(j, 1, 4 * gt * p)
    return mt.astype(BF16), wst.astype(BF16), wout.astype(BF16), at


def _gelu_tanh(y):
    return 0.5 * y * (1.0 + jnp.tanh(0.7978845608028654 * (y + 0.044715 * y * y * y)))


def _s5_kernel(u_ref, mt_ref, wst_ref, wout_ref, at_ref, h0_ref, o_ref, hfin_ref, s_scr, hin_scr,
               *, n_chunks, half):
    u = u_ref[0, 0]
    s_scr[...] = jnp.dot(u, wst_ref[0], preferred_element_type=F32)
    at = at_ref[0]
    afr, afi = at[:, 0:half], at[:, half:2 * half]
    abr, abi = at[:, 2 * half:3 * half], at[:, 3 * half:4 * half]
    h0 = h0_ref[0, 0]
    init = (h0[:, 0:half], h0[:, half:2 * half], h0[:, 2 * half:3 * half], h0[:, 3 * half:4 * half])

    def body(i, carry):
        hfr, hfi, hbr, hbi = carry
        cb = n_chunks - 1 - i
        hin_scr[pl.ds(i, 1), 0:half] = hfr
        hin_scr[pl.ds(i, 1), half:2 * half] = hfi
        hin_scr[pl.ds(cb, 1), 2 * half:3 * half] = hbr
        hin_scr[pl.ds(cb, 1), 3 * half:4 * half] = hbi
        sf = s_scr[pl.ds(i, 1), 0:2 * half]
        sb = s_scr[pl.ds(cb, 1), 2 * half:4 * half]
        nfr = afr * hfr - afi * hfi + sf[:, 0:half]
        nfi = afr * hfi + afi * hfr + sf[:, half:2 * half]
        nbr = abr * hbr - abi * hbi + sb[:, 0:half]
        nbi = abr * hbi + abi * hbr + sb[:, half:2 * half]
        return nfr, nfi, nbr, nbi

    hfr, hfi, hbr, hbi = lax.fori_loop(0, n_chunks, body, init)
    hfin_ref[0, 0] = jnp.concatenate([hfr, hfi, hbr, hbi], axis=1)
    y = (jnp.dot(u, mt_ref[0], preferred_element_type=F32)
         + jnp.dot(hin_scr[...].astype(BF16), wout_ref[0], preferred_element_type=F32))
    o_ref[0, 0] = _gelu_tanh(y).astype(o_ref.dtype)


def _s5(u2, batch, ops, h0):
    mt, wst, wout, at = ops
    m, ds5 = u2.shape
    l = m // batch
    t = S5_CHUNK
    c = l // t
    j = ds5 // V7X_LANES
    w = t * V7X_LANES
    sw = wst.shape[2]
    ut = u2.reshape(batch, c, t, j, V7X_LANES).transpose(3, 0, 1, 2, 4).reshape(j, batch, c, w)
    est = (3 * _nbytes((w, sw), BF16) + 4 * _nbytes((c, w), BF16) + 2 * _nbytes((c, sw), F32)
           + 3 * _nbytes((c, w), F32) + _nbytes((c, sw), BF16))
    single = pl.Buffered(1)
    a_t, hfin = pl.pallas_call(
        functools.partial(_s5_kernel, n_chunks=c, half=sw // 4),
        grid=(j, batch),
        in_specs=[
            pl.BlockSpec((1, 1, c, w), lambda jj, b: (jj, b, 0, 0)),
            pl.BlockSpec((1, w, w), lambda jj, b: (jj, 0, 0), pipeline_mode=single),
            pl.BlockSpec((1, w, sw), lambda jj, b: (jj, 0, 0), pipeline_mode=single),
            pl.BlockSpec((1, sw, w), lambda jj, b: (jj, 0, 0), pipeline_mode=single),
            pl.BlockSpec((1, 1, sw), lambda jj, b: (jj, 0, 0)),
            pl.BlockSpec((1, 1, 1, sw), lambda jj, b: (jj, b, 0, 0)),
        ],
        out_specs=[
            pl.BlockSpec((1, 1, c, w), lambda jj, b: (jj, b, 0, 0)),
            pl.BlockSpec((1, 1, 1, sw), lambda jj, b: (jj, b, 0, 0)),
        ],
        out_shape=[
            jax.ShapeDtypeStruct((j, batch, c, w), BF16),
            jax.ShapeDtypeStruct((j, batch, 1, sw), F32),
        ],
        scratch_shapes=[pltpu.VMEM((c, sw), F32), pltpu.VMEM((c, sw), F32)],
        compiler_params=_cparams(("arbitrary", "arbitrary"), est),
        name="s5",
    )(ut, mt, wst, wout, at, h0)
    a2 = a_t.reshape(j, batch, c, t, V7X_LANES).transpose(1, 2, 3, 0, 4).reshape(m, ds5)
    return a2, hfin


def _dft_mats(n):
    k = jnp.arange(n, dtype=jnp.int32)
    idx = (k[:, None] * k[None, :]) % n
    ang = idx.astype(F32) * (2.0 * math.pi / n)
    s = 1.0 / math.sqrt(n)
    return (jnp.cos(ang) * s).astype(BF16), (jnp.sin(ang) * s).astype(BF16)


def _chan_dft_kernel(v_ref, w_ref, xc_ref, xs_ref, *, gd):
    r = jnp.dot(v_ref[...], w_ref[...], preferred_element_type=F32)
    xc_ref[...] = r[:, :gd].astype(xc_ref.dtype)
    xs_ref[...] = r[:, gd:].astype(xs_ref.dtype)


def _chan_dft(v2, wc, tm=1024):
    m, dft = v2.shape
    gd = wc.shape[0]
    tm = min(tm, m)
    spec = pl.BlockSpec((tm, gd), lambda i, g: (i, g))
    est = 6 * _nbytes((tm, gd), BF16) + 2 * _nbytes((gd, 2 * gd), BF16) + 2 * _nbytes((tm, 2 * gd), F32)
    return pl.pallas_call(
        functools.partial(_chan_dft_kernel, gd=gd),
        grid=(m // tm, dft // gd),
        in_specs=[spec, pl.BlockSpec((gd, 2 * gd), lambda i, g: (0, 0))],
        out_specs=[spec, spec],
        out_shape=[jax.ShapeDtypeStruct((m, dft), BF16)] * 2,
        compiler_params=_cparams(("parallel", "parallel"), est),
        name="chan_dft",
    )(v2, wc)


def _seq_dft_kernel(cl_ref, sl_ref, xc_ref, xs_ref, o_ref):
    y = (jnp.dot(cl_ref[...], xc_ref[0], preferred_element_type=F32)
         - jnp.dot(sl_ref[...], xs_ref[0], preferred_element_type=F32))
    o_ref[0] = y.astype(o_ref.dtype)


def _seq_dft(xc, xs, cl, sl, batch, tm=512, tn=512):
    m, dft = xc.shape
    l = m // batch
    tm = min(tm, l)
    tn = min(tn, dft)
    xc3, xs3 = xc.reshape(batch, l, dft), xs.reshape(batch, l, dft)
    est = 4 * _nbytes((tm, l), BF16) + 4 * _nbytes((l, tn), BF16) + 4 * _nbytes((tm, tn), F32)
    out = pl.pallas_call(
        _seq_dft_kernel,
        grid=(batch, dft // tn, l // tm),
        in_specs=[
            pl.BlockSpec((tm, l), lambda b, jn, i: (i, 0)),
            pl.BlockSpec((tm, l), lambda b, jn, i: (i, 0)),
            pl.BlockSpec((1, l, tn), lambda b, jn, i: (b, 0, jn)),
            pl.BlockSpec((1, l, tn), lambda b, jn, i: (b, 0, jn)),
        ],
        out_specs=pl.BlockSpec((1, tm, tn), lambda b, jn, i: (b, i, jn)),
        out_shape=jax.ShapeDtypeStruct((batch, l, dft), BF16),
        compiler_params=_cparams(("parallel", "parallel", "parallel"), est),
        name="seq_dft",
    )(cl, sl, xc3, xs3)
    return out.reshape(m, dft)


def _prefix_count(mask_f32, tri):
    e, l = mask_f32.shape
    carry = jnp.zeros((e, 1), F32)
    outs = []
    for k in range(l // CUMSUM_BLOCK):
        blk = mask_f32[:, k * CUMSUM_BLOCK:(k + 1) * CUMSUM_BLOCK]
        outs.append(jnp.dot(blk.astype(BF16), tri, preferred_element_type=F32) + carry)
        carry = carry + jnp.sum(blk, axis=1, keepdims=True)
    return outs[0] if len(outs) == 1 else jnp.concatenate(outs, axis=1)


def _route_kernel(lg_ref, posm_ref, aff_ref, *, cap):
    lg = lg_ref[...]
    mx = jnp.max(lg, axis=0, keepdims=True)
    ex = jnp.exp(lg - mx)
    aff = ex / jnp.sum(ex, axis=0, keepdims=True)
    bits = pltpu.bitcast(aff, jnp.int32)
    e = lg.shape[0]
    v = jnp.zeros((e, 1), jnp.int32)
    for bit in range(30, -1, -1):
        cand = v | (1 << bit)
        cnt = jnp.sum(jnp.where(bits >= cand, 1.0, 0.0), axis=1, keepdims=True)
        v = jnp.where(cnt >= cap, cand, v)
    gt = bits > v
    eq = bits == v
    n_gt = jnp.sum(gt.astype(F32), axis=1, keepdims=True)
    r = lax.broadcasted_iota(jnp.int32, (CUMSUM_BLOCK, CUMSUM_BLOCK), 0)
    c = lax.broadcasted_iota(jnp.int32, (CUMSUM_BLOCK, CUMSUM_BLOCK), 1)
    tri = jnp.where(r < c, 1.0, 0.0).astype(BF16)
    eq_f = jnp.where(eq, 1.0, 0.0)
    tie_ok = _prefix_count(eq_f, tri) < (cap - n_gt)
    sel_f = jnp.where(gt, 1.0, jnp.where(tie_ok, eq_f, 0.0))
    pos = _prefix_count(sel_f, tri)
    posm_ref[0] = jnp.where(sel_f > 0.5, pos, -1.0).astype(jnp.int32)
    aff_ref[0] = aff


def _route(lg_t, batch, cap):
    e, m = lg_t.shape
    l = m // batch
    est = 16 * _nbytes((e, l), F32)
    return pl.pallas_call(
        functools.partial(_route_kernel, cap=cap),
        grid=(batch,),
        in_specs=[pl.BlockSpec((e, l), lambda b: (0, b))],
        out_specs=[pl.BlockSpec((1, e, l), lambda b: (b, 0, 0))] * 2,
        out_shape=[jax.ShapeDtypeStruct((batch, e, l), jnp.int32),
                   jax.ShapeDtypeStruct((batch, e, l), F32)],
        compiler_params=_cparams(("parallel",), est),
        name="route",
    )(lg_t)


def _gather_kernel(posm_ref, aff_ref, h_ref, xg_ref, gs_ref, p_scr, *, cap):
    @pl.when(pl.program_id(2) == 0)
    def _():
        pm = posm_ref[0, 0]
        slot = lax.broadcasted_iota(jnp.int32, p_scr.shape, 0)
        hit = pm == slot
        p_scr[...] = jnp.where(hit, 1.0, 0.0).astype(BF16)
        gs_ref[0, 0] = jnp.sum(jnp.where(hit, aff_ref[0, 0], 0.0), axis=1, keepdims=True)

    xg_ref[0, 0] = jnp.dot(p_scr[...], h_ref[0], preferred_element_type=F32).astype(xg_ref.dtype)


def _gather(posm, aff, h2, cap, tn=512):
    batch, e, l = posm.shape
    d = h2.shape[1]
    tn = min(tn, d)
    est = (2 * _nbytes((l, tn), BF16) + _nbytes((cap, l), BF16) + 3 * _nbytes((cap, l), F32)
           + 4 * _nbytes((cap, tn), F32))
    return pl.pallas_call(
        functools.partial(_gather_kernel, cap=cap),
        grid=(batch, e, d // tn),
        in_specs=[
            pl.BlockSpec((1, 1, 1, l), lambda b, ee, j: (b, ee, 0, 0)),
            pl.BlockSpec((1, 1, 1, l), lambda b, ee, j: (b, ee, 0, 0)),
            pl.BlockSpec((1, l, tn), lambda b, ee, j: (b, 0, j)),
        ],
        out_specs=[
            pl.BlockSpec((1, 1, cap, tn), lambda b, ee, j: (b, ee, 0, j)),
            pl.BlockSpec((1, 1, cap, 1), lambda b, ee, j: (b, ee, 0, 0)),
        ],
        out_shape=[jax.ShapeDtypeStruct((batch, e, cap, d), BF16),
                   jax.ShapeDtypeStruct((batch, e, cap, 1), F32)],
        scratch_shapes=[pltpu.VMEM((cap, l), BF16)],
        compiler_params=_cparams(("parallel", "parallel", "arbitrary"), est),
        name="moe_gather",
    )(posm.reshape(batch, e, 1, l), aff.reshape(batch, e, 1, l), h2.reshape(batch, l, d))


def _up_kernel(xg_ref, wg_ref, wu_ref, o_ref):
    x = xg_ref[0, 0]
    g = jnp.dot(x, wg_ref[0].astype(BF16), preferred_element_type=F32)
    u = jnp.dot(x, wu_ref[0].astype(BF16), preferred_element_type=F32)
    o_ref[0, 0] = (g * jax.nn.sigmoid(g) * u).astype(o_ref.dtype)


def _expert_up(xg, w_gate, w_up, tf=256):
    batch, e, cap, d = xg.shape
    f = w_gate.shape[2]
    tf = min(tf, f)
    est = (4 * _nbytes((d, tf), F32) + 2 * _nbytes((d, tf), BF16) + 2 * _nbytes((cap, d), BF16)
           + 6 * _nbytes((cap, tf), F32))
    return pl.pallas_call(
        _up_kernel,
        grid=(e, f // tf, batch),
        in_specs=[
            pl.BlockSpec((1, 1, cap, d), lambda ee, fj, b: (b, ee, 0, 0)),
            pl.BlockSpec((1, d, tf), lambda ee, fj, b: (ee, 0, fj)),
            pl.BlockSpec((1, d, tf), lambda ee, fj, b: (ee, 0, fj)),
        ],
        out_specs=pl.BlockSpec((1, 1, cap, tf), lambda ee, fj, b: (b, ee, 0, fj)),
        out_shape=jax.ShapeDtypeStruct((batch, e, cap, f), BF16),
        compiler_params=_cparams(("parallel", "parallel", "arbitrary"), est),
        name="moe_up",
    )(xg, w_gate, w_up)


def _down_kernel(h_ref, wd_ref, gs_ref, o_ref):
    y = jnp.dot(h_ref[0, 0], wd_ref[0].astype(BF16), preferred_element_type=F32)
    o_ref[0] = (y * gs_ref[0, 0]).astype(o_ref.dtype)


def _expert_down(hid, w_down, gslot, td=1024):
    batch, e, cap, f = hid.shape
    d = w_down.shape[2]
    td = min(td, d)
    est = (2 * _nbytes((f, td), F32) + _nbytes((f, td), BF16) + 2 * _nbytes((cap, f), BF16)
           + 4 * _nbytes((cap, td), F32))
    return pl.pallas_call(
        _down_kernel,
        grid=(e, d // td, batch),
        in_specs=[
            pl.BlockSpec((1, 1, cap, f), lambda ee, dj, b: (b, ee, 0, 0)),
            pl.BlockSpec((1, f, td), lambda ee, dj, b: (ee, 0, dj)),
            pl.BlockSpec((1, 1, cap, 1), lambda ee, dj, b: (b, ee, 0, 0)),
        ],
        out_specs=pl.BlockSpec((1, cap, td), lambda ee, dj, b: (b, ee, dj)),
        out_shape=jax.ShapeDtypeStruct((batch, e * cap, d), BF16),
        compiler_params=_cparams(("parallel", "parallel", "arbitrary"), est),
        name="moe_down",
    )(hid, w_down, gslot)


def _combine_kernel(pt_ref, y_ref, x_ref, g_ref, o_ref, p_scr, *, n_exp, cap):
    @pl.when(pl.program_id(2) == 0)
    def _():
        pt = pt_ref[...]
        slot = lax.broadcasted_iota(jnp.int32, (pt.shape[0], cap), 1)
        for ee in range(n_exp):
            p_scr[:, ee * cap:(ee + 1) * cap] = jnp.where(
                pt[:, ee:ee + 1] == slot, 1.0, 0.0).astype(BF16)

    acc = jnp.dot(p_scr[...], y_ref[0], preferred_element_type=F32)
    o_ref[...] = x_ref[...] + g_ref[0] * acc


def _combine(posm_t, y, x2, gate, batch, cap, tm=512, tn=512):
    m, d = x2.shape
    l = m // batch
    e = posm_t.shape[1]
    tm = min(tm, l)
    tn = min(tn, d)
    tpb = l // tm
    est = (2 * _nbytes((e * cap, tn), BF16) + _nbytes((tm, e * cap), BF16) + 6 * _nbytes((tm, tn), F32)
           + 2 * _nbytes((tm, V7X_LANES), F32))
    return pl.pallas_call(
        functools.partial(_combine_kernel, n_exp=e, cap=cap),
        grid=(batch, tpb, d // tn),
        in_specs=[
            pl.BlockSpec((tm, e), lambda b, i, j: (b * tpb + i, 0)),
            pl.BlockSpec((1, e * cap, tn), lambda b, i, j: (b, 0, j)),
            pl.BlockSpec((tm, tn), lambda b, i, j: (b * tpb + i, j)),
            pl.BlockSpec((1, 1, tn), lambda b, i, j: (b, 0, j)),
        ],
        out_specs=pl.BlockSpec((tm, tn), lambda b, i, j: (b * tpb + i, j)),
        out_shape=jax.ShapeDtypeStruct((m, d), F32),
        scratch_shapes=[pltpu.VMEM((tm, e * cap), BF16)],
        input_output_aliases={2: 0},
        compiler_params=_cparams(("parallel", "parallel", "arbitrary"), est),
        name="moe_combine",
    )(posm_t, y, x2, gate)


def _sigmoid(z):
    return jax.nn.sigmoid(z)


def _in_proj(h2, w_in_bf, d_s5, d_ft, d):
    tn_s = min(512, d_s5)
    first = lambda accs, ex: accs[0]
    u = _mm([(h2, w_in_bf, 0)], [], first, d_s5, BF16, 1024, tn_s, name="in_s5")
    if d_ft is None:
        return u, None, None
    v = _mm([(h2, w_in_bf, d_s5 // tn_s)], [], first, d_ft, BF16, 1024, tn_s, name="in_ft")
    tn_g = math.gcd(512, d_s5 + d_ft, d)
    pg = _mm([(h2, w_in_bf, (d_s5 + d_ft) // tn_g)], [], first, 2 * d, F32, 1024, tn_g, name="in_gates")
    return u, v, pg


def _mixer_out(x2, a2, v2, pg, g_mix, wts, dfts, batch, rpb):
    w_glu, w_s5o, w_fto, w_out = wts
    wc, cl, sl = dfts
    d = x2.shape[1]
    d_s5 = a2.shape[1]
    xc, xs = _chan_dft(v2, wc)
    yf = _seq_dft(xc, xs, cl, sl, batch)
    glu = _mm([(a2, w_glu, 0)], [("tile", a2, 0)],
              lambda accs, ex: ex[0].astype(F32) * _sigmoid(accs[0]),
              d_s5, BF16, 1024, 512, name="glu")
    tn = min(512, d)
    merged = _mm([(glu, w_s5o, 0), (yf, w_fto, 0)],
                 [("tile", pg, 0), ("tile", pg, d // tn)],
                 lambda accs, ex: _sigmoid(ex[0]) * accs[0] + _sigmoid(ex[1]) * accs[1],
                 d, BF16, 1024, tn, name="merge")
    return _mm([(merged, w_out, 0)], [("tile", x2, 0), ("row", g_mix, 0)],
               lambda accs, ex: ex[0] + ex[1] * accs[0],
               d, F32, 1024, tn, rows_per_batch=rpb, alias_extra=0, name="out_proj")


def _moe(x2, h2, lg_t, g_ffn, w_gate, w_up, w_down, batch):
    m, d = x2.shape
    l = m // batch
    e = lg_t.shape[0]
    cap = OP["capacity_factor"] * l // e
    posm, aff = _route(lg_t, batch, cap)
    xg, gslot = _gather(posm, aff, h2, cap)
    hid = _expert_up(xg, w_gate, w_up)
    y = _expert_down(hid, w_down, gslot)
    posm_t = posm.transpose(0, 2, 1).reshape(m, e)
    return _combine(posm_t, y, x2, g_ffn, batch, cap)


def _grid_posembed(n_tok, d):
    gw = OP["grid_w"]
    rows = n_tok // gw
    row = jnp.broadcast_to(jnp.arange(rows, dtype=F32)[:, None], (rows, gw)).reshape(-1)
    col = jnp.broadcast_to(jnp.arange(gw, dtype=F32)[None, :], (rows, gw)).reshape(-1)
    quarter = d // 4
    inv_freq = OP["pos_base"] ** (-jnp.arange(quarter, dtype=F32) / quarter)
    ang_r = row[:, None] * inv_freq
    ang_c = col[:, None] * inv_freq
    return jnp.concatenate([jnp.sin(ang_r), jnp.cos(ang_r), jnp.sin(ang_c), jnp.cos(ang_c)], axis=-1)


def kernel(x, c, ctx, c_ctx, ada_w, ada_b, norm_mix_g, norm_ffn_g, w_in, s5_lam_re, s5_lam_im,
           s5_log_dt, s5_b_re, s5_b_im, s5_c_re, s5_c_im, s5_d, w_glu, w_s5_out, w_ft_out, w_out,
           w_router, w_gate, w_up, w_down, norm_final_g):
    batch, seq, d = x.shape
    ctx_len = ctx.shape[1]
    depth = ada_w.shape[0]
    d_s5 = w_glu.shape[1]
    d_ft = w_ft_out.shape[1]
    n_mod = OP["n_mod"]
    m_x, m_c = batch * seq, batch * ctx_len

    x2 = (x + _grid_posembed(seq, d).astype(x.dtype)[None]).reshape(m_x, d)
    c2 = ctx.reshape(m_c, d)

    rows = -(-(batch + 1) // 8) * 8
    c8 = jnp.zeros((rows, d), F32).at[:batch].set(c).at[batch].set(c_ctx)
    mod = _adaln(c8, ada_w, ada_b)

    gd = d_ft // OP["ft_groups"]
    wc_c, wc_s = _dft_mats(gd)
    wc = jnp.concatenate([wc_c, wc_s], axis=1)
    dft_x = (wc,) + _dft_mats(seq)
    dft_c = (wc,) + _dft_mats(ctx_len)

    sw = 4 * S5_TILE_GROUPS * OP["s5_state"]
    h_zero = jnp.zeros((d_s5 // V7X_LANES, batch, 1, sw), F32)

    for i in range(depth):
        last = i == depth - 1
        mx = [mod[i, :batch, k * d:(k + 1) * d].reshape(batch, 1, d) for k in range(n_mod)]
        mc = [jnp.broadcast_to(mod[i, batch:batch + 1, k * d:(k + 1) * d].reshape(1, 1, d),
                               (batch, 1, d)) for k in range(n_mod)]
        w_in_bf = w_in[i].astype(BF16)
        mix_w = (w_glu[i].astype(BF16), w_s5_out[i].astype(BF16), w_ft_out[i].astype(BF16),
                 w_out[i].astype(BF16))
        wr_t = w_router[i].T.astype(BF16)
        ops = _s5_operators(s5_lam_re[i], s5_lam_im[i], s5_log_dt[i], s5_b_re[i], s5_b_im[i],
                            s5_c_re[i], s5_c_im[i], s5_d[i], S5_CHUNK)

        hc = _norm(c2, norm_mix_g[i], mc[0], mc[1], rows_per_batch=ctx_len)
        uc, vc, pgc = _in_proj(hc, w_in_bf, d_s5, None if last else d_ft, d)
        ac, h_ctx = _s5(uc, batch, ops, h_zero)

        hx = _norm(x2, norm_mix_g[i], mx[0], mx[1], rows_per_batch=seq)
        ux, vx, pgx = _in_proj(hx, w_in_bf, d_s5, d_ft, d)
        ax, _ = _s5(ux, batch, ops, h_ctx)
        x2 = _mixer_out(x2, ax, vx, pgx, mx[2], mix_w, dft_x, batch, seq)
        hx2, lgx = _norm(x2, norm_ffn_g[i], mx[3], mx[4], rows_per_batch=seq, wr_t=wr_t)
        x2 = _moe(x2, hx2, lgx, mx[5], w_gate[i], w_up[i], w_down[i], batch)

        if not last:
            c2 = _mixer_out(c2, ac, vc, pgc, mc[2], mix_w, dft_c, batch, ctx_len)
            hc2, lgc = _norm(c2, norm_ffn_g[i], mc[3], mc[4], rows_per_batch=ctx_len, wr_t=wr_t)
            c2 = _moe(c2, hc2, lgc, mc[5], w_gate[i], w_up[i], w_down[i], batch)

    out = _norm(x2, norm_final_g, out_dtype=x.dtype)
    return out.reshape(batch, seq, d)
```

```python
import functools
import math

import jax
import jax.numpy as jnp
from jax import lax
from jax.experimental import pallas as pl
from jax.experimental.pallas import tpu as pltpu

BF16 = jnp.bfloat16
F32 = jnp.float32
HI = lax.Precision.HIGHEST

OP = dict(
    s5_group_dim=16,
    s5_state=64,
    ft_groups=4,
    capacity_factor=2,
    n_mod=6,
    rms_eps=1e-6,
    pos_base=10000.0,
    grid_w=64,
    lambda_re_max=-1e-4,
)

V7X_VMEM_BYTES = 64 * 1024 * 1024
V7X_LANES = 128
S5_TILE_GROUPS = V7X_LANES // 16
S5_CHUNK = 16
CUMSUM_BLOCK = 256


def _cparams(sem, vmem_est):
    limit = int(min(max(vmem_est * 5 // 4 + (4 << 20), 32 << 20), V7X_VMEM_BYTES - (6 << 20)))
    return pltpu.CompilerParams(dimension_semantics=sem, vmem_limit_bytes=limit)


def _nbytes(shape, dtype):
    return math.prod(shape) * jnp.dtype(dtype).itemsize


def _adaln_kernel(c_ref, w_ref, b_ref, o_ref):
    cv = c_ref[...]
    a = (cv * jax.nn.sigmoid(cv)).astype(BF16)
    o_ref[0] = jnp.dot(a, w_ref[0].astype(BF16), preferred_element_type=F32) + b_ref[0]


def _adaln(c8, ada_w, ada_b, tn=512):
    depth, d, n6 = ada_w.shape
    rows = c8.shape[0]
    tn = min(tn, n6)
    est = 2 * _nbytes((d, tn), F32) + 4 * _nbytes((rows, tn), F32) + _nbytes((rows, d), F32) * 2
    return pl.pallas_call(
        _adaln_kernel,
        grid=(depth, n6 // tn),
        in_specs=[
            pl.BlockSpec((rows, d), lambda l, j: (0, 0)),
            pl.BlockSpec((1, d, tn), lambda l, j: (l, 0, j)),
            pl.BlockSpec((1, 1, tn), lambda l, j: (l, 0, j)),
        ],
        out_specs=pl.BlockSpec((1, rows, tn), lambda l, j: (l, 0, j)),
        out_shape=jax.ShapeDtypeStruct((depth, rows, n6), F32),
        compiler_params=_cparams(("parallel", "parallel"), est),
        name="adaln",
    )(c8, ada_w, ada_b.reshape(depth, 1, n6))


def _norm_kernel(*refs, eps, modulate, router, add_pos):
    x_ref, g_ref = refs[0], refs[1]
    k = 2
    xf = x_ref[...]
    if add_pos:
        xf = xf + refs[k][...]
        k += 1
        refs[-1][...] = xf
    ms = jnp.mean(xf * xf, axis=-1, keepdims=True)
    y = xf * lax.rsqrt(ms + eps) * g_ref[...]
    if modulate:
        sh_ref, sc_ref = refs[k], refs[k + 1]
        k += 2
        y = y * (1.0 + sc_ref[0]) + sh_ref[0]
    if router:
        wr_ref = refs[k]
        k += 1
    o_ref = refs[k]
    o_ref[...] = y.astype(o_ref.dtype)
    if router:
        lg_ref = refs[k + 1]
        lg_ref[...] = lax.dot_general(
            wr_ref[...], y.astype(BF16), (((1,), (1,)), ((), ())), preferred_element_type=F32)


def _norm(x2, g, shift=None, scale=None, rows_per_batch=None, wr_t=None, out_dtype=BF16, tm=256,
          pos=None):
    m, d = x2.shape
    modulate = shift is not None
    router = wr_t is not None
    add_pos = pos is not None
    rpb = rows_per_batch if rows_per_batch is not None else m
    tm = min(tm, rpb)
    tpb = rpb // tm
    in_specs = [pl.BlockSpec((tm, d), lambda i: (i, 0)), pl.BlockSpec((1, d), lambda i: (0, 0))]
    args = [x2, g.reshape(1, d)]
    if add_pos:
        in_specs.append(pl.BlockSpec((tm, d), lambda i: (i % tpb, 0)))
        args.append(pos)
    if modulate:
        in_specs += [pl.BlockSpec((1, 1, d), lambda i: (i // tpb, 0, 0))] * 2
        args += [shift, scale]
    out_specs = [pl.BlockSpec((tm, d), lambda i: (i, 0))]
    out_shape = [jax.ShapeDtypeStruct((m, d), out_dtype)]
    if router:
        e = wr_t.shape[0]
        in_specs.append(pl.BlockSpec((e, d), lambda i: (0, 0)))
        args.append(wr_t)
        out_specs.append(pl.BlockSpec((e, tm), lambda i: (0, i)))
        out_shape.append(jax.ShapeDtypeStruct((e, m), F32))
    est = 2 * _nbytes((tm, d), F32) + 2 * _nbytes((tm, d), out_dtype) + 3 * _nbytes((tm, d), F32)
    if add_pos:
        out_specs.append(pl.BlockSpec((tm, d), lambda i: (i, 0)))
        out_shape.append(jax.ShapeDtypeStruct((m, d), F32))
        est += 4 * _nbytes((tm, d), F32)
    res = pl.pallas_call(
        functools.partial(_norm_kernel, eps=OP["rms_eps"], modulate=modulate, router=router,
                          add_pos=add_pos),
        grid=(m // tm,),
        in_specs=in_specs,
        out_specs=out_specs,
        out_shape=out_shape,
        compiler_params=_cparams(("parallel",), est),
        name="rmsnorm",
    )(*args)
    return res if (router or add_pos) else res[0]


def _mm_kernel(*refs, n_pairs, kinds, epilogue):
    accs = []
    for k in range(n_pairs):
        a = refs[2 * k][...].astype(BF16)
        w = refs[2 * k + 1][...].astype(BF16)
        accs.append(jnp.dot(a, w, preferred_element_type=F32))
    ex = []
    for k, kind in enumerate(kinds):
        r = refs[2 * n_pairs + k]
        ex.append(r[0] if kind == "row" else r[...])
    o_ref = refs[-1]
    o_ref[...] = epilogue(accs, ex).astype(o_ref.dtype)


def _mm(pairs, extras, epilogue, n_out, out_dtype, tm, tn, rows_per_batch=None, alias_extra=None,
        name="mm"):
    m = pairs[0][0].shape[0]
    rpb = rows_per_batch if rows_per_batch is not None else m
    tm = min(tm, rpb)
    tn = min(tn, n_out)
    tpb = rpb // tm
    in_specs, args = [], []
    est = 2 * _nbytes((tm, tn), out_dtype) + 2 * _nbytes((tm, tn), F32) * max(1, len(pairs))
    for a, w, off in pairs:
        kdim = a.shape[1]
        in_specs.append(pl.BlockSpec((tm, kdim), lambda i, j: (i, 0)))
        in_specs.append(pl.BlockSpec((kdim, tn), lambda i, j, off=off: (0, j + off)))
        args += [a, w]
        est += 2 * _nbytes((tm, kdim), a.dtype) + 2 * _nbytes((kdim, tn), w.dtype)
        if w.dtype != BF16:
            est += _nbytes((kdim, tn), BF16)
    kinds = []
    for kind, arr, off in extras:
        kinds.append(kind)
        if kind == "tile":
            in_specs.append(pl.BlockSpec((tm, tn), lambda i, j, off=off: (i, j + off)))
            est += 2 * _nbytes((tm, tn), arr.dtype)
        else:
            in_specs.append(pl.BlockSpec((1, 1, tn), lambda i, j, off=off: (i // tpb, 0, j + off)))
        args.append(arr)
    aliases = {}
    if alias_extra is not None:
        aliases = {2 * len(pairs) + alias_extra: 0}
    return pl.pallas_call(
        functools.partial(_mm_kernel, n_pairs=len(pairs), kinds=tuple(kinds), epilogue=epilogue),
        grid=(m // tm, n_out // tn),
        in_specs=in_specs,
        out_specs=pl.BlockSpec((tm, tn), lambda i, j: (i, j)),
        out_shape=jax.ShapeDtypeStruct((m, n_out), out_dtype),
        input_output_aliases=aliases,
        compiler_params=_cparams(("parallel", "parallel"), est),
        name=name,
    )(*args)


def _s5_operators(lam_re, lam_im, log_dt, b_re, b_im, c_re, c_im, d_skip, t):
    n = OP["s5_group_dim"]
    lr = jnp.minimum(lam_re.astype(F32), OP["lambda_re_max"])
    li = lam_im.astype(F32)
    dt = jnp.exp(log_dt.astype(F32))[..., None]
    g, p = lr.shape[1], lr.shape[2]
    gt = S5_TILE_GROUPS
    j = g // gt
    mag = jnp.exp(lr * dt)
    a_re = mag * jnp.cos(li * dt)
    a_im = mag * jnp.sin(li * dt)
    num_re, num_im = a_re - 1.0, a_im
    den = lr * lr + li * li
    f_re = (num_re * lr + num_im * li) / den
    f_im = (num_im * lr - num_re * li) / den
    br, bi = b_re.astype(F32), b_im.astype(F32)
    bb_re = f_re[..., None] * br - f_im[..., None] * bi
    bb_im = f_re[..., None] * bi + f_im[..., None] * br
    cr, ci = c_re.astype(F32), c_im.astype(F32)

    k = jnp.arange(t + 1, dtype=F32)[:, None, None, None]
    pw_mag = jnp.exp(lr[None] * dt[None] * k)
    pw_re = pw_mag * jnp.cos(li[None] * dt[None] * k)
    pw_im = pw_mag * jnp.sin(li[None] * dt[None] * k)

    ca_re = cr[None] * pw_re[:t, :, :, None, :] - ci[None] * pw_im[:t, :, :, None, :]
    ca_im = cr[None] * pw_im[:t, :, :, None, :] + ci[None] * pw_re[:t, :, :, None, :]
    bt_re, bt_im = bb_re.transpose(0, 1, 3, 2), bb_im.transpose(0, 1, 3, 2)
    kk = jnp.sum(ca_re[:, :, :, :, None, :] * bt_re[None, :, :, None, :, :]
                 - ca_im[:, :, :, :, None, :] * bt_im[None, :, :, None, :, :], axis=-1)
    kf, kb = kk[:, 0], kk[:, 1]
    k0 = kf[0] + kb[0] + d_skip.astype(F32).reshape(g, n)[:, :, None] * jnp.eye(n, dtype=F32)
    kall = jnp.concatenate([kb[1:][::-1], k0[None], kf[1:]], axis=0)
    kc = kall.reshape(2 * t - 1, j, gt, n, n).transpose(1, 0, 4, 2, 3)
    kc = kc.reshape(j, 2 * t - 1, n, gt * n).astype(BF16)
    lane = jnp.arange(gt * n)
    same_group = (lane[:, None] // n) == (lane[None, :] // n)
    bd = jnp.where(same_group, jnp.tile(kc, (1, 1, gt, 1)), 0)
    mt = jnp.stack([bd[:, t - 1 - s:2 * t - 1 - s] for s in range(t)], axis=1)
    mt = mt.transpose(0, 1, 3, 2, 4).reshape(j, t * gt * n, t * gt * n)

    ps_re = jnp.stack([pw_re[:t][::-1, 0], pw_re[:t, 1]], axis=0)
    ps_im = jnp.stack([pw_im[:t][::-1, 0], pw_im[:t, 1]], axis=0)
    w_re = ps_re[..., None] * bb_re[:, None] - ps_im[..., None] * bb_im[:, None]
    w_im = ps_re[..., None] * bb_im[:, None] + ps_im[..., None] * bb_re[:, None]
    w6 = jnp.stack([w_re, w_im], axis=1).reshape(2, 2, t, j, gt, p, n)
    wc = w6.transpose(3, 2, 6, 0, 1, 4, 5).reshape(j, t, n, 4 * gt * p).astype(BF16)
    col_group = (jnp.arange(4 * gt * p) // p) % gt
    pick = jnp.arange(gt)[:, None, None] == col_group[None, None, :]
    wst = jnp.where(pick[None, None], wc[:, :, None, :, :], 0).reshape(j, t * gt * n, 4 * gt * p)

    po_re = jnp.stack([pw_re[1:, 0], pw_re[1:][::-1, 1]], axis=0)
    po_im = jnp.stack([pw_im[1:, 0], pw_im[1:][::-1, 1]], axis=0)
    co_re = cr[:, None] * po_re[:, :, :, None, :] - ci[:, None] * po_im[:, :, :, None, :]
    co_im = cr[:, None] * po_im[:, :, :, None, :] + ci[:, None] * po_re[:, :, :, None, :]
    o6 = jnp.stack([co_re, -co_im], axis=1).reshape(2, 2, t, j, gt, n, p)
    oc = o6.transpose(3, 0, 1, 6, 2, 4, 5).reshape(j, 4, p, t * gt * n).astype(BF16)
    out_group = (jnp.arange(t * gt * n) // n) % gt
    pick_o = jnp.arange(gt)[:, None, None] == out_group[None, None, :]
    wout = jnp.where(pick_o[None, None], oc[:, :, None, :, :], 0).reshape(j, 4 * gt * p, t * gt * n)

    at = jnp.stack([pw_re[t, 0], pw_im[t, 0], pw_re[t, 1], pw_im[t, 1]], axis=0)
    at = at.reshape(4, j, gt * p).transpose(1, 0, 2).reshape(j, 1, 4 * gt * p)
    return mt, wst, wout, at


def _gelu_tanh(y):
    return 0.5 * y * (1.0 + jnp.tanh(0.7978845608028654 * (y + 0.044715 * y * y * y)))


def _s5_kernel(u_ref, mt_ref, wst_ref, wout_ref, at_ref, h0_ref, o_ref, hfin_ref, s_scr, hin_scr,
               *, n_chunks, half, t):
    lanes = u_ref.shape[1]
    u = jnp.concatenate([u_ref[pl.ds(s, n_chunks, stride=t), :] for s in range(t)],
                        axis=1).astype(BF16)
    s_scr[...] = jnp.dot(u, wst_ref[0], preferred_element_type=F32)
    at = at_ref[0]
    afr, afi = at[:, 0:half], at[:, half:2 * half]
    abr, abi = at[:, 2 * half:3 * half], at[:, 3 * half:4 * half]
    h0 = h0_ref[0, 0]
    init = (h0[:, 0:half], h0[:, half:2 * half], h0[:, 2 * half:3 * half], h0[:, 3 * half:4 * half])

    def body(i, carry):
        hfr, hfi, hbr, hbi = carry
        cb = n_chunks - 1 - i
        hin_scr[pl.ds(i, 1), 0:half] = hfr
        hin_scr[pl.ds(i, 1), half:2 * half] = hfi
        hin_scr[pl.ds(cb, 1), 2 * half:3 * half] = hbr
        hin_scr[pl.ds(cb, 1), 3 * half:4 * half] = hbi
        sf = s_scr[pl.ds(i, 1), 0:2 * half]
        sb = s_scr[pl.ds(cb, 1), 2 * half:4 * half]
        nfr = afr * hfr - afi * hfi + sf[:, 0:half]
        nfi = afr * hfi + afi * hfr + sf[:, half:2 * half]
        nbr = abr * hbr - abi * hbi + sb[:, 0:half]
        nbi = abr * hbi + abi * hbr + sb[:, half:2 * half]
        return nfr, nfi, nbr, nbi

    hfr, hfi, hbr, hbi = lax.fori_loop(0, n_chunks, body, init)
    hfin_ref[0, 0] = jnp.concatenate([hfr, hfi, hbr, hbi], axis=1)
    y = (jnp.dot(u, mt_ref[0], preferred_element_type=F32)
         + jnp.dot(hin_scr[...].astype(BF16), wout_ref[0], preferred_element_type=F32))
    for k in range(t):
        o_ref[pl.ds(k, n_chunks, stride=t), :] = _gelu_tanh(
            y[:, k * lanes:(k + 1) * lanes]).astype(o_ref.dtype)


def _s5(u2, batch, ops, h0):
    mt, wst, wout, at = ops
    m, ds5 = u2.shape
    l = m // batch
    t = S5_CHUNK
    c = l // t
    j = ds5 // V7X_LANES
    w = t * V7X_LANES
    sw = wst.shape[2]
    est = (3 * _nbytes((w, sw), BF16) + 4 * _nbytes((l, V7X_LANES), F32) + 2 * _nbytes((c, sw), F32)
           + 3 * _nbytes((c, w), F32) + _nbytes((c, sw), BF16) + 2 * _nbytes((c, w), BF16))
    single = pl.Buffered(1)
    return pl.pallas_call(
        functools.partial(_s5_kernel, n_chunks=c, half=sw // 4, t=t),
        grid=(j, batch),
        in_specs=[
            pl.BlockSpec((l, V7X_LANES), lambda jj, b: (b, jj)),
            pl.BlockSpec((1, w, w), lambda jj, b: (jj, 0, 0), pipeline_mode=single),
            pl.BlockSpec((1, w, sw), lambda jj, b: (jj, 0, 0), pipeline_mode=single),
            pl.BlockSpec((1, sw, w), lambda jj, b: (jj, 0, 0), pipeline_mode=single),
            pl.BlockSpec((1, 1, sw), lambda jj, b: (jj, 0, 0)),
            pl.BlockSpec((1, 1, 1, sw), lambda jj, b: (jj, b, 0, 0)),
        ],
        out_specs=[
            pl.BlockSpec((l, V7X_LANES), lambda jj, b: (b, jj)),
            pl.BlockSpec((1, 1, 1, sw), lambda jj, b: (jj, b, 0, 0)),
        ],
        out_shape=[
            jax.ShapeDtypeStruct((m, ds5), F32),
            jax.ShapeDtypeStruct((j, batch, 1, sw), F32),
        ],
        scratch_shapes=[pltpu.VMEM((c, sw), F32), pltpu.VMEM((c, sw), F32)],
        compiler_params=_cparams(("arbitrary", "arbitrary"), est),
        name="s5",
    )(u2, mt, wst, wout, at, h0)


def _dft_mats(n):
    hi = n // V7X_LANES
    j = jnp.arange(n, dtype=jnp.int32)[:, None]
    a1 = ((j * jnp.arange(V7X_LANES, dtype=jnp.int32)[None, :]) % n).astype(F32) * (2.0 * math.pi / n)
    a2 = ((j * jnp.arange(hi, dtype=jnp.int32)[None, :]) % hi).astype(F32) * (2.0 * math.pi / hi)
    c1, s1 = jnp.cos(a1)[:, None, :], jnp.sin(a1)[:, None, :]
    c2, s2 = jnp.cos(a2)[:, :, None], jnp.sin(a2)[:, :, None]
    s = 1.0 / math.sqrt(n)
    cos = ((c2 * c1 - s2 * s1) * s).astype(BF16).reshape(n, n)
    sin = ((s2 * c1 + c2 * s1) * s).astype(BF16).reshape(n, n)
    return cos, sin


def _chan_dft_kernel(v_ref, w_ref, xc_ref, xs_ref, *, gd):
    r = jnp.dot(v_ref[...], w_ref[...], preferred_element_type=F32)
    xc_ref[...] = r[:, :gd].astype(xc_ref.dtype)
    xs_ref[...] = r[:, gd:].astype(xs_ref.dtype)


def _chan_dft(v2, wc, tm=1024):
    m, dft = v2.shape
    gd = wc.shape[0]
    tm = min(tm, m)
    spec = pl.BlockSpec((tm, gd), lambda i, g: (i, g))
    est = 6 * _nbytes((tm, gd), BF16) + 2 * _nbytes((gd, 2 * gd), BF16) + 2 * _nbytes((tm, 2 * gd), F32)
    return pl.pallas_call(
        functools.partial(_chan_dft_kernel, gd=gd),
        grid=(m // tm, dft // gd),
        in_specs=[spec, pl.BlockSpec((gd, 2 * gd), lambda i, g: (0, 0))],
        out_specs=[spec, spec],
        out_shape=[jax.ShapeDtypeStruct((m, dft), BF16)] * 2,
        compiler_params=_cparams(("parallel", "parallel"), est),
        name="chan_dft",
    )(v2, wc)


def _seq_dft_kernel(cl_ref, sl_ref, xc_ref, xs_ref, o_ref):
    y = (jnp.dot(cl_ref[...], xc_ref[0], preferred_element_type=F32)
         - jnp.dot(sl_ref[...], xs_ref[0], preferred_element_type=F32))
    o_ref[0] = y.astype(o_ref.dtype)


def _seq_dft(xc, xs, cl, sl, batch, tm=512, tn=512):
    m, dft = xc.shape
    l = m // batch
    tm = min(tm, l)
    tn = min(tn, dft)
    xc3, xs3 = xc.reshape(batch, l, dft), xs.reshape(batch, l, dft)
    est = 4 * _nbytes((tm, l), BF16) + 4 * _nbytes((l, tn), BF16) + 4 * _nbytes((tm, tn), F32)
    out = pl.pallas_call(
        _seq_dft_kernel,
        grid=(batch, dft // tn, l // tm),
        in_specs=[
            pl.BlockSpec((tm, l), lambda b, jn, i: (i, 0)),
            pl.BlockSpec((tm, l), lambda b, jn, i: (i, 0)),
            pl.BlockSpec((1, l, tn), lambda b, jn, i: (b, 0, jn)),
            pl.BlockSpec((1, l, tn), lambda b, jn, i: (b, 0, jn)),
        ],
        out_specs=pl.BlockSpec((1, tm, tn), lambda b, jn, i: (b, i, jn)),
        out_shape=jax.ShapeDtypeStruct((batch, l, dft), BF16),
        compiler_params=_cparams(("parallel", "parallel", "parallel"), est),
        name="seq_dft",
    )(cl, sl, xc3, xs3)
    return out.reshape(m, dft)


def _prefix_count(mask_f32, tri):
    e, l = mask_f32.shape
    carry = jnp.zeros((e, 1), F32)
    outs = []
    for k in range(l // CUMSUM_BLOCK):
        blk = mask_f32[:, k * CUMSUM_BLOCK:(k + 1) * CUMSUM_BLOCK]
        outs.append(jnp.dot(blk.astype(BF16), tri, preferred_element_type=F32) + carry)
        carry = carry + jnp.sum(blk, axis=1, keepdims=True)
    return outs[0] if len(outs) == 1 else jnp.concatenate(outs, axis=1)


def _route_kernel(lg_ref, posm_ref, aff_ref, *, cap):
    lg = lg_ref[...]
    mx = jnp.max(lg, axis=0, keepdims=True)
    ex = jnp.exp(lg - mx)
    aff = ex / jnp.sum(ex, axis=0, keepdims=True)
    bits = pltpu.bitcast(aff, jnp.int32)
    e = lg.shape[0]
    v = jnp.zeros((e, 1), jnp.int32)
    for bit in range(30, -1, -1):
        cand = v | (1 << bit)
        cnt = jnp.sum(jnp.where(bits >= cand, 1.0, 0.0), axis=1, keepdims=True)
        v = jnp.where(cnt >= cap, cand, v)
    gt = bits > v
    eq = bits == v
    n_gt = jnp.sum(gt.astype(F32), axis=1, keepdims=True)
    r = lax.broadcasted_iota(jnp.int32, (CUMSUM_BLOCK, CUMSUM_BLOCK), 0)
    c = lax.broadcasted_iota(jnp.int32, (CUMSUM_BLOCK, CUMSUM_BLOCK), 1)
    tri = jnp.where(r < c, 1.0, 0.0).astype(BF16)
    eq_f = jnp.where(eq, 1.0, 0.0)
    tie_ok = _prefix_count(eq_f, tri) < (cap - n_gt)
    sel_f = jnp.where(gt, 1.0, jnp.where(tie_ok, eq_f, 0.0))
    pos = _prefix_count(sel_f, tri)
    posm_ref[0] = jnp.where(sel_f > 0.5, pos, -1.0).astype(jnp.int32)
    aff_ref[0] = aff


def _route(lg_t, batch, cap):
    e, m = lg_t.shape
    l = m // batch
    est = 16 * _nbytes((e, l), F32)
    return pl.pallas_call(
        functools.partial(_route_kernel, cap=cap),
        grid=(batch,),
        in_specs=[pl.BlockSpec((e, l), lambda b: (0, b))],
        out_specs=[pl.BlockSpec((1, e, l), lambda b: (b, 0, 0))] * 2,
        out_shape=[jax.ShapeDtypeStruct((batch, e, l), jnp.int32),
                   jax.ShapeDtypeStruct((batch, e, l), F32)],
        compiler_params=_cparams(("parallel",), est),
        name="route",
    )(lg_t)


def _gather_kernel(posm_ref, aff_ref, h_ref, xg_ref, gs_ref, p_scr, *, cap):
    @pl.when(pl.program_id(2) == 0)
    def _():
        pm = posm_ref[0, 0]
        slot = lax.broadcasted_iota(jnp.int32, p_scr.shape, 0)
        hit = pm == slot
        p_scr[...] = jnp.where(hit, 1.0, 0.0).astype(BF16)
        gs_ref[0, 0] = jnp.sum(jnp.where(hit, aff_ref[0, 0], 0.0), axis=1, keepdims=True)

    xg_ref[0, 0] = jnp.dot(p_scr[...], h_ref[0], preferred_element_type=F32).astype(xg_ref.dtype)


def _gather(posm, aff, h2, cap, tn=512):
    batch, e, l = posm.shape
    d = h2.shape[1]
    tn = min(tn, d)
    est = (2 * _nbytes((l, tn), BF16) + _nbytes((cap, l), BF16) + 3 * _nbytes((cap, l), F32)
           + 4 * _nbytes((cap, tn), F32))
    return pl.pallas_call(
        functools.partial(_gather_kernel, cap=cap),
        grid=(batch, e, d // tn),
        in_specs=[
            pl.BlockSpec((1, 1, 1, l), lambda b, ee, j: (b, ee, 0, 0)),
            pl.BlockSpec((1, 1, 1, l), lambda b, ee, j: (b, ee, 0, 0)),
            pl.BlockSpec((1, l, tn), lambda b, ee, j: (b, 0, j)),
        ],
        out_specs=[
            pl.BlockSpec((1, 1, cap, tn), lambda b, ee, j: (b, ee, 0, j)),
            pl.BlockSpec((1, 1, cap, 1), lambda b, ee, j: (b, ee, 0, 0)),
        ],
        out_shape=[jax.ShapeDtypeStruct((batch, e, cap, d), BF16),
                   jax.ShapeDtypeStruct((batch, e, cap, 1), F32)],
        scratch_shapes=[pltpu.VMEM((cap, l), BF16)],
        compiler_params=_cparams(("parallel", "parallel", "arbitrary"), est),
        name="moe_gather",
    )(posm.reshape(batch, e, 1, l), aff.reshape(batch, e, 1, l), h2.reshape(batch, l, d))


def _up_kernel(xg_ref, wg_ref, wu_ref, o_ref):
    x = xg_ref[0, 0]
    g = jnp.dot(x, wg_ref[0, 0].astype(BF16), preferred_element_type=F32)
    u = jnp.dot(x, wu_ref[0, 0].astype(BF16), preferred_element_type=F32)
    o_ref[0, 0] = (g * jax.nn.sigmoid(g) * u).astype(o_ref.dtype)


def _expert_up(xg, w_gate, w_up, layer, tf=256):
    batch, e, cap, d = xg.shape
    f = w_gate.shape[3]
    tf = min(tf, f)
    est = (4 * _nbytes((d, tf), F32) + 2 * _nbytes((d, tf), BF16) + 2 * _nbytes((cap, d), BF16)
           + 6 * _nbytes((cap, tf), F32))
    return pl.pallas_call(
        _up_kernel,
        grid=(e, f // tf, batch),
        in_specs=[
            pl.BlockSpec((1, 1, cap, d), lambda ee, fj, b: (b, ee, 0, 0)),
            pl.BlockSpec((1, 1, d, tf), lambda ee, fj, b: (layer, ee, 0, fj)),
            pl.BlockSpec((1, 1, d, tf), lambda ee, fj, b: (layer, ee, 0, fj)),
        ],
        out_specs=pl.BlockSpec((1, 1, cap, tf), lambda ee, fj, b: (b, ee, 0, fj)),
        out_shape=jax.ShapeDtypeStruct((batch, e, cap, f), BF16),
        compiler_params=_cparams(("parallel", "parallel", "arbitrary"), est),
        name="moe_up",
    )(xg, w_gate, w_up)


def _down_kernel(h_ref, wd_ref, gs_ref, o_ref):
    y = jnp.dot(h_ref[0, 0], wd_ref[0, 0].astype(BF16), preferred_element_type=F32)
    o_ref[0] = (y * gs_ref[0, 0]).astype(o_ref.dtype)


def _expert_down(hid, w_down, gslot, layer, td=1024):
    batch, e, cap, f = hid.shape
    d = w_down.shape[3]
    td = min(td, d)
    est = (2 * _nbytes((f, td), F32) + _nbytes((f, td), BF16) + 2 * _nbytes((cap, f), BF16)
           + 4 * _nbytes((cap, td), F32))
    return pl.pallas_call(
        _down_kernel,
        grid=(e, d // td, batch),
        in_specs=[
            pl.BlockSpec((1, 1, cap, f), lambda ee, dj, b: (b, ee, 0, 0)),
            pl.BlockSpec((1, 1, f, td), lambda ee, dj, b: (layer, ee, 0, dj)),
            pl.BlockSpec((1, 1, cap, 1), lambda ee, dj, b: (b, ee, 0, 0)),
        ],
        out_specs=pl.BlockSpec((1, cap, td), lambda ee, dj, b: (b, ee, dj)),
        out_shape=jax.ShapeDtypeStruct((batch, e * cap, d), BF16),
        compiler_params=_cparams(("parallel", "parallel", "arbitrary"), est),
        name="moe_down",
    )(hid, w_down, gslot)


def _combine_kernel(pt_ref, y_ref, x_ref, g_ref, o_ref, p_scr, *, n_exp, cap):
    @pl.when(pl.program_id(2) == 0)
    def _():
        pt = pt_ref[...]
        slot = lax.broadcasted_iota(jnp.int32, (pt.shape[0], cap), 1)
        for ee in range(n_exp):
            p_scr[:, ee * cap:(ee + 1) * cap] = jnp.where(
                pt[:, ee:ee + 1] == slot, 1.0, 0.0).astype(BF16)

    acc = jnp.dot(p_scr[...], y_ref[0], preferred_element_type=F32)
    o_ref[...] = x_ref[...] + g_ref[0] * acc


def _combine(posm_t, y, x2, gate, batch, cap, tm=512, tn=512):
    m, d = x2.shape
    l = m // batch
    e = posm_t.shape[1]
    tm = min(tm, l)
    tn = min(tn, d)
    tpb = l // tm
    est = (2 * _nbytes((e * cap, tn), BF16) + _nbytes((tm, e * cap), BF16) + 6 * _nbytes((tm, tn), F32)
           + 2 * _nbytes((tm, V7X_LANES), F32))
    return pl.pallas_call(
        functools.partial(_combine_kernel, n_exp=e, cap=cap),
        grid=(batch, tpb, d // tn),
        in_specs=[
            pl.BlockSpec((tm, e), lambda b, i, j: (b * tpb + i, 0)),
            pl.BlockSpec((1, e * cap, tn), lambda b, i, j: (b, 0, j)),
            pl.BlockSpec((tm, tn), lambda b, i, j: (b * tpb + i, j)),
            pl.BlockSpec((1, 1, tn), lambda b, i, j: (b, 0, j)),
        ],
        out_specs=pl.BlockSpec((tm, tn), lambda b, i, j: (b * tpb + i, j)),
        out_shape=jax.ShapeDtypeStruct((m, d), F32),
        scratch_shapes=[pltpu.VMEM((tm, e * cap), BF16)],
        input_output_aliases={2: 0},
        compiler_params=_cparams(("parallel", "parallel", "arbitrary"), est),
        name="moe_combine",
    )(posm_t, y, x2, gate)


def _sigmoid(z):
    return jax.nn.sigmoid(z)


def _in_proj(h2, w_in_bf, d_s5, d_ft, d):
    tn_s = min(512, d_s5)
    first = lambda accs, ex: accs[0]
    u = _mm([(h2, w_in_bf, 0)], [], first, d_s5, F32, 1024, tn_s, name="in_s5")
    if d_ft is None:
        return u, None, None
    v = _mm([(h2, w_in_bf, d_s5 // tn_s)], [], first, d_ft, BF16, 1024, tn_s, name="in_ft")
    tn_g = math.gcd(512, d_s5 + d_ft, d)
    pg = _mm([(h2, w_in_bf, (d_s5 + d_ft) // tn_g)], [], first, 2 * d, F32, 1024, tn_g, name="in_gates")
    return u, v, pg


def _mixer_out(x2, a2, v2, pg, g_mix, wts, dfts, batch, rpb):
    w_glu, w_s5o, w_fto, w_out = wts
    wc, cl, sl = dfts
    d = x2.shape[1]
    d_s5 = a2.shape[1]
    xc, xs = _chan_dft(v2, wc)
    yf = _seq_dft(xc, xs, cl, sl, batch)
    glu = _mm([(a2, w_glu, 0)], [("tile", a2, 0)],
              lambda accs, ex: ex[0].astype(F32) * _sigmoid(accs[0]),
              d_s5, BF16, 1024, 512, name="glu")
    tn = min(512, d)
    merged = _mm([(glu, w_s5o, 0), (yf, w_fto, 0)],
                 [("tile", pg, 0), ("tile", pg, d // tn)],
                 lambda accs, ex: _sigmoid(ex[0]) * accs[0] + _sigmoid(ex[1]) * accs[1],
                 d, BF16, 1024, tn, name="merge")
    return _mm([(merged, w_out, 0)], [("tile", x2, 0), ("row", g_mix, 0)],
               lambda accs, ex: ex[0] + ex[1] * accs[0],
               d, F32, 1024, tn, rows_per_batch=rpb, alias_extra=0, name="out_proj")


def _moe(x2, h2, lg_t, g_ffn, w_gate, w_up, w_down, layer, batch):
    m, d = x2.shape
    l = m // batch
    e = lg_t.shape[0]
    cap = OP["capacity_factor"] * l // e
    posm, aff = _route(lg_t, batch, cap)
    xg, gslot = _gather(posm, aff, h2, cap)
    hid = _expert_up(xg, w_gate, w_up, layer)
    y = _expert_down(hid, w_down, gslot, layer)
    posm_t = posm.transpose(0, 2, 1).reshape(m, e)
    return _combine(posm_t, y, x2, g_ffn, batch, cap)


def _grid_posembed(n_tok, d):
    gw = OP["grid_w"]
    rows = n_tok // gw
    quarter = d // 4
    inv_freq = OP["pos_base"] ** (-jnp.arange(quarter, dtype=F32) / quarter)
    ang_r = jnp.arange(rows, dtype=F32)[:, None] * inv_freq
    ang_c = jnp.arange(gw, dtype=F32)[:, None] * inv_freq
    shape = (rows, gw, quarter)
    parts = [jnp.broadcast_to(jnp.sin(ang_r)[:, None, :], shape),
             jnp.broadcast_to(jnp.cos(ang_r)[:, None, :], shape),
             jnp.broadcast_to(jnp.sin(ang_c)[None, :, :], shape),
             jnp.broadcast_to(jnp.cos(ang_c)[None, :, :], shape)]
    return jnp.concatenate(parts, axis=-1).reshape(n_tok, d)


def kernel(x, c, ctx, c_ctx, ada_w, ada_b, norm_mix_g, norm_ffn_g, w_in, s5_lam_re, s5_lam_im,
           s5_log_dt, s5_b_re, s5_b_im, s5_c_re, s5_c_im, s5_d, w_glu, w_s5_out, w_ft_out, w_out,
           w_router, w_gate, w_up, w_down, norm_final_g):
    batch, seq, d = x.shape
    ctx_len = ctx.shape[1]
    depth = ada_w.shape[0]
    d_s5 = w_glu.shape[1]
    d_ft = w_ft_out.shape[1]
    n_mod = OP["n_mod"]
    m_x, m_c = batch * seq, batch * ctx_len

    x2 = x.reshape(m_x, d)
    pos = _grid_posembed(seq, d).astype(x.dtype)
    c2 = ctx.reshape(m_c, d)

    rows = -(-(batch + 1) // 8) * 8
    c8 = jnp.zeros((rows, d), F32).at[:batch].set(c).at[batch].set(c_ctx)
    mod = _adaln(c8, ada_w, ada_b)

    gd = d_ft // OP["ft_groups"]
    wc_c, wc_s = _dft_mats(gd)
    wc = jnp.concatenate([wc_c, wc_s], axis=1)
    dft_x = (wc,) + _dft_mats(seq)
    dft_c = (wc,) + _dft_mats(ctx_len)

    sw = 4 * S5_TILE_GROUPS * OP["s5_state"]
    h_zero = jnp.zeros((d_s5 // V7X_LANES, batch, 1, sw), F32)

    for i in range(depth):
        last = i == depth - 1
        mx = [mod[i, :batch, k * d:(k + 1) * d].reshape(batch, 1, d) for k in range(n_mod)]
        mc = [jnp.broadcast_to(mod[i, batch:batch + 1, k * d:(k + 1) * d].reshape(1, 1, d),
                               (batch, 1, d)) for k in range(n_mod)]
        w_in_bf = w_in[i].astype(BF16)
        mix_w = (w_glu[i].astype(BF16), w_s5_out[i].astype(BF16), w_ft_out[i].astype(BF16),
                 w_out[i].astype(BF16))
        wr_t = w_router[i].T.astype(BF16)
        ops = _s5_operators(s5_lam_re[i], s5_lam_im[i], s5_log_dt[i], s5_b_re[i], s5_b_im[i],
                            s5_c_re[i], s5_c_im[i], s5_d[i], S5_CHUNK)

        hc = _norm(c2, norm_mix_g[i], mc[0], mc[1], rows_per_batch=ctx_len)
        uc, vc, pgc = _in_proj(hc, w_in_bf, d_s5, None if last else d_ft, d)
        ac, h_ctx = _s5(uc, batch, ops, h_zero)

        if i == 0:
            hx, x2 = _norm(x2, norm_mix_g[i], mx[0], mx[1], rows_per_batch=seq, pos=pos)
        else:
            hx = _norm(x2, norm_mix_g[i], mx[0], mx[1], rows_per_batch=seq)
        ux, vx, pgx = _in_proj(hx, w_in_bf, d_s5, d_ft, d)
        ax, _ = _s5(ux, batch, ops, h_ctx)
        x2 = _mixer_out(x2, ax, vx, pgx, mx[2], mix_w, dft_x, batch, seq)
        hx2, lgx = _norm(x2, norm_ffn_g[i], mx[3], mx[4], rows_per_batch=seq, wr_t=wr_t)
        x2 = _moe(x2, hx2, lgx, mx[5], w_gate, w_up, w_down, i, batch)

        if not last:
            c2 = _mixer_out(c2, ac, vc, pgc, mc[2], mix_w, dft_c, batch, ctx_len)
            hc2, lgc = _norm(c2, norm_ffn_g[i], mc[3], mc[4], rows_per_batch=ctx_len, wr_t=wr_t)
            c2 = _moe(c2, hc2, lgc, mc[5], w_gate, w_up, w_down, i, batch)

    out = _norm(x2, norm_final_g, out_dtype=x.dtype)
    return out.reshape(batch, seq, d)
```

```python
import functools
import math

import jax
import jax.numpy as jnp
from jax import lax
from jax.experimental import pallas as pl
from jax.experimental.pallas import tpu as pltpu

BF16 = jnp.bfloat16
F32 = jnp.float32
HI = lax.Precision.HIGHEST

OP = dict(
    s5_group_dim=16,
    s5_state=64,
    ft_groups=4,
    capacity_factor=2,
    n_mod=6,
    rms_eps=1e-6,
    pos_base=10000.0,
    grid_w=64,
    lambda_re_max=-1e-4,
)

V7X_VMEM_BYTES = 64 * 1024 * 1024
V7X_LANES = 128
S5_TILE_GROUPS = V7X_LANES // 16
S5_CHUNK = 16
CUMSUM_BLOCK = 256


def _cparams(sem, vmem_est):
    limit = int(min(max(vmem_est * 5 // 4 + (4 << 20), 32 << 20), V7X_VMEM_BYTES - (6 << 20)))
    return pltpu.CompilerParams(dimension_semantics=sem, vmem_limit_bytes=limit)


def _nbytes(shape, dtype):
    return math.prod(shape) * jnp.dtype(dtype).itemsize


def _adaln_kernel(c_ref, w_ref, b_ref, o_ref):
    cv = c_ref[...]
    a = (cv * jax.nn.sigmoid(cv)).astype(BF16)
    o_ref[0] = jnp.dot(a, w_ref[0].astype(BF16), preferred_element_type=F32) + b_ref[0]


def _adaln(c8, ada_w, ada_b, tn=512):
    depth, d, n6 = ada_w.shape
    rows = c8.shape[0]
    tn = min(tn, n6)
    est = 2 * _nbytes((d, tn), F32) + 4 * _nbytes((rows, tn), F32) + _nbytes((rows, d), F32) * 2
    return pl.pallas_call(
        _adaln_kernel,
        grid=(depth, n6 // tn),
        in_specs=[
            pl.BlockSpec((rows, d), lambda l, j: (0, 0)),
            pl.BlockSpec((1, d, tn), lambda l, j: (l, 0, j)),
            pl.BlockSpec((1, 1, tn), lambda l, j: (l, 0, j)),
        ],
        out_specs=pl.BlockSpec((1, rows, tn), lambda l, j: (l, 0, j)),
        out_shape=jax.ShapeDtypeStruct((depth, rows, n6), F32),
        compiler_params=_cparams(("parallel", "parallel"), est),
        name="adaln",
    )(c8, ada_w, ada_b.reshape(depth, 1, n6))


def _rms_modulate(xf, g, shift, scale, eps):
    ms = jnp.mean(xf * xf, axis=-1, keepdims=True)
    y = xf * lax.rsqrt(ms + eps) * g
    if shift is not None:
        y = y * (1.0 + scale) + shift
    return y


def _norm_kernel(*refs, eps, modulate, router, add_pos, emit_h):
    x_ref, g_ref = refs[0], refs[1]
    k = 2
    xf = x_ref[...]
    if add_pos:
        xf = xf + refs[k][...]
        k += 1
        refs[-1][...] = xf
    shift = scale = None
    if modulate:
        shift, scale = refs[k][0], refs[k + 1][0]
        k += 2
    y = _rms_modulate(xf, g_ref[...], shift, scale, eps)
    if router:
        wr_ref = refs[k]
        k += 1
    if emit_h:
        refs[k][...] = y.astype(refs[k].dtype)
        k += 1
    if router:
        refs[k][...] = lax.dot_general(
            wr_ref[...], y.astype(BF16), (((1,), (1,)), ((), ())), preferred_element_type=F32)


def _norm(x2, g, shift=None, scale=None, rows_per_batch=None, wr_t=None, out_dtype=BF16, tm=256,
          pos=None, emit_h=True):
    m, d = x2.shape
    modulate = shift is not None
    router = wr_t is not None
    add_pos = pos is not None
    rpb = rows_per_batch if rows_per_batch is not None else m
    tm = min(tm, rpb)
    tpb = rpb // tm
    in_specs = [pl.BlockSpec((tm, d), lambda i: (i, 0)), pl.BlockSpec((1, d), lambda i: (0, 0))]
    args = [x2, g.reshape(1, d)]
    if add_pos:
        in_specs.append(pl.BlockSpec((tm, d), lambda i: (i % tpb, 0)))
        args.append(pos)
    if modulate:
        in_specs += [pl.BlockSpec((1, 1, d), lambda i: (i // tpb, 0, 0))] * 2
        args += [shift, scale]
    out_specs, out_shape = [], []
    if emit_h:
        out_specs.append(pl.BlockSpec((tm, d), lambda i: (i, 0)))
        out_shape.append(jax.ShapeDtypeStruct((m, d), out_dtype))
    if router:
        e = wr_t.shape[0]
        in_specs.append(pl.BlockSpec((e, d), lambda i: (0, 0)))
        args.append(wr_t)
        out_specs.append(pl.BlockSpec((e, tm), lambda i: (0, i)))
        out_shape.append(jax.ShapeDtypeStruct((e, m), F32))
    est = 2 * _nbytes((tm, d), F32) + 2 * _nbytes((tm, d), out_dtype) + 3 * _nbytes((tm, d), F32)
    if add_pos:
        out_specs.append(pl.BlockSpec((tm, d), lambda i: (i, 0)))
        out_shape.append(jax.ShapeDtypeStruct((m, d), F32))
        est += 4 * _nbytes((tm, d), F32)
    res = pl.pallas_call(
        functools.partial(_norm_kernel, eps=OP["rms_eps"], modulate=modulate, router=router,
                          add_pos=add_pos, emit_h=emit_h),
        grid=(m // tm,),
        in_specs=in_specs,
        out_specs=out_specs,
        out_shape=out_shape,
        compiler_params=_cparams(("parallel",), est),
        name="rmsnorm",
    )(*args)
    return res if len(res) > 1 else res[0]


def _mm_kernel(*refs, n_pairs, kinds, epilogue):
    accs = []
    for k in range(n_pairs):
        a = refs[2 * k][...].astype(BF16)
        w = refs[2 * k + 1][...].astype(BF16)
        accs.append(jnp.dot(a, w, preferred_element_type=F32))
    ex = []
    for k, kind in enumerate(kinds):
        r = refs[2 * n_pairs + k]
        ex.append(r[0] if kind == "row" else r[...])
    o_ref = refs[-1]
    o_ref[...] = epilogue(accs, ex).astype(o_ref.dtype)


def _mm(pairs, extras, epilogue, n_out, out_dtype, tm, tn, rows_per_batch=None, alias_extra=None,
        name="mm"):
    m = pairs[0][0].shape[0]
    rpb = rows_per_batch if rows_per_batch is not None else m
    tm = min(tm, rpb)
    tn = min(tn, n_out)
    tpb = rpb // tm
    in_specs, args = [], []
    est = 2 * _nbytes((tm, tn), out_dtype) + 2 * _nbytes((tm, tn), F32) * max(1, len(pairs))
    for a, w, off in pairs:
        kdim = a.shape[1]
        in_specs.append(pl.BlockSpec((tm, kdim), lambda i, j: (i, 0)))
        in_specs.append(pl.BlockSpec((kdim, tn), lambda i, j, off=off: (0, j + off)))
        args += [a, w]
        est += 2 * _nbytes((tm, kdim), a.dtype) + 2 * _nbytes((kdim, tn), w.dtype)
        if w.dtype != BF16:
            est += _nbytes((kdim, tn), BF16)
    kinds = []
    for kind, arr, off in extras:
        kinds.append(kind)
        if kind == "tile":
            in_specs.append(pl.BlockSpec((tm, tn), lambda i, j, off=off: (i, j + off)))
            est += 2 * _nbytes((tm, tn), arr.dtype)
        else:
            in_specs.append(pl.BlockSpec((1, 1, tn), lambda i, j, off=off: (i // tpb, 0, j + off)))
        args.append(arr)
    aliases = {}
    if alias_extra is not None:
        aliases = {2 * len(pairs) + alias_extra: 0}
    return pl.pallas_call(
        functools.partial(_mm_kernel, n_pairs=len(pairs), kinds=tuple(kinds), epilogue=epilogue),
        grid=(m // tm, n_out // tn),
        in_specs=in_specs,
        out_specs=pl.BlockSpec((tm, tn), lambda i, j: (i, j)),
        out_shape=jax.ShapeDtypeStruct((m, n_out), out_dtype),
        input_output_aliases=aliases,
        compiler_params=_cparams(("parallel", "parallel"), est),
        name=name,
    )(*args)


def _s5_operators(lam_re, lam_im, log_dt, b_re, b_im, c_re, c_im, d_skip, t):
    n = OP["s5_group_dim"]
    lr = jnp.minimum(lam_re.astype(F32), OP["lambda_re_max"])
    li = lam_im.astype(F32)
    dt = jnp.exp(log_dt.astype(F32))[..., None]
    g, p = lr.shape[1], lr.shape[2]
    gt = S5_TILE_GROUPS
    j = g // gt
    mag = jnp.exp(lr * dt)
    a_re = mag * jnp.cos(li * dt)
    a_im = mag * jnp.sin(li * dt)
    num_re, num_im = a_re - 1.0, a_im
    den = lr * lr + li * li
    f_re = (num_re * lr + num_im * li) / den
    f_im = (num_im * lr - num_re * li) / den
    br, bi = b_re.astype(F32), b_im.astype(F32)
    bb_re = f_re[..., None] * br - f_im[..., None] * bi
    bb_im = f_re[..., None] * bi + f_im[..., None] * br
    cr, ci = c_re.astype(F32), c_im.astype(F32)

    k = jnp.arange(t + 1, dtype=F32)[:, None, None, None]
    pw_mag = jnp.exp(lr[None] * dt[None] * k)
    pw_re = pw_mag * jnp.cos(li[None] * dt[None] * k)
    pw_im = pw_mag * jnp.sin(li[None] * dt[None] * k)

    ca_re = cr[None] * pw_re[:t, :, :, None, :] - ci[None] * pw_im[:t, :, :, None, :]
    ca_im = cr[None] * pw_im[:t, :, :, None, :] + ci[None] * pw_re[:t, :, :, None, :]
    bt_re, bt_im = bb_re.transpose(0, 1, 3, 2), bb_im.transpose(0, 1, 3, 2)
    kk = jnp.sum(ca_re[:, :, :, :, None, :] * bt_re[None, :, :, None, :, :]
                 - ca_im[:, :, :, :, None, :] * bt_im[None, :, :, None, :, :], axis=-1)
    kf, kb = kk[:, 0], kk[:, 1]
    k0 = kf[0] + kb[0] + d_skip.astype(F32).reshape(g, n)[:, :, None] * jnp.eye(n, dtype=F32)
    kall = jnp.concatenate([kb[1:][::-1], k0[None], kf[1:]], axis=0)
    kc = kall.reshape(2 * t - 1, j, gt, n, n).transpose(1, 0, 4, 2, 3)
    kc = kc.reshape(j, 2 * t - 1, n, gt * n).astype(BF16)
    lane = jnp.arange(gt * n)
    same_group = (lane[:, None] // n) == (lane[None, :] // n)
    bd = jnp.where(same_group, jnp.tile(kc, (1, 1, gt, 1)), 0)
    mt = jnp.stack([bd[:, t - 1 - s:2 * t - 1 - s] for s in range(t)], axis=1)
    mt = mt.transpose(0, 1, 3, 2, 4).reshape(j, t * gt * n, t * gt * n)

    ps_re = jnp.stack([pw_re[:t][::-1, 0], pw_re[:t, 1]], axis=0)
    ps_im = jnp.stack([pw_im[:t][::-1, 0], pw_im[:t, 1]], axis=0)
    w_re = ps_re[..., None] * bb_re[:, None] - ps_im[..., None] * bb_im[:, None]
    w_im = ps_re[..., None] * bb_im[:, None] + ps_im[..., None] * bb_re[:, None]
    w6 = jnp.stack([w_re, w_im], axis=1).reshape(2, 2, t, j, gt, p, n)
    wc = w6.transpose(3, 2, 6, 0, 1, 4, 5).reshape(j, t, n, 4 * gt * p).astype(BF16)
    col_group = (jnp.arange(4 * gt * p) // p) % gt
    pick = jnp.arange(gt)[:, None, None] == col_group[None, None, :]
    wst = jnp.where(pick[None, None], wc[:, :, None, :, :], 0).reshape(j, t * gt * n, 4 * gt * p)

    po_re = jnp.stack([pw_re[1:, 0], pw_re[1:][::-1, 1]], axis=0)
    po_im = jnp.stack([pw_im[1:, 0], pw_im[1:][::-1, 1]], axis=0)
    co_re = cr[:, None] * po_re[:, :, :, None, :] - ci[:, None] * po_im[:, :, :, None, :]
    co_im = cr[:, None] * po_im[:, :, :, None, :] + ci[:, None] * po_re[:, :, :, None, :]
    o6 = jnp.stack([co_re, -co_im], axis=1).reshape(2, 2, t, j, gt, n, p)
    oc = o6.transpose(3, 0, 1, 6, 2, 4, 5).reshape(j, 4, p, t * gt * n).astype(BF16)
    out_group = (jnp.arange(t * gt * n) // n) % gt
    pick_o = jnp.arange(gt)[:, None, None] == out_group[None, None, :]
    wout = jnp.where(pick_o[None, None], oc[:, :, None, :, :], 0).reshape(j, 4 * gt * p, t * gt * n)

    at = jnp.stack([pw_re[t, 0], pw_im[t, 0], pw_re[t, 1], pw_im[t, 1]], axis=0)
    at = at.reshape(4, j, gt * p).transpose(1, 0, 2).reshape(j, 1, 4 * gt * p)
    return mt, wst, wout, at


def _gelu_tanh(y):
    return 0.5 * y * (1.0 + jnp.tanh(0.7978845608028654 * (y + 0.044715 * y * y * y)))


def _s5_kernel(u_ref, mt_ref, wst_ref, wout_ref, at_ref, h0_ref, o_ref, hfin_ref, s_scr, hin_scr,
               *, n_chunks, half, t):
    lanes = u_ref.shape[1]
    u = jnp.concatenate([u_ref[pl.ds(s, n_chunks, stride=t), :] for s in range(t)],
                        axis=1).astype(BF16)
    s_scr[...] = jnp.dot(u, wst_ref[0], preferred_element_type=F32)
    at = at_ref[0]
    afr, afi = at[:, 0:half], at[:, half:2 * half]
    abr, abi = at[:, 2 * half:3 * half], at[:, 3 * half:4 * half]
    h0 = h0_ref[0, 0]
    init = (h0[:, 0:half], h0[:, half:2 * half], h0[:, 2 * half:3 * half], h0[:, 3 * half:4 * half])

    def body(i, carry):
        hfr, hfi, hbr, hbi = carry
        cb = n_chunks - 1 - i
        hin_scr[pl.ds(i, 1), 0:half] = hfr
        hin_scr[pl.ds(i, 1), half:2 * half] = hfi
        hin_scr[pl.ds(cb, 1), 2 * half:3 * half] = hbr
        hin_scr[pl.ds(cb, 1), 3 * half:4 * half] = hbi
        sf = s_scr[pl.ds(i, 1), 0:2 * half]
        sb = s_scr[pl.ds(cb, 1), 2 * half:4 * half]
        nfr = afr * hfr - afi * hfi + sf[:, 0:half]
        nfi = afr * hfi + afi * hfr + sf[:, half:2 * half]
        nbr = abr * hbr - abi * hbi + sb[:, 0:half]
        nbi = abr * hbi + abi * hbr + sb[:, half:2 * half]
        return nfr, nfi, nbr, nbi

    hfr, hfi, hbr, hbi = lax.fori_loop(0, n_chunks, body, init)
    hfin_ref[0, 0] = jnp.concatenate([hfr, hfi, hbr, hbi], axis=1)
    y = (jnp.dot(u, mt_ref[0], preferred_element_type=F32)
         + jnp.dot(hin_scr[...].astype(BF16), wout_ref[0], preferred_element_type=F32))
    for k in range(t):
        o_ref[pl.ds(k, n_chunks, stride=t), :] = _gelu_tanh(
            y[:, k * lanes:(k + 1) * lanes]).astype(o_ref.dtype)


def _s5(u2, batch, ops, h0):
    mt, wst, wout, at = ops
    m, ds5 = u2.shape
    l = m // batch
    t = S5_CHUNK
    c = l // t
    j = ds5 // V7X_LANES
    w = t * V7X_LANES
    sw = wst.shape[2]
    est = (3 * _nbytes((w, sw), BF16) + 4 * _nbytes((l, V7X_LANES), F32) + 2 * _nbytes((c, sw), F32)
           + 3 * _nbytes((c, w), F32) + _nbytes((c, sw), BF16) + 2 * _nbytes((c, w), BF16))
    single = pl.Buffered(1)
    return pl.pallas_call(
        functools.partial(_s5_kernel, n_chunks=c, half=sw // 4, t=t),
        grid=(j, batch),
        in_specs=[
            pl.BlockSpec((l, V7X_LANES), lambda jj, b: (b, jj)),
            pl.BlockSpec((1, w, w), lambda jj, b: (jj, 0, 0), pipeline_mode=single),
            pl.BlockSpec((1, w, sw), lambda jj, b: (jj, 0, 0), pipeline_mode=single),
            pl.BlockSpec((1, sw, w), lambda jj, b: (jj, 0, 0), pipeline_mode=single),
            pl.BlockSpec((1, 1, sw), lambda jj, b: (jj, 0, 0)),
            pl.BlockSpec((1, 1, 1, sw), lambda jj, b: (jj, b, 0, 0)),
        ],
        out_specs=[
            pl.BlockSpec((l, V7X_LANES), lambda jj, b: (b, jj)),
            pl.BlockSpec((1, 1, 1, sw), lambda jj, b: (jj, b, 0, 0)),
        ],
        out_shape=[
            jax.ShapeDtypeStruct((m, ds5), F32),
            jax.ShapeDtypeStruct((j, batch, 1, sw), F32),
        ],
        scratch_shapes=[pltpu.VMEM((c, sw), F32), pltpu.VMEM((c, sw), F32)],
        compiler_params=_cparams(("arbitrary", "arbitrary"), est),
        name="s5",
    )(u2, mt, wst, wout, at, h0)


def _dft_mats(n):
    hi = n // V7X_LANES
    j = jnp.arange(n, dtype=jnp.int32)[:, None]
    a1 = ((j * jnp.arange(V7X_LANES, dtype=jnp.int32)[None, :]) % n).astype(F32) * (2.0 * math.pi / n)
    a2 = ((j * jnp.arange(hi, dtype=jnp.int32)[None, :]) % hi).astype(F32) * (2.0 * math.pi / hi)
    c1, s1 = jnp.cos(a1)[:, None, :], jnp.sin(a1)[:, None, :]
    c2, s2 = jnp.cos(a2)[:, :, None], jnp.sin(a2)[:, :, None]
    s = 1.0 / math.sqrt(n)
    cos = ((c2 * c1 - s2 * s1) * s).astype(BF16).reshape(n, n)
    sin = ((s2 * c1 + c2 * s1) * s).astype(BF16).reshape(n, n)
    return cos, sin


def _chan_dft_kernel(v_ref, w_ref, xc_ref, xs_ref, *, gd):
    r = jnp.dot(v_ref[...], w_ref[...], preferred_element_type=F32)
    xc_ref[...] = r[:, :gd].astype(xc_ref.dtype)
    xs_ref[...] = r[:, gd:].astype(xs_ref.dtype)


def _chan_dft(v2, wc, tm=1024):
    m, dft = v2.shape
    gd = wc.shape[0]
    tm = min(tm, m)
    spec = pl.BlockSpec((tm, gd), lambda i, g: (i, g))
    est = 6 * _nbytes((tm, gd), BF16) + 2 * _nbytes((gd, 2 * gd), BF16) + 2 * _nbytes((tm, 2 * gd), F32)
    return pl.pallas_call(
        functools.partial(_chan_dft_kernel, gd=gd),
        grid=(m // tm, dft // gd),
        in_specs=[spec, pl.BlockSpec((gd, 2 * gd), lambda i, g: (0, 0))],
        out_specs=[spec, spec],
        out_shape=[jax.ShapeDtypeStruct((m, dft), BF16)] * 2,
        compiler_params=_cparams(("parallel", "parallel"), est),
        name="chan_dft",
    )(v2, wc)


def _seq_dft_kernel(cl_ref, sl_ref, xc_ref, xs_ref, o_ref):
    y = (jnp.dot(cl_ref[...], xc_ref[0], preferred_element_type=F32)
         - jnp.dot(sl_ref[...], xs_ref[0], preferred_element_type=F32))
    o_ref[0] = y.astype(o_ref.dtype)


def _seq_dft(xc, xs, cl, sl, batch, tm=512, tn=512):
    m, dft = xc.shape
    l = m // batch
    tm = min(tm, l)
    tn = min(tn, dft)
    xc3, xs3 = xc.reshape(batch, l, dft), xs.reshape(batch, l, dft)
    est = 4 * _nbytes((tm, l), BF16) + 4 * _nbytes((l, tn), BF16) + 4 * _nbytes((tm, tn), F32)
    out = pl.pallas_call(
        _seq_dft_kernel,
        grid=(batch, dft // tn, l // tm),
        in_specs=[
            pl.BlockSpec((tm, l), lambda b, jn, i: (i, 0)),
            pl.BlockSpec((tm, l), lambda b, jn, i: (i, 0)),
            pl.BlockSpec((1, l, tn), lambda b, jn, i: (b, 0, jn)),
            pl.BlockSpec((1, l, tn), lambda b, jn, i: (b, 0, jn)),
        ],
        out_specs=pl.BlockSpec((1, tm, tn), lambda b, jn, i: (b, i, jn)),
        out_shape=jax.ShapeDtypeStruct((batch, l, dft), BF16),
        compiler_params=_cparams(("parallel", "parallel", "parallel"), est),
        name="seq_dft",
    )(cl, sl, xc3, xs3)
    return out.reshape(m, dft)


def _prefix_count(mask_f32, tri):
    e, l = mask_f32.shape
    carry = jnp.zeros((e, 1), F32)
    outs = []
    for k in range(l // CUMSUM_BLOCK):
        blk = mask_f32[:, k * CUMSUM_BLOCK:(k + 1) * CUMSUM_BLOCK]
        outs.append(jnp.dot(blk.astype(BF16), tri, preferred_element_type=F32) + carry)
        carry = carry + jnp.sum(blk, axis=1, keepdims=True)
    return outs[0] if len(outs) == 1 else jnp.concatenate(outs, axis=1)


def _route_kernel(lg_ref, posm_ref, aff_ref, *, cap):
    lg = lg_ref[...]
    mx = jnp.max(lg, axis=0, keepdims=True)
    ex = jnp.exp(lg - mx)
    aff = ex / jnp.sum(ex, axis=0, keepdims=True)
    bits = pltpu.bitcast(aff, jnp.int32)
    e = lg.shape[0]
    v = jnp.zeros((e, 1), jnp.int32)
    for bit in range(30, -1, -1):
        cand = v | (1 << bit)
        cnt = jnp.sum(jnp.where(bits >= cand, 1.0, 0.0), axis=1, keepdims=True)
        v = jnp.where(cnt >= cap, cand, v)
    gt = bits > v
    eq = bits == v
    n_gt = jnp.sum(gt.astype(F32), axis=1, keepdims=True)
    r = lax.broadcasted_iota(jnp.int32, (CUMSUM_BLOCK, CUMSUM_BLOCK), 0)
    c = lax.broadcasted_iota(jnp.int32, (CUMSUM_BLOCK, CUMSUM_BLOCK), 1)
    tri = jnp.where(r < c, 1.0, 0.0).astype(BF16)
    eq_f = jnp.where(eq, 1.0, 0.0)
    tie_ok = _prefix_count(eq_f, tri) < (cap - n_gt)
    sel_f = jnp.where(gt, 1.0, jnp.where(tie_ok, eq_f, 0.0))
    pos = _prefix_count(sel_f, tri)
    posm_ref[0] = jnp.where(sel_f > 0.5, pos, -1.0).astype(jnp.int32)
    aff_ref[0] = aff


def _route(lg_t, batch, cap):
    e, m = lg_t.shape
    l = m // batch
    est = 16 * _nbytes((e, l), F32)
    return pl.pallas_call(
        functools.partial(_route_kernel, cap=cap),
        grid=(batch,),
        in_specs=[pl.BlockSpec((e, l), lambda b: (0, b))],
        out_specs=[pl.BlockSpec((1, e, l), lambda b: (b, 0, 0))] * 2,
        out_shape=[jax.ShapeDtypeStruct((batch, e, l), jnp.int32),
                   jax.ShapeDtypeStruct((batch, e, l), F32)],
        compiler_params=_cparams(("parallel",), est),
        name="route",
    )(lg_t)


def _slots_kernel(posm_ref, aff_ref, *rest, cap):
    idx_ref, gs_ref = rest[-2], rest[-1]
    pm = posm_ref[0, 0]
    l = pm.shape[1]
    slot = lax.broadcasted_iota(jnp.int32, (cap, l), 0)
    tok = lax.broadcasted_iota(jnp.int32, (1, l), 1).astype(F32)
    hit = pm == slot
    idx_ref[0, 0] = jnp.sum(jnp.where(hit, tok, 0.0), axis=1, keepdims=True).astype(jnp.int32)
    gs_ref[0] = jnp.sum(jnp.where(hit, aff_ref[0, 0], 0.0), axis=1, keepdims=True)


def _slots(posm, aff, cap, n_rows, row_off, gs_all=None):
    batch, e, l = posm.shape
    blk = row_off // cap
    in_specs = [pl.BlockSpec((1, 1, 1, l), lambda b, ee: (b, ee, 0, 0))] * 2
    args = [posm.reshape(batch, e, 1, l), aff.reshape(batch, e, 1, l)]
    aliases = {}
    if gs_all is not None:
        in_specs.append(pl.BlockSpec(memory_space=pl.ANY))
        args.append(gs_all)
        aliases = {2: 1}
    est = 6 * _nbytes((cap, l), F32) + 4 * _nbytes((cap, V7X_LANES), F32)
    return pl.pallas_call(
        functools.partial(_slots_kernel, cap=cap),
        grid=(batch, e),
        in_specs=in_specs,
        out_specs=[pl.BlockSpec((1, 1, cap, 1), lambda b, ee: (b, ee, 0, 0)),
                   pl.BlockSpec((1, cap, 1), lambda b, ee: (ee, blk + b, 0))],
        out_shape=[jax.ShapeDtypeStruct((batch, e, cap, 1), jnp.int32),
                   jax.ShapeDtypeStruct((e, n_rows, 1), F32)],
        input_output_aliases=aliases,
        compiler_params=_cparams(("parallel", "parallel"), est),
        name="moe_slots",
    )(*args)


GATHER_ROW_CHUNK = 128


def _gather_kernel(idx_ref, x_hbm, g_ref, sh_ref, sc_ref, *rest, cap, seq, n_exp, eps):
    o_ref, rows_scr, sem = rest[-3], rest[-2], rest[-1]
    b, ee = pl.program_id(0), pl.program_id(1)
    base = (b * n_exp + ee) * cap
    row0 = b * seq

    def row_copy(src_row, slot):
        return pltpu.make_async_copy(x_hbm.at[pl.ds(src_row, 1)], rows_scr.at[pl.ds(slot, 1)], sem)

    def issue(s, carry):
        row_copy(row0 + idx_ref[base + s], s).start()
        return carry

    lax.fori_loop(0, cap, issue, 0)
    pltpu.make_async_copy(x_hbm.at[pl.ds(0, cap)], rows_scr, sem).wait()
    step = min(GATHER_ROW_CHUNK, cap)
    for r in range(0, cap, step):
        y = _rms_modulate(rows_scr[r:r + step, :], g_ref[...], sh_ref[0], sc_ref[0], eps)
        o_ref[0, r:r + step, :] = y.astype(o_ref.dtype)


def _gather(idx, x2, g, shift, scale, cap, n_rows, row_off, xg_all=None):
    batch, e = idx.shape[0], idx.shape[1]
    m, d = x2.shape
    seq = m // batch
    blk = row_off // cap
    in_specs = [
        pl.BlockSpec(memory_space=pl.ANY),
        pl.BlockSpec((1, d), lambda b, ee, ix: (0, 0)),
        pl.BlockSpec((1, 1, d), lambda b, ee, ix: (b, 0, 0)),
        pl.BlockSpec((1, 1, d), lambda b, ee, ix: (b, 0, 0)),
    ]
    args = [idx.reshape(-1), x2, g.reshape(1, d), shift, scale]
    aliases = {}
    if xg_all is not None:
        in_specs.append(pl.BlockSpec(memory_space=pl.ANY))
        args.append(xg_all)
        aliases = {5: 0}
    step = min(GATHER_ROW_CHUNK, cap)
    est = _nbytes((cap, d), F32) + 2 * _nbytes((cap, d), BF16) + 6 * _nbytes((step, d), F32)
    return pl.pallas_call(
        functools.partial(_gather_kernel, cap=cap, seq=seq, n_exp=e, eps=OP["rms_eps"]),
        grid_spec=pltpu.PrefetchScalarGridSpec(
            num_scalar_prefetch=1,
            grid=(batch, e),
            in_specs=in_specs,
            out_specs=pl.BlockSpec((1, cap, d), lambda b, ee, ix: (ee, blk + b, 0)),
            scratch_shapes=[pltpu.VMEM((cap, d), F32), pltpu.SemaphoreType.DMA(())],
        ),
        out_shape=jax.ShapeDtypeStruct((e, n_rows, d), BF16),
        input_output_aliases=aliases,
        compiler_params=_cparams(("arbitrary", "arbitrary"), est),
        name="moe_gather",
    )(*args)


EXPERT_ROW_BLOCKS = 4


def _up_kernel(xg_ref, wg_ref, wu_ref, o_ref, wg_bf, wu_bf):
    @pl.when(pl.program_id(2) == 0)
    def _():
        wg_bf[...] = wg_ref[0, 0].astype(BF16)
        wu_bf[...] = wu_ref[0, 0].astype(BF16)

    x = xg_ref[0]
    g = jnp.dot(x, wg_bf[...], preferred_element_type=F32)
    u = jnp.dot(x, wu_bf[...], preferred_element_type=F32)
    o_ref[0] = (g * jax.nn.sigmoid(g) * u).astype(o_ref.dtype)


def _expert_up(xg, w_gate, w_up, layer, tf=256):
    e, r, d = xg.shape
    f = w_gate.shape[3]
    tf = min(tf, f)
    tr = r // EXPERT_ROW_BLOCKS
    est = (4 * _nbytes((d, tf), F32) + 3 * _nbytes((d, tf), BF16) + 2 * _nbytes((tr, d), BF16)
           + 6 * _nbytes((tr, tf), F32))
    return pl.pallas_call(
        _up_kernel,
        grid=(e, f // tf, EXPERT_ROW_BLOCKS),
        in_specs=[
            pl.BlockSpec((1, tr, d), lambda ee, fj, rr: (ee, rr, 0)),
            pl.BlockSpec((1, 1, d, tf), lambda ee, fj, rr: (layer, ee, 0, fj)),
            pl.BlockSpec((1, 1, d, tf), lambda ee, fj, rr: (layer, ee, 0, fj)),
        ],
        out_specs=pl.BlockSpec((1, tr, tf), lambda ee, fj, rr: (ee, rr, fj)),
        out_shape=jax.ShapeDtypeStruct((e, r, f), BF16),
        scratch_shapes=[pltpu.VMEM((d, tf), BF16), pltpu.VMEM((d, tf), BF16)],
        compiler_params=_cparams(("parallel", "parallel", "arbitrary"), est),
        name="moe_up",
    )(xg, w_gate, w_up)


def _down_kernel(h_ref, wd_ref, gs_ref, o_ref, wd_bf):
    @pl.when(pl.program_id(2) == 0)
    def _():
        wd_bf[...] = wd_ref[0, 0].astype(BF16)

    y = jnp.dot(h_ref[0], wd_bf[...], preferred_element_type=F32)
    o_ref[0] = (y * gs_ref[0]).astype(o_ref.dtype)


def _expert_down(hid, w_down, gslot, layer, td=1024):
    e, r, f = hid.shape
    d = w_down.shape[3]
    td = min(td, d)
    tr = r // EXPERT_ROW_BLOCKS
    est = (2 * _nbytes((f, td), F32) + 2 * _nbytes((f, td), BF16) + 2 * _nbytes((tr, f), BF16)
           + 4 * _nbytes((tr, td), F32) + 2 * _nbytes((tr, V7X_LANES), F32))
    return pl.pallas_call(
        _down_kernel,
        grid=(e, d // td, EXPERT_ROW_BLOCKS),
        in_specs=[
            pl.BlockSpec((1, tr, f), lambda ee, dj, rr: (ee, rr, 0)),
            pl.BlockSpec((1, 1, f, td), lambda ee, dj, rr: (layer, ee, 0, dj)),
            pl.BlockSpec((1, tr, 1), lambda ee, dj, rr: (ee, rr, 0)),
        ],
        out_specs=pl.BlockSpec((1, tr, td), lambda ee, dj, rr: (ee, rr, dj)),
        out_shape=jax.ShapeDtypeStruct((e, r, d), BF16),
        scratch_shapes=[pltpu.VMEM((f, td), BF16)],
        compiler_params=_cparams(("parallel", "parallel", "arbitrary"), est),
        name="moe_down",
    )(hid, w_down, gslot)


def _combine_kernel(pt_ref, y_ref, x_ref, g_ref, o_ref, p_scr, *, n_exp, cap):
    @pl.when(pl.program_id(2) == 0)
    def _():
        pt = pt_ref[...]
        slot = lax.broadcasted_iota(jnp.int32, (pt.shape[0], cap), 1)
        for ee in range(n_exp):
            p_scr[:, ee * cap:(ee + 1) * cap] = jnp.where(
                pt[:, ee:ee + 1] == slot, 1.0, 0.0).astype(BF16)

    yy = y_ref[...]
    acc = jnp.dot(p_scr[...], yy.reshape(n_exp * cap, yy.shape[2]), preferred_element_type=F32)
    o_ref[...] = x_ref[...] + g_ref[0] * acc


def _combine(posm_t, y, x2, gate, batch, cap, row_off, tm=512, tn=512):
    m, d = x2.shape
    l = m // batch
    e = posm_t.shape[1]
    tm = min(tm, l)
    tn = min(tn, d)
    tpb = l // tm
    blk = row_off // cap
    est = (2 * _nbytes((e * cap, tn), BF16) + _nbytes((tm, e * cap), BF16) + 6 * _nbytes((tm, tn), F32)
           + 2 * _nbytes((tm, V7X_LANES), F32))
    return pl.pallas_call(
        functools.partial(_combine_kernel, n_exp=e, cap=cap),
        grid=(batch, tpb, d // tn),
        in_specs=[
            pl.BlockSpec((tm, e), lambda b, i, j: (b * tpb + i, 0)),
            pl.BlockSpec((e, cap, tn), lambda b, i, j: (0, blk + b, j)),
            pl.BlockSpec((tm, tn), lambda b, i, j: (b * tpb + i, j)),
            pl.BlockSpec((1, 1, tn), lambda b, i, j: (b, 0, j)),
        ],
        out_specs=pl.BlockSpec((tm, tn), lambda b, i, j: (b * tpb + i, j)),
        out_shape=jax.ShapeDtypeStruct((m, d), F32),
        scratch_shapes=[pltpu.VMEM((tm, e * cap), BF16)],
        input_output_aliases={2: 0},
        compiler_params=_cparams(("parallel", "parallel", "arbitrary"), est),
        name="moe_combine",
    )(posm_t, y, x2, gate)


def _sigmoid(z):
    return jax.nn.sigmoid(z)


def _in_proj(h2, w_in_bf, d_s5, d_ft, d):
    tn_s = min(512, d_s5)
    first = lambda accs, ex: accs[0]
    u = _mm([(h2, w_in_bf, 0)], [], first, d_s5, F32, 1024, tn_s, name="in_s5")
    if d_ft is None:
        return u, None, None
    v = _mm([(h2, w_in_bf, d_s5 // tn_s)], [], first, d_ft, BF16, 1024, tn_s, name="in_ft")
    tn_g = math.gcd(512, d_s5 + d_ft, d)
    pg = _mm([(h2, w_in_bf, (d_s5 + d_ft) // tn_g)], [], first, 2 * d, F32, 1024, tn_g, name="in_gates")
    return u, v, pg


def _mixer_out(x2, a2, v2, pg, g_mix, wts, dfts, batch, rpb):
    w_glu, w_s5o, w_fto, w_out = wts
    wc, cl, sl = dfts
    d = x2.shape[1]
    d_s5 = a2.shape[1]
    xc, xs = _chan_dft(v2, wc)
    yf = _seq_dft(xc, xs, cl, sl, batch)
    glu = _mm([(a2, w_glu, 0)], [("tile", a2, 0)],
              lambda accs, ex: ex[0].astype(F32) * _sigmoid(accs[0]),
              d_s5, BF16, 1024, 512, name="glu")
    tn = min(512, d)
    merged = _mm([(glu, w_s5o, 0), (yf, w_fto, 0)],
                 [("tile", pg, 0), ("tile", pg, d // tn)],
                 lambda accs, ex: _sigmoid(ex[0]) * accs[0] + _sigmoid(ex[1]) * accs[1],
                 d, BF16, 1024, tn, name="merge")
    return _mm([(merged, w_out, 0)], [("tile", x2, 0), ("row", g_mix, 0)],
               lambda accs, ex: ex[0] + ex[1] * accs[0],
               d, F32, 1024, tn, rows_per_batch=rpb, alias_extra=0, name="out_proj")


def _moe(streams, w_gate, w_up, w_down, layer):
    e = streams[0][6].shape[0]
    caps = [OP["capacity_factor"] * (s[0].shape[0] // s[1]) // e for s in streams]
    offs, n_rows = [], 0
    for s, cap in zip(streams, caps):
        assert n_rows % cap == 0
        offs.append(n_rows)
        n_rows += s[1] * cap
    xg = gslot = None
    if len(streams) > 1:
        xg = jnp.zeros((e, n_rows, streams[0][0].shape[1]), BF16)
        gslot = jnp.zeros((e, n_rows, 1), F32)
    posms = []
    for (x2, batch, g, shift, scale, _, lg_t), cap, off in zip(streams, caps, offs):
        posm, aff = _route(lg_t, batch, cap)
        idx, gslot = _slots(posm, aff, cap, n_rows, off, gslot)
        xg = _gather(idx, x2, g, shift, scale, cap, n_rows, off, xg)
        posms.append(posm)
    y = _expert_down(_expert_up(xg, w_gate, w_up, layer), w_down, gslot, layer)
    outs = []
    for (x2, batch, _, _, _, gate, _), cap, off, posm in zip(streams, caps, offs, posms):
        posm_t = posm.transpose(0, 2, 1).reshape(x2.shape[0], e)
        outs.append(_combine(posm_t, y, x2, gate, batch, cap, off))
    return outs


def _grid_posembed(n_tok, d):
    gw = OP["grid_w"]
    rows = n_tok // gw
    quarter = d // 4
    inv_freq = OP["pos_base"] ** (-jnp.arange(quarter, dtype=F32) / quarter)
    ang_r = jnp.arange(rows, dtype=F32)[:, None] * inv_freq
    ang_c = jnp.arange(gw, dtype=F32)[:, None] * inv_freq
    shape = (rows, gw, quarter)
    parts = [jnp.broadcast_to(jnp.sin(ang_r)[:, None, :], shape),
             jnp.broadcast_to(jnp.cos(ang_r)[:, None, :], shape),
             jnp.broadcast_to(jnp.sin(ang_c)[None, :, :], shape),
             jnp.broadcast_to(jnp.cos(ang_c)[None, :, :], shape)]
    return jnp.concatenate(parts, axis=-1).reshape(n_tok, d)


def kernel(x, c, ctx, c_ctx, ada_w, ada_b, norm_mix_g, norm_ffn_g, w_in, s5_lam_re, s5_lam_im,
           s5_log_dt, s5_b_re, s5_b_im, s5_c_re, s5_c_im, s5_d, w_glu, w_s5_out, w_ft_out, w_out,
           w_router, w_gate, w_up, w_down, norm_final_g):
    batch, seq, d = x.shape
    ctx_len = ctx.shape[1]
    depth = ada_w.shape[0]
    d_s5 = w_glu.shape[1]
    d_ft = w_ft_out.shape[1]
    n_mod = OP["n_mod"]
    m_x, m_c = batch * seq, batch * ctx_len

    x2 = x.reshape(m_x, d)
    pos = _grid_posembed(seq, d).astype(x.dtype)
    c2 = ctx.reshape(m_c, d)

    rows = -(-(batch + 1) // 8) * 8
    c8 = jnp.zeros((rows, d), F32).at[:batch].set(c).at[batch].set(c_ctx)
    mod = _adaln(c8, ada_w, ada_b)

    gd = d_ft // OP["ft_groups"]
    wc_c, wc_s = _dft_mats(gd)
    wc = jnp.concatenate([wc_c, wc_s], axis=1)
    dft_x = (wc,) + _dft_mats(seq)
    dft_c = (wc,) + _dft_mats(ctx_len)

    sw = 4 * S5_TILE_GROUPS * OP["s5_state"]
    h_zero = jnp.zeros((d_s5 // V7X_LANES, batch, 1, sw), F32)

    for i in range(depth):
        last = i == depth - 1
        mx = [mod[i, :batch, k * d:(k + 1) * d].reshape(batch, 1, d) for k in range(n_mod)]
        mc = [jnp.broadcast_to(mod[i, batch:batch + 1, k * d:(k + 1) * d].reshape(1, 1, d),
                               (batch, 1, d)) for k in range(n_mod)]
        w_in_bf = w_in[i].astype(BF16)
        mix_w = (w_glu[i].astype(BF16), w_s5_out[i].astype(BF16), w_ft_out[i].astype(BF16),
                 w_out[i].astype(BF16))
        wr_t = w_router[i].T.astype(BF16)
        ops = _s5_operators(s5_lam_re[i], s5_lam_im[i], s5_log_dt[i], s5_b_re[i], s5_b_im[i],
                            s5_c_re[i], s5_c_im[i], s5_d[i], S5_CHUNK)

        hc = _norm(c2, norm_mix_g[i], mc[0], mc[1], rows_per_batch=ctx_len)
        uc, vc, pgc = _in_proj(hc, w_in_bf, d_s5, None if last else d_ft, d)
        ac, h_ctx = _s5(uc, batch, ops, h_zero)

        if i == 0:
            hx, x2 = _norm(x2, norm_mix_g[i], mx[0], mx[1], rows_per_batch=seq, pos=pos)
        else:
            hx = _norm(x2, norm_mix_g[i], mx[0], mx[1], rows_per_batch=seq)
        ux, vx, pgx = _in_proj(hx, w_in_bf, d_s5, d_ft, d)
        ax, _ = _s5(ux, batch, ops, h_ctx)
        x2 = _mixer_out(x2, ax, vx, pgx, mx[2], mix_w, dft_x, batch, seq)
        lgx = _norm(x2, norm_ffn_g[i], mx[3], mx[4], rows_per_batch=seq, wr_t=wr_t, emit_h=False)
        streams = [(x2, batch, norm_ffn_g[i], mx[3], mx[4], mx[5], lgx)]
        if not last:
            c2 = _mixer_out(c2, ac, vc, pgc, mc[2], mix_w, dft_c, batch, ctx_len)
            lgc = _norm(c2, norm_ffn_g[i], mc[3], mc[4], rows_per_batch=ctx_len, wr_t=wr_t,
                        emit_h=False)
            streams.append((c2, batch, norm_ffn_g[i], mc[3], mc[4], mc[5], lgc))
        outs = _moe(streams, w_gate, w_up, w_down, i)
        x2 = outs[0]
        if not last:
            c2 = outs[1]

    out = _norm(x2, norm_final_g, out_dtype=x.dtype)
    return out.reshape(batch, seq, d)
```

```python
import functools
import math

import jax
import jax.numpy as jnp
from jax import lax
from jax.experimental import pallas as pl
from jax.experimental.pallas import tpu as pltpu

BF16 = jnp.bfloat16
F32 = jnp.float32
HI = lax.Precision.HIGHEST

OP = dict(
    s5_group_dim=16,
    s5_state=64,
    ft_groups=4,
    capacity_factor=2,
    n_mod=6,
    rms_eps=1e-6,
    pos_base=10000.0,
    grid_w=64,
    lambda_re_max=-1e-4,
)

V7X_VMEM_BYTES = 64 * 1024 * 1024
V7X_LANES = 128
S5_TILE_GROUPS = V7X_LANES // 16
S5_CHUNK = 16
CUMSUM_BLOCK = 256


def _cparams(sem, vmem_est):
    limit = int(min(max(vmem_est * 5 // 4 + (4 << 20), 32 << 20), V7X_VMEM_BYTES - (6 << 20)))
    return pltpu.CompilerParams(dimension_semantics=sem, vmem_limit_bytes=limit)


def _nbytes(shape, dtype):
    return math.prod(shape) * jnp.dtype(dtype).itemsize


def _adaln_kernel(c_ref, w_ref, b_ref, o_ref):
    cv = c_ref[...]
    a = (cv * jax.nn.sigmoid(cv)).astype(BF16)
    o_ref[0] = jnp.dot(a, w_ref[0].astype(BF16), preferred_element_type=F32) + b_ref[0]


def _adaln(c8, ada_w, ada_b, tn=512):
    depth, d, n6 = ada_w.shape
    rows = c8.shape[0]
    tn = min(tn, n6)
    est = 2 * _nbytes((d, tn), F32) + 4 * _nbytes((rows, tn), F32) + _nbytes((rows, d), F32) * 2
    return pl.pallas_call(
        _adaln_kernel,
        grid=(depth, n6 // tn),
        in_specs=[
            pl.BlockSpec((rows, d), lambda l, j: (0, 0)),
            pl.BlockSpec((1, d, tn), lambda l, j: (l, 0, j)),
            pl.BlockSpec((1, 1, tn), lambda l, j: (l, 0, j)),
        ],
        out_specs=pl.BlockSpec((1, rows, tn), lambda l, j: (l, 0, j)),
        out_shape=jax.ShapeDtypeStruct((depth, rows, n6), F32),
        compiler_params=_cparams(("parallel", "parallel"), est),
        name="adaln",
    )(c8, ada_w, ada_b.reshape(depth, 1, n6))


def _rms_modulate(xf, g, shift, scale, eps):
    ms = jnp.mean(xf * xf, axis=-1, keepdims=True)
    y = xf * lax.rsqrt(ms + eps) * g
    if shift is not None:
        y = y * (1.0 + scale) + shift
    return y


def _norm_kernel(*refs, eps, modulate, router, add_pos, emit_h):
    x_ref, g_ref = refs[0], refs[1]
    k = 2
    xf = x_ref[...]
    if add_pos:
        xf = xf + refs[k][...]
        k += 1
        refs[-1][...] = xf
    shift = scale = None
    if modulate:
        shift, scale = refs[k][0], refs[k + 1][0]
        k += 2
    y = _rms_modulate(xf, g_ref[...], shift, scale, eps)
    if router:
        wr_ref = refs[k]
        k += 1
    if emit_h:
        refs[k][...] = y.astype(refs[k].dtype)
        k += 1
    if router:
        refs[k][...] = lax.dot_general(
            wr_ref[...], y.astype(BF16), (((1,), (1,)), ((), ())), preferred_element_type=F32)


def _norm(x2, g, shift=None, scale=None, rows_per_batch=None, wr_t=None, out_dtype=BF16, tm=256,
          pos=None, emit_h=True):
    m, d = x2.shape
    modulate = shift is not None
    router = wr_t is not None
    add_pos = pos is not None
    rpb = rows_per_batch if rows_per_batch is not None else m
    tm = min(tm, rpb)
    tpb = rpb // tm
    in_specs = [pl.BlockSpec((tm, d), lambda i: (i, 0)), pl.BlockSpec((1, d), lambda i: (0, 0))]
    args = [x2, g.reshape(1, d)]
    if add_pos:
        in_specs.append(pl.BlockSpec((tm, d), lambda i: (i % tpb, 0)))
        args.append(pos)
    if modulate:
        in_specs += [pl.BlockSpec((1, 1, d), lambda i: (i // tpb, 0, 0))] * 2
        args += [shift, scale]
    out_specs, out_shape = [], []
    if emit_h:
        out_specs.append(pl.BlockSpec((tm, d), lambda i: (i, 0)))
        out_shape.append(jax.ShapeDtypeStruct((m, d), out_dtype))
    if router:
        e = wr_t.shape[0]
        in_specs.append(pl.BlockSpec((e, d), lambda i: (0, 0)))
        args.append(wr_t)
        out_specs.append(pl.BlockSpec((e, tm), lambda i: (0, i)))
        out_shape.append(jax.ShapeDtypeStruct((e, m), F32))
    est = 2 * _nbytes((tm, d), F32) + 2 * _nbytes((tm, d), out_dtype) + 3 * _nbytes((tm, d), F32)
    if add_pos:
        out_specs.append(pl.BlockSpec((tm, d), lambda i: (i, 0)))
        out_shape.append(jax.ShapeDtypeStruct((m, d), F32))
        est += 4 * _nbytes((tm, d), F32)
    res = pl.pallas_call(
        functools.partial(_norm_kernel, eps=OP["rms_eps"], modulate=modulate, router=router,
                          add_pos=add_pos, emit_h=emit_h),
        grid=(m // tm,),
        in_specs=in_specs,
        out_specs=out_specs,
        out_shape=out_shape,
        compiler_params=_cparams(("parallel",), est),
        name="rmsnorm",
    )(*args)
    return res if len(res) > 1 else res[0]


def _mm_kernel(*refs, n_pairs, kinds, epilogue):
    accs = []
    for k in range(n_pairs):
        a = refs[2 * k][...].astype(BF16)
        w = refs[2 * k + 1][...].astype(BF16)
        accs.append(jnp.dot(a, w, preferred_element_type=F32))
    ex = []
    for k, kind in enumerate(kinds):
        r = refs[2 * n_pairs + k]
        ex.append(r[0] if kind == "row" else r[...])
    o_ref = refs[-1]
    o_ref[...] = epilogue(accs, ex).astype(o_ref.dtype)


def _mm(pairs, extras, epilogue, n_out, out_dtype, tm, tn, rows_per_batch=None, alias_extra=None,
        name="mm"):
    m = pairs[0][0].shape[0]
    rpb = rows_per_batch if rows_per_batch is not None else m
    tm = min(tm, rpb)
    tn = min(tn, n_out)
    tpb = rpb // tm
    in_specs, args = [], []
    est = 2 * _nbytes((tm, tn), out_dtype) + 2 * _nbytes((tm, tn), F32) * max(1, len(pairs))
    for a, w, off in pairs:
        kdim = a.shape[1]
        in_specs.append(pl.BlockSpec((tm, kdim), lambda i, j: (i, 0)))
        in_specs.append(pl.BlockSpec((kdim, tn), lambda i, j, off=off: (0, j + off)))
        args += [a, w]
        est += 2 * _nbytes((tm, kdim), a.dtype) + 2 * _nbytes((kdim, tn), w.dtype)
        if w.dtype != BF16:
            est += _nbytes((kdim, tn), BF16)
    kinds = []
    for kind, arr, off in extras:
        kinds.append(kind)
        if kind == "tile":
            in_specs.append(pl.BlockSpec((tm, tn), lambda i, j, off=off: (i, j + off)))
            est += 2 * _nbytes((tm, tn), arr.dtype)
        else:
            in_specs.append(pl.BlockSpec((1, 1, tn), lambda i, j, off=off: (i // tpb, 0, j + off)))
        args.append(arr)
    aliases = {}
    if alias_extra is not None:
        aliases = {2 * len(pairs) + alias_extra: 0}
    return pl.pallas_call(
        functools.partial(_mm_kernel, n_pairs=len(pairs), kinds=tuple(kinds), epilogue=epilogue),
        grid=(m // tm, n_out // tn),
        in_specs=in_specs,
        out_specs=pl.BlockSpec((tm, tn), lambda i, j: (i, j)),
        out_shape=jax.ShapeDtypeStruct((m, n_out), out_dtype),
        input_output_aliases=aliases,
        compiler_params=_cparams(("parallel", "parallel"), est),
        name=name,
    )(*args)


def _s5_operators(lam_re, lam_im, log_dt, b_re, b_im, c_re, c_im, d_skip, t):
    n = OP["s5_group_dim"]
    lr = jnp.minimum(lam_re.astype(F32), OP["lambda_re_max"])
    li = lam_im.astype(F32)
    dt = jnp.exp(log_dt.astype(F32))[..., None]
    g, p = lr.shape[1], lr.shape[2]
    gt = S5_TILE_GROUPS
    j = g // gt
    mag = jnp.exp(lr * dt)
    a_re = mag * jnp.cos(li * dt)
    a_im = mag * jnp.sin(li * dt)
    num_re, num_im = a_re - 1.0, a_im
    den = lr * lr + li * li
    f_re = (num_re * lr + num_im * li) / den
    f_im = (num_im * lr - num_re * li) / den
    br, bi = b_re.astype(F32), b_im.astype(F32)
    bb_re = f_re[..., None] * br - f_im[..., None] * bi
    bb_im = f_re[..., None] * bi + f_im[..., None] * br
    cr, ci = c_re.astype(F32), c_im.astype(F32)

    k = jnp.arange(t + 1, dtype=F32)[:, None, None, None]
    pw_mag = jnp.exp(lr[None] * dt[None] * k)
    pw_re = pw_mag * jnp.cos(li[None] * dt[None] * k)
    pw_im = pw_mag * jnp.sin(li[None] * dt[None] * k)

    ca_re = cr[None] * pw_re[:t, :, :, None, :] - ci[None] * pw_im[:t, :, :, None, :]
    ca_im = cr[None] * pw_im[:t, :, :, None, :] + ci[None] * pw_re[:t, :, :, None, :]
    bt_re, bt_im = bb_re.transpose(0, 1, 3, 2), bb_im.transpose(0, 1, 3, 2)
    kk = jnp.sum(ca_re[:, :, :, :, None, :] * bt_re[None, :, :, None, :, :]
                 - ca_im[:, :, :, :, None, :] * bt_im[None, :, :, None, :, :], axis=-1)
    kf, kb = kk[:, 0], kk[:, 1]
    k0 = kf[0] + kb[0] + d_skip.astype(F32).reshape(g, n)[:, :, None] * jnp.eye(n, dtype=F32)
    kall = jnp.concatenate([kb[1:][::-1], k0[None], kf[1:]], axis=0)
    kc = kall.reshape(2 * t - 1, j, gt, n, n).transpose(1, 0, 4, 2, 3)
    kc = kc.reshape(j, 2 * t - 1, n, gt * n).astype(BF16)
    lane = jnp.arange(gt * n)
    same_group = (lane[:, None] // n) == (lane[None, :] // n)
    bd = jnp.where(same_group, jnp.tile(kc, (1, 1, gt, 1)), 0)
    mt = jnp.stack([bd[:, t - 1 - s:2 * t - 1 - s] for s in range(t)], axis=1)
    mt = mt.transpose(0, 1, 3, 2, 4).reshape(j, t * gt * n, t * gt * n)

    ps_re = jnp.stack([pw_re[:t][::-1, 0], pw_re[:t, 1]], axis=0)
    ps_im = jnp.stack([pw_im[:t][::-1, 0], pw_im[:t, 1]], axis=0)
    w_re = ps_re[..., None] * bb_re[:, None] - ps_im[..., None] * bb_im[:, None]
    w_im = ps_re[..., None] * bb_im[:, None] + ps_im[..., None] * bb_re[:, None]
    w6 = jnp.stack([w_re, w_im], axis=1).reshape(2, 2, t, j, gt, p, n)
    wc = w6.transpose(3, 2, 6, 0, 1, 4, 5).reshape(j, t, n, 4 * gt * p).astype(BF16)
    col_group = (jnp.arange(4 * gt * p) // p) % gt
    pick = jnp.arange(gt)[:, None, None] == col_group[None, None, :]
    wst = jnp.where(pick[None, None], wc[:, :, None, :, :], 0).reshape(j, t * gt * n, 4 * gt * p)

    po_re = jnp.stack([pw_re[1:, 0], pw_re[1:][::-1, 1]], axis=0)
    po_im = jnp.stack([pw_im[1:, 0], pw_im[1:][::-1, 1]], axis=0)
    co_re = cr[:, None] * po_re[:, :, :, None, :] - ci[:, None] * po_im[:, :, :, None, :]
    co_im = cr[:, None] * po_im[:, :, :, None, :] + ci[:, None] * po_re[:, :, :, None, :]
    o6 = jnp.stack([co_re, -co_im], axis=1).reshape(2, 2, t, j, gt, n, p)
    oc = o6.transpose(3, 0, 1, 6, 2, 4, 5).reshape(j, 4, p, t * gt * n).astype(BF16)
    out_group = (jnp.arange(t * gt * n) // n) % gt
    pick_o = jnp.arange(gt)[:, None, None] == out_group[None, None, :]
    wout = jnp.where(pick_o[None, None], oc[:, :, None, :, :], 0).reshape(j, 4 * gt * p, t * gt * n)

    at = jnp.stack([pw_re[t, 0], pw_im[t, 0], pw_re[t, 1], pw_im[t, 1]], axis=0)
    at = at.reshape(4, j, gt * p).transpose(1, 0, 2).reshape(j, 1, 4 * gt * p)
    return mt, wst, wout, at


def _gelu_tanh(y):
    return 0.5 * y * (1.0 + jnp.tanh(0.7978845608028654 * (y + 0.044715 * y * y * y)))


def _s5_kernel(u_ref, mt_ref, wst_ref, wout_ref, at_ref, h0_ref, o_ref, hfin_ref, s_scr, hin_scr,
               *, n_chunks, half, t):
    lanes = u_ref.shape[1]
    u = jnp.concatenate([u_ref[pl.ds(s, n_chunks, stride=t), :] for s in range(t)],
                        axis=1).astype(BF16)
    s_scr[...] = jnp.dot(u, wst_ref[0], preferred_element_type=F32)
    at = at_ref[0]
    afr, afi = at[:, 0:half], at[:, half:2 * half]
    abr, abi = at[:, 2 * half:3 * half], at[:, 3 * half:4 * half]
    h0 = h0_ref[0, 0]
    init = (h0[:, 0:half], h0[:, half:2 * half], h0[:, 2 * half:3 * half], h0[:, 3 * half:4 * half])

    def body(i, carry):
        hfr, hfi, hbr, hbi = carry
        cb = n_chunks - 1 - i
        hin_scr[pl.ds(i, 1), 0:half] = hfr
        hin_scr[pl.ds(i, 1), half:2 * half] = hfi
        hin_scr[pl.ds(cb, 1), 2 * half:3 * half] = hbr
        hin_scr[pl.ds(cb, 1), 3 * half:4 * half] = hbi
        sf = s_scr[pl.ds(i, 1), 0:2 * half]
        sb = s_scr[pl.ds(cb, 1), 2 * half:4 * half]
        nfr = afr * hfr - afi * hfi + sf[:, 0:half]
        nfi = afr * hfi + afi * hfr + sf[:, half:2 * half]
        nbr = abr * hbr - abi * hbi + sb[:, 0:half]
        nbi = abr * hbi + abi * hbr + sb[:, half:2 * half]
        return nfr, nfi, nbr, nbi

    hfr, hfi, hbr, hbi = lax.fori_loop(0, n_chunks, body, init)
    hfin_ref[0, 0] = jnp.concatenate([hfr, hfi, hbr, hbi], axis=1)
    y = (jnp.dot(u, mt_ref[0], preferred_element_type=F32)
         + jnp.dot(hin_scr[...].astype(BF16), wout_ref[0], preferred_element_type=F32))
    for k in range(t):
        o_ref[pl.ds(k, n_chunks, stride=t), :] = _gelu_tanh(
            y[:, k * lanes:(k + 1) * lanes]).astype(o_ref.dtype)


def _s5(u2, batch, ops, h0):
    mt, wst, wout, at = ops
    m, ds5 = u2.shape
    l = m // batch
    t = S5_CHUNK
    c = l // t
    j = ds5 // V7X_LANES
    w = t * V7X_LANES
    sw = wst.shape[2]
    est = (3 * _nbytes((w, sw), BF16) + 4 * _nbytes((l, V7X_LANES), F32) + 2 * _nbytes((c, sw), F32)
           + 3 * _nbytes((c, w), F32) + _nbytes((c, sw), BF16) + 2 * _nbytes((c, w), BF16))
    single = pl.Buffered(1)
    return pl.pallas_call(
        functools.partial(_s5_kernel, n_chunks=c, half=sw // 4, t=t),
        grid=(j, batch),
        in_specs=[
            pl.BlockSpec((l, V7X_LANES), lambda jj, b: (b, jj)),
            pl.BlockSpec((1, w, w), lambda jj, b: (jj, 0, 0), pipeline_mode=single),
            pl.BlockSpec((1, w, sw), lambda jj, b: (jj, 0, 0), pipeline_mode=single),
            pl.BlockSpec((1, sw, w), lambda jj, b: (jj, 0, 0), pipeline_mode=single),
            pl.BlockSpec((1, 1, sw), lambda jj, b: (jj, 0, 0)),
            pl.BlockSpec((1, 1, 1, sw), lambda jj, b: (jj, b, 0, 0)),
        ],
        out_specs=[
            pl.BlockSpec((l, V7X_LANES), lambda jj, b: (b, jj)),
            pl.BlockSpec((1, 1, 1, sw), lambda jj, b: (jj, b, 0, 0)),
        ],
        out_shape=[
            jax.ShapeDtypeStruct((m, ds5), F32),
            jax.ShapeDtypeStruct((j, batch, 1, sw), F32),
        ],
        scratch_shapes=[pltpu.VMEM((c, sw), F32), pltpu.VMEM((c, sw), F32)],
        compiler_params=_cparams(("arbitrary", "arbitrary"), est),
        name="s5",
    )(u2, mt, wst, wout, at, h0)


def _dft_mats(n):
    hi = n // V7X_LANES
    j = jnp.arange(n, dtype=jnp.int32)[:, None]
    a1 = ((j * jnp.arange(V7X_LANES, dtype=jnp.int32)[None, :]) % n).astype(F32) * (2.0 * math.pi / n)
    a2 = ((j * jnp.arange(hi, dtype=jnp.int32)[None, :]) % hi).astype(F32) * (2.0 * math.pi / hi)
    c1, s1 = jnp.cos(a1)[:, None, :], jnp.sin(a1)[:, None, :]
    c2, s2 = jnp.cos(a2)[:, :, None], jnp.sin(a2)[:, :, None]
    s = 1.0 / math.sqrt(n)
    cos = ((c2 * c1 - s2 * s1) * s).astype(BF16).reshape(n, n)
    sin = ((s2 * c1 + c2 * s1) * s).astype(BF16).reshape(n, n)
    return cos, sin


def _chan_dft_kernel(v_ref, w_ref, xc_ref, xs_ref, *, gd):
    r = jnp.dot(v_ref[...], w_ref[...], preferred_element_type=F32)
    xc_ref[...] = r[:, :gd].astype(xc_ref.dtype)
    xs_ref[...] = r[:, gd:].astype(xs_ref.dtype)


def _chan_dft(v2, wc, tm=1024):
    m, dft = v2.shape
    gd = wc.shape[0]
    tm = min(tm, m)
    spec = pl.BlockSpec((tm, gd), lambda i, g: (i, g))
    est = 6 * _nbytes((tm, gd), BF16) + 2 * _nbytes((gd, 2 * gd), BF16) + 2 * _nbytes((tm, 2 * gd), F32)
    return pl.pallas_call(
        functools.partial(_chan_dft_kernel, gd=gd),
        grid=(m // tm, dft // gd),
        in_specs=[spec, pl.BlockSpec((gd, 2 * gd), lambda i, g: (0, 0))],
        out_specs=[spec, spec],
        out_shape=[jax.ShapeDtypeStruct((m, dft), BF16)] * 2,
        compiler_params=_cparams(("parallel", "parallel"), est),
        name="chan_dft",
    )(v2, wc)


def _seq_dft_kernel(cl_ref, sl_ref, xc_ref, xs_ref, o_ref):
    y = (jnp.dot(cl_ref[...], xc_ref[0], preferred_element_type=F32)
         - jnp.dot(sl_ref[...], xs_ref[0], preferred_element_type=F32))
    o_ref[0] = y.astype(o_ref.dtype)


def _seq_dft(xc, xs, cl, sl, batch, tm=512, tn=512):
    m, dft = xc.shape
    l = m // batch
    tm = min(tm, l)
    tn = min(tn, dft)
    xc3, xs3 = xc.reshape(batch, l, dft), xs.reshape(batch, l, dft)
    est = 4 * _nbytes((tm, l), BF16) + 4 * _nbytes((l, tn), BF16) + 4 * _nbytes((tm, tn), F32)
    out = pl.pallas_call(
        _seq_dft_kernel,
        grid=(batch, dft // tn, l // tm),
        in_specs=[
            pl.BlockSpec((tm, l), lambda b, jn, i: (i, 0)),
            pl.BlockSpec((tm, l), lambda b, jn, i: (i, 0)),
            pl.BlockSpec((1, l, tn), lambda b, jn, i: (b, 0, jn)),
            pl.BlockSpec((1, l, tn), lambda b, jn, i: (b, 0, jn)),
        ],
        out_specs=pl.BlockSpec((1, tm, tn), lambda b, jn, i: (b, i, jn)),
        out_shape=jax.ShapeDtypeStruct((batch, l, dft), BF16),
        compiler_params=_cparams(("parallel", "parallel", "parallel"), est),
        name="seq_dft",
    )(cl, sl, xc3, xs3)
    return out.reshape(m, dft)


def _prefix_count(mask_f32, tri):
    e, l = mask_f32.shape
    carry = jnp.zeros((e, 1), F32)
    outs = []
    for k in range(l // CUMSUM_BLOCK):
        blk = mask_f32[:, k * CUMSUM_BLOCK:(k + 1) * CUMSUM_BLOCK]
        outs.append(jnp.dot(blk.astype(BF16), tri, preferred_element_type=F32) + carry)
        carry = carry + jnp.sum(blk, axis=1, keepdims=True)
    return outs[0] if len(outs) == 1 else jnp.concatenate(outs, axis=1)


def _route_kernel(lg_ref, posm_ref, aff_ref, bnd_ref, *, cap, tile):
    lg = lg_ref[...]
    mx = jnp.max(lg, axis=0, keepdims=True)
    ex = jnp.exp(lg - mx)
    aff = ex / jnp.sum(ex, axis=0, keepdims=True)
    bits = pltpu.bitcast(aff, jnp.int32)
    e = lg.shape[0]
    v = jnp.zeros((e, 1), jnp.int32)
    for bit in range(30, -1, -1):
        cand = v | (1 << bit)
        cnt = jnp.sum(jnp.where(bits >= cand, 1.0, 0.0), axis=1, keepdims=True)
        v = jnp.where(cnt >= cap, cand, v)
    gt = bits > v
    eq = bits == v
    n_gt = jnp.sum(gt.astype(F32), axis=1, keepdims=True)
    r = lax.broadcasted_iota(jnp.int32, (CUMSUM_BLOCK, CUMSUM_BLOCK), 0)
    c = lax.broadcasted_iota(jnp.int32, (CUMSUM_BLOCK, CUMSUM_BLOCK), 1)
    tri = jnp.where(r < c, 1.0, 0.0).astype(BF16)
    eq_f = jnp.where(eq, 1.0, 0.0)
    tie_ok = _prefix_count(eq_f, tri) < (cap - n_gt)
    sel_f = jnp.where(gt, 1.0, jnp.where(tie_ok, eq_f, 0.0))
    pos = _prefix_count(sel_f, tri)
    posm_ref[0] = jnp.where(sel_f > 0.5, pos, -1.0).astype(jnp.int32)
    aff_ref[0] = aff
    l = lg.shape[1]
    starts = [pos[:, k * tile:k * tile + 1] for k in range(l // tile)]
    starts.append(jnp.full((e, 1), float(cap), F32))
    bnd_ref[0] = jnp.concatenate(starts, axis=1).astype(jnp.int32)


def _route(lg_t, batch, cap, tile):
    e, m = lg_t.shape
    l = m // batch
    nt = l // tile
    est = 16 * _nbytes((e, l), F32)
    return pl.pallas_call(
        functools.partial(_route_kernel, cap=cap, tile=tile),
        grid=(batch,),
        in_specs=[pl.BlockSpec((e, l), lambda b: (0, b))],
        out_specs=[pl.BlockSpec((1, e, l), lambda b: (b, 0, 0))] * 2
        + [pl.BlockSpec((1, e, nt + 1), lambda b: (b, 0, 0))],
        out_shape=[jax.ShapeDtypeStruct((batch, e, l), jnp.int32),
                   jax.ShapeDtypeStruct((batch, e, l), F32),
                   jax.ShapeDtypeStruct((batch, e, nt + 1), jnp.int32)],
        compiler_params=_cparams(("parallel",), est),
        name="route",
    )(lg_t)


def _slots_kernel(posm_ref, aff_ref, *rest, cap):
    idx_ref, gs_ref = rest[-2], rest[-1]
    pm = posm_ref[0, 0]
    l = pm.shape[1]
    slot = lax.broadcasted_iota(jnp.int32, (cap, l), 0)
    tok = lax.broadcasted_iota(jnp.int32, (1, l), 1).astype(F32)
    hit = pm == slot
    idx_ref[0, 0] = jnp.sum(jnp.where(hit, tok, 0.0), axis=1, keepdims=True).astype(jnp.int32)
    gs_ref[0] = jnp.sum(jnp.where(hit, aff_ref[0, 0], 0.0), axis=1, keepdims=True)


def _slots(posm, aff, cap, n_rows, row_off, gs_all=None):
    batch, e, l = posm.shape
    blk = row_off // cap
    in_specs = [pl.BlockSpec((1, 1, 1, l), lambda b, ee: (b, ee, 0, 0))] * 2
    args = [posm.reshape(batch, e, 1, l), aff.reshape(batch, e, 1, l)]
    aliases = {}
    if gs_all is not None:
        in_specs.append(pl.BlockSpec(memory_space=pl.ANY))
        args.append(gs_all)
        aliases = {2: 1}
    est = 6 * _nbytes((cap, l), F32) + 4 * _nbytes((cap, V7X_LANES), F32)
    return pl.pallas_call(
        functools.partial(_slots_kernel, cap=cap),
        grid=(batch, e),
        in_specs=in_specs,
        out_specs=[pl.BlockSpec((1, 1, cap, 1), lambda b, ee: (b, ee, 0, 0)),
                   pl.BlockSpec((1, cap, 1), lambda b, ee: (ee, blk + b, 0))],
        out_shape=[jax.ShapeDtypeStruct((batch, e, cap, 1), jnp.int32),
                   jax.ShapeDtypeStruct((e, n_rows, 1), F32)],
        input_output_aliases=aliases,
        compiler_params=_cparams(("parallel", "parallel"), est),
        name="moe_slots",
    )(*args)


GATHER_ROW_CHUNK = 64


def _gather_kernel(idx_ref, x_hbm, g_ref, sh_ref, sc_ref, *rest, cap, seq, n_exp, n_steps, eps):
    o_ref, rows_scr, sems = rest[-3], rest[-2], rest[-1]
    n = pl.program_id(0) * n_exp + pl.program_id(1)
    cur = n % 2
    nxt_step = jnp.minimum(n + 1, n_steps - 1)
    ch = min(GATHER_ROW_CHUNK, cap)

    def issue_rows(step, buf, lo):
        base = step * cap
        row0 = (step // n_exp) * seq
        for k in range(ch):
            src = x_hbm.at[pl.ds(row0 + idx_ref[base + lo + k], 1)]
            pltpu.make_async_copy(src, rows_scr.at[buf, pl.ds(lo + k, 1)], sems.at[buf]).start()

    def wait_rows(buf):
        pltpu.make_async_copy(x_hbm.at[pl.ds(0, cap)], rows_scr.at[buf], sems.at[buf]).wait()

    @pl.when(n == 0)
    def _():
        def first(c, carry):
            issue_rows(0, 0, c * ch)
            return carry
        lax.fori_loop(0, cap // ch, first, 0)

    wait_rows(cur)
    g, shift, scale = g_ref[...], sh_ref[0], sc_ref[0]

    def body(c, carry):
        r = pl.multiple_of(c * ch, ch)
        issue_rows(nxt_step, 1 - cur, r)
        y = _rms_modulate(rows_scr[cur, pl.ds(r, ch), :], g, shift, scale, eps)
        o_ref[0, pl.ds(r, ch), :] = y.astype(o_ref.dtype)
        return carry

    lax.fori_loop(0, cap // ch, body, 0)

    @pl.when(n == n_steps - 1)
    def _():
        wait_rows(1 - cur)


def _gather(idx, x2, g, shift, scale, cap, n_rows, row_off, xg_all=None):
    batch, e = idx.shape[0], idx.shape[1]
    m, d = x2.shape
    seq = m // batch
    blk = row_off // cap
    in_specs = [
        pl.BlockSpec(memory_space=pl.ANY),
        pl.BlockSpec((1, d), lambda b, ee, ix: (0, 0)),
        pl.BlockSpec((1, 1, d), lambda b, ee, ix: (b, 0, 0)),
        pl.BlockSpec((1, 1, d), lambda b, ee, ix: (b, 0, 0)),
    ]
    args = [idx.reshape(-1), x2, g.reshape(1, d), shift, scale]
    aliases = {}
    if xg_all is not None:
        in_specs.append(pl.BlockSpec(memory_space=pl.ANY))
        args.append(xg_all)
        aliases = {5: 0}
    est = 2 * _nbytes((cap, d), F32) + 2 * _nbytes((cap, d), BF16) + 8 * _nbytes((GATHER_ROW_CHUNK, d), F32)
    return pl.pallas_call(
        functools.partial(_gather_kernel, cap=cap, seq=seq, n_exp=e, n_steps=batch * e,
                          eps=OP["rms_eps"]),
        grid_spec=pltpu.PrefetchScalarGridSpec(
            num_scalar_prefetch=1,
            grid=(batch, e),
            in_specs=in_specs,
            out_specs=pl.BlockSpec((1, cap, d), lambda b, ee, ix: (ee, blk + b, 0)),
            scratch_shapes=[pltpu.VMEM((2, cap, d), F32), pltpu.SemaphoreType.DMA((2,))],
        ),
        out_shape=jax.ShapeDtypeStruct((e, n_rows, d), BF16),
        input_output_aliases=aliases,
        compiler_params=_cparams(("arbitrary", "arbitrary"), est),
        name="moe_gather",
    )(*args)


EXPERT_ROW_BLOCKS = 4


def _up_kernel(xg_ref, wg_ref, wu_ref, o_ref, wg_bf, wu_bf):
    @pl.when(pl.program_id(2) == 0)
    def _():
        wg_bf[...] = wg_ref[0, 0].astype(BF16)
        wu_bf[...] = wu_ref[0, 0].astype(BF16)

    x = xg_ref[0]
    g = jnp.dot(x, wg_bf[...], preferred_element_type=F32)
    u = jnp.dot(x, wu_bf[...], preferred_element_type=F32)
    o_ref[0] = (g * jax.nn.sigmoid(g) * u).astype(o_ref.dtype)


def _expert_up(xg, w_gate, w_up, layer, tf=256):
    e, r, d = xg.shape
    f = w_gate.shape[3]
    tf = min(tf, f)
    tr = r // EXPERT_ROW_BLOCKS
    est = (4 * _nbytes((d, tf), F32) + 3 * _nbytes((d, tf), BF16) + 2 * _nbytes((tr, d), BF16)
           + 6 * _nbytes((tr, tf), F32))
    return pl.pallas_call(
        _up_kernel,
        grid=(e, f // tf, EXPERT_ROW_BLOCKS),
        in_specs=[
            pl.BlockSpec((1, tr, d), lambda ee, fj, rr: (ee, rr, 0)),
            pl.BlockSpec((1, 1, d, tf), lambda ee, fj, rr: (layer, ee, 0, fj)),
            pl.BlockSpec((1, 1, d, tf), lambda ee, fj, rr: (layer, ee, 0, fj)),
        ],
        out_specs=pl.BlockSpec((1, tr, tf), lambda ee, fj, rr: (ee, rr, fj)),
        out_shape=jax.ShapeDtypeStruct((e, r, f), BF16),
        scratch_shapes=[pltpu.VMEM((d, tf), BF16), pltpu.VMEM((d, tf), BF16)],
        compiler_params=_cparams(("parallel", "parallel", "arbitrary"), est),
        name="moe_up",
    )(xg, w_gate, w_up)


def _down_kernel(h_ref, wd_ref, gs_ref, o_ref, wd_bf):
    @pl.when(pl.program_id(2) == 0)
    def _():
        wd_bf[...] = wd_ref[0, 0].astype(BF16)

    y = jnp.dot(h_ref[0], wd_bf[...], preferred_element_type=F32)
    o_ref[0] = (y * gs_ref[0]).astype(o_ref.dtype)


def _expert_down(hid, w_down, gslot, layer, td=1024):
    e, r, f = hid.shape
    d = w_down.shape[3]
    td = min(td, d)
    tr = r // EXPERT_ROW_BLOCKS
    est = (2 * _nbytes((f, td), F32) + 2 * _nbytes((f, td), BF16) + 2 * _nbytes((tr, f), BF16)
           + 4 * _nbytes((tr, td), F32) + 2 * _nbytes((tr, V7X_LANES), F32))
    return pl.pallas_call(
        _down_kernel,
        grid=(e, d // td, EXPERT_ROW_BLOCKS),
        in_specs=[
            pl.BlockSpec((1, tr, f), lambda ee, dj, rr: (ee, rr, 0)),
            pl.BlockSpec((1, 1, f, td), lambda ee, dj, rr: (layer, ee, 0, dj)),
            pl.BlockSpec((1, tr, 1), lambda ee, dj, rr: (ee, rr, 0)),
        ],
        out_specs=pl.BlockSpec((1, tr, td), lambda ee, dj, rr: (ee, rr, dj)),
        out_shape=jax.ShapeDtypeStruct((e, r, d), BF16),
        scratch_shapes=[pltpu.VMEM((f, td), BF16)],
        compiler_params=_cparams(("parallel", "parallel", "arbitrary"), est),
        name="moe_down",
    )(hid, w_down, gslot)


COMBINE_TILE = 256
DMA_ROW_ALIGN = 16


def combine_window(cap, seq, tile):
    return min(cap, max(DMA_ROW_ALIGN, 2 * tile * cap // seq))


def _combine_kernel(tbl_ref, pt_ref, y_hbm, x_ref, g_ref, o_ref, stage, extra, acc_scr, sems, xsem,
                    *, n_exp, cap, win, nt, row_off, n_steps):
    n = pl.program_id(0) * nt + pl.program_id(1)
    cur = n % 2
    nxt_step = jnp.minimum(n + 1, n_steps - 1)

    def window_starts(step):
        b, i = step // nt, step % nt
        t0 = (b * (nt + 1) + i) * n_exp
        out = []
        for ee in range(n_exp):
            s0a = jnp.minimum((tbl_ref[t0 + ee] // DMA_ROW_ALIGN) * DMA_ROW_ALIGN, cap - win)
            out.append(pl.multiple_of(s0a, DMA_ROW_ALIGN))
        return row_off + b * cap, out, t0

    def issue_windows(step, buf):
        row_base, starts, _ = window_starts(step)
        for ee in range(n_exp):
            pltpu.make_async_copy(y_hbm.at[ee, pl.ds(row_base + starts[ee], win), :],
                                  stage.at[buf, pl.ds(ee * win, win), :], sems.at[buf]).start()

    def wait_windows(buf):
        pltpu.make_async_copy(y_hbm.at[0, pl.ds(0, n_exp * win), :], stage.at[buf],
                              sems.at[buf]).wait()

    @pl.when(n == 0)
    def _():
        issue_windows(0, 0)

    issue_windows(nxt_step, 1 - cur)
    row_base, starts, t0 = window_starts(n)
    pt = pt_ref[...]
    lane = lax.broadcasted_iota(jnp.int32, (pt.shape[0], win), 1)
    p = jnp.concatenate(
        [jnp.where(pt[:, ee:ee + 1] - starts[ee] == lane, 1.0, 0.0).astype(BF16)
         for ee in range(n_exp)], axis=1)
    wait_windows(cur)
    acc_scr[...] = jnp.dot(p, stage[cur], preferred_element_type=F32)

    for ee in range(n_exp):
        first = starts[ee] + win
        n_extra = jnp.maximum(0, (tbl_ref[t0 + n_exp + ee] - first + win - 1) // win)

        def extra_window(w, carry, ee=ee, first=first):
            lo = first + w * win
            src0 = pl.multiple_of(jnp.minimum(lo, cap - win), DMA_ROW_ALIGN)
            cp = pltpu.make_async_copy(y_hbm.at[ee, pl.ds(row_base + src0, win), :], extra, xsem)
            cp.start()
            cp.wait()
            col = pt[:, ee:ee + 1]
            hit = jnp.where(col >= lo, col - src0, -1) == lane
            acc_scr[...] += jnp.dot(jnp.where(hit, 1.0, 0.0).astype(BF16), extra[...],
                                    preferred_element_type=F32)
            return carry

        lax.fori_loop(0, n_extra, extra_window, 0)

    o_ref[...] = x_ref[...] + g_ref[0] * acc_scr[...]

    @pl.when(n == n_steps - 1)
    def _():
        wait_windows(1 - cur)


def _combine(posm_t, bounds, y, x2, gate, batch, cap, row_off):
    m, d = x2.shape
    l = m // batch
    e = posm_t.shape[1]
    tm = min(COMBINE_TILE, l)
    nt = l // tm
    win = combine_window(cap, l, tm)
    assert cap % DMA_ROW_ALIGN == 0 and win % DMA_ROW_ALIGN == 0 and bounds.shape == (batch, e, nt + 1)
    tbl = bounds.transpose(0, 2, 1).reshape(-1)
    est = (2 * _nbytes((e * win, d), BF16) + _nbytes((win, d), BF16) + 7 * _nbytes((tm, d), F32)
           + 2 * _nbytes((tm, e * win), BF16) + 2 * _nbytes((tm, V7X_LANES), F32))
    return pl.pallas_call(
        functools.partial(_combine_kernel, n_exp=e, cap=cap, win=win, nt=nt, row_off=row_off,
                          n_steps=batch * nt),
        grid_spec=pltpu.PrefetchScalarGridSpec(
            num_scalar_prefetch=1,
            grid=(batch, nt),
            in_specs=[
                pl.BlockSpec((tm, e), lambda b, i, t: (b * nt + i, 0)),
                pl.BlockSpec(memory_space=pl.ANY),
                pl.BlockSpec((tm, d), lambda b, i, t: (b * nt + i, 0)),
                pl.BlockSpec((1, 1, d), lambda b, i, t: (b, 0, 0)),
            ],
            out_specs=pl.BlockSpec((tm, d), lambda b, i, t: (b * nt + i, 0)),
            scratch_shapes=[pltpu.VMEM((2, e * win, d), BF16), pltpu.VMEM((win, d), BF16),
                            pltpu.VMEM((tm, d), F32), pltpu.SemaphoreType.DMA((2,)),
                            pltpu.SemaphoreType.DMA(())],
        ),
        out_shape=jax.ShapeDtypeStruct((m, d), F32),
        input_output_aliases={3: 0},
        compiler_params=_cparams(("arbitrary", "arbitrary"), est),
        name="moe_combine",
    )(tbl, posm_t, y, x2, gate)


def _sigmoid(z):
    return jax.nn.sigmoid(z)


def _in_proj(h2, w_in_bf, d_s5, d_ft, d):
    tn_s = min(512, d_s5)
    first = lambda accs, ex: accs[0]
    u = _mm([(h2, w_in_bf, 0)], [], first, d_s5, F32, 1024, tn_s, name="in_s5")
    if d_ft is None:
        return u, None, None
    v = _mm([(h2, w_in_bf, d_s5 // tn_s)], [], first, d_ft, BF16, 1024, tn_s, name="in_ft")
    tn_g = math.gcd(512, d_s5 + d_ft, d)
    pg = _mm([(h2, w_in_bf, (d_s5 + d_ft) // tn_g)], [], first, 2 * d, F32, 1024, tn_g, name="in_gates")
    return u, v, pg


def _mixer_out(x2, a2, v2, pg, g_mix, wts, dfts, batch, rpb):
    w_glu, w_s5o, w_fto, w_out = wts
    wc, cl, sl = dfts
    d = x2.shape[1]
    d_s5 = a2.shape[1]
    xc, xs = _chan_dft(v2, wc)
    yf = _seq_dft(xc, xs, cl, sl, batch)
    glu = _mm([(a2, w_glu, 0)], [("tile", a2, 0)],
              lambda accs, ex: ex[0].astype(F32) * _sigmoid(accs[0]),
              d_s5, BF16, 1024, 512, name="glu")
    tn = min(512, d)
    merged = _mm([(glu, w_s5o, 0), (yf, w_fto, 0)],
                 [("tile", pg, 0), ("tile", pg, d // tn)],
                 lambda accs, ex: _sigmoid(ex[0]) * accs[0] + _sigmoid(ex[1]) * accs[1],
                 d, BF16, 1024, tn, name="merge")
    return _mm([(merged, w_out, 0)], [("tile", x2, 0), ("row", g_mix, 0)],
               lambda accs, ex: ex[0] + ex[1] * accs[0],
               d, F32, 1024, tn, rows_per_batch=rpb, alias_extra=0, name="out_proj")


def _moe(streams, w_gate, w_up, w_down, layer):
    e = streams[0][6].shape[0]
    caps = [OP["capacity_factor"] * (s[0].shape[0] // s[1]) // e for s in streams]
    offs, n_rows = [], 0
    for s, cap in zip(streams, caps):
        assert n_rows % cap == 0
        offs.append(n_rows)
        n_rows += s[1] * cap
    xg = gslot = None
    if len(streams) > 1:
        xg = jnp.zeros((e, n_rows, streams[0][0].shape[1]), BF16)
        gslot = jnp.zeros((e, n_rows, 1), F32)
    routed = []
    for (x2, batch, g, shift, scale, _, lg_t), cap, off in zip(streams, caps, offs):
        tile = min(COMBINE_TILE, x2.shape[0] // batch)
        posm, aff, bounds = _route(lg_t, batch, cap, tile)
        idx, gslot = _slots(posm, aff, cap, n_rows, off, gslot)
        xg = _gather(idx, x2, g, shift, scale, cap, n_rows, off, xg)
        routed.append((posm, bounds))
    y = _expert_down(_expert_up(xg, w_gate, w_up, layer), w_down, gslot, layer)
    outs = []
    for (x2, batch, _, _, _, gate, _), cap, off, (posm, bounds) in zip(streams, caps, offs, routed):
        posm_t = posm.transpose(0, 2, 1).reshape(x2.shape[0], e)
        outs.append(_combine(posm_t, bounds, y, x2, gate, batch, cap, off))
    return outs


def _grid_posembed(n_tok, d):
    gw = OP["grid_w"]
    rows = n_tok // gw
    quarter = d // 4
    inv_freq = OP["pos_base"] ** (-jnp.arange(quarter, dtype=F32) / quarter)
    ang_r = jnp.arange(rows, dtype=F32)[:, None] * inv_freq
    ang_c = jnp.arange(gw, dtype=F32)[:, None] * inv_freq
    shape = (rows, gw, quarter)
    parts = [jnp.broadcast_to(jnp.sin(ang_r)[:, None, :], shape),
             jnp.broadcast_to(jnp.cos(ang_r)[:, None, :], shape),
             jnp.broadcast_to(jnp.sin(ang_c)[None, :, :], shape),
             jnp.broadcast_to(jnp.cos(ang_c)[None, :, :], shape)]
    return jnp.concatenate(parts, axis=-1).reshape(n_tok, d)


def kernel(x, c, ctx, c_ctx, ada_w, ada_b, norm_mix_g, norm_ffn_g, w_in, s5_lam_re, s5_lam_im,
           s5_log_dt, s5_b_re, s5_b_im, s5_c_re, s5_c_im, s5_d, w_glu, w_s5_out, w_ft_out, w_out,
           w_router, w_gate, w_up, w_down, norm_final_g):
    batch, seq, d = x.shape
    ctx_len = ctx.shape[1]
    depth = ada_w.shape[0]
    d_s5 = w_glu.shape[1]
    d_ft = w_ft_out.shape[1]
    n_mod = OP["n_mod"]
    m_x, m_c = batch * seq, batch * ctx_len

    x2 = x.reshape(m_x, d)
    pos = _grid_posembed(seq, d).astype(x.dtype)
    c2 = ctx.reshape(m_c, d)

    rows = -(-(batch + 1) // 8) * 8
    c8 = jnp.zeros((rows, d), F32).at[:batch].set(c).at[batch].set(c_ctx)
    mod = _adaln(c8, ada_w, ada_b)

    gd = d_ft // OP["ft_groups"]
    wc_c, wc_s = _dft_mats(gd)
    wc = jnp.concatenate([wc_c, wc_s], axis=1)
    dft_x = (wc,) + _dft_mats(seq)
    dft_c = (wc,) + _dft_mats(ctx_len)

    sw = 4 * S5_TILE_GROUPS * OP["s5_state"]
    h_zero = jnp.zeros((d_s5 // V7X_LANES, batch, 1, sw), F32)

    for i in range(depth):
        last = i == depth - 1
        mx = [mod[i, :batch, k * d:(k + 1) * d].reshape(batch, 1, d) for k in range(n_mod)]
        mc = [jnp.broadcast_to(mod[i, batch:batch + 1, k * d:(k + 1) * d].reshape(1, 1, d),
                               (batch, 1, d)) for k in range(n_mod)]
        w_in_bf = w_in[i].astype(BF16)
        mix_w = (w_glu[i].astype(BF16), w_s5_out[i].astype(BF16), w_ft_out[i].astype(BF16),
                 w_out[i].astype(BF16))
        wr_t = w_router[i].T.astype(BF16)
        ops = _s5_operators(s5_lam_re[i], s5_lam_im[i], s5_log_dt[i], s5_b_re[i], s5_b_im[i],
                            s5_c_re[i], s5_c_im[i], s5_d[i], S5_CHUNK)

        hc = _norm(c2, norm_mix_g[i], mc[0], mc[1], rows_per_batch=ctx_len)
        uc, vc, pgc = _in_proj(hc, w_in_bf, d_s5, None if last else d_ft, d)
        ac, h_ctx = _s5(uc, batch, ops, h_zero)

        if i == 0:
            hx, x2 = _norm(x2, norm_mix_g[i], mx[0], mx[1], rows_per_batch=seq, pos=pos)
        else:
            hx = _norm(x2, norm_mix_g[i], mx[0], mx[1], rows_per_batch=seq)
        ux, vx, pgx = _in_proj(hx, w_in_bf, d_s5, d_ft, d)
        ax, _ = _s5(ux, batch, ops, h_ctx)
        x2 = _mixer_out(x2, ax, vx, pgx, mx[2], mix_w, dft_x, batch, seq)
        lgx = _norm(x2, norm_ffn_g[i], mx[3], mx[4], rows_per_batch=seq, wr_t=wr_t, emit_h=False)
        streams = [(x2, batch, norm_ffn_g[i], mx[3], mx[4], mx[5], lgx)]
        if not last:
            c2 = _mixer_out(c2, ac, vc, pgc, mc[2], mix_w, dft_c, batch, ctx_len)
            lgc = _norm(c2, norm_ffn_g[i], mc[3], mc[4], rows_per_batch=ctx_len, wr_t=wr_t,
                        emit_h=False)
            streams.append((c2, batch, norm_ffn_g[i], mc[3], mc[4], mc[5], lgc))
        outs = _moe(streams, w_gate, w_up, w_down, i)
        x2 = outs[0]
        if not last:
            c2 = outs[1]

    out = _norm(x2, norm_final_g, out_dtype=x.dtype)
    return out.reshape(batch, seq, d)
```

```python
import functools
import math

import jax
import jax.numpy as jnp
from jax import lax
from jax.experimental import pallas as pl
from jax.experimental.pallas import tpu as pltpu

BF16 = jnp.bfloat16
F32 = jnp.float32
HI = lax.Precision.HIGHEST

OP = dict(
    s5_group_dim=16,
    s5_state=64,
    ft_groups=4,
    capacity_factor=2,
    n_mod=6,
    rms_eps=1e-6,
    pos_base=10000.0,
    grid_w=64,
    lambda_re_max=-1e-4,
)

V7X_VMEM_BYTES = 64 * 1024 * 1024
V7X_LANES = 128
S5_TILE_GROUPS = V7X_LANES // 16
S5_CHUNK = 16
CUMSUM_BLOCK = 256


def _cparams(sem, vmem_est):
    limit = int(min(max(vmem_est * 5 // 4 + (4 << 20), 32 << 20), V7X_VMEM_BYTES - (6 << 20)))
    return pltpu.CompilerParams(dimension_semantics=sem, vmem_limit_bytes=limit)


def _nbytes(shape, dtype):
    return math.prod(shape) * jnp.dtype(dtype).itemsize


def _adaln_kernel(c_ref, w_ref, b_ref, o_ref):
    cv = c_ref[...]
    a = (cv * jax.nn.sigmoid(cv)).astype(BF16)
    o_ref[0] = jnp.dot(a, w_ref[0].astype(BF16), preferred_element_type=F32) + b_ref[0]


def _adaln(c8, ada_w, ada_b, tn=512):
    depth, d, n6 = ada_w.shape
    rows = c8.shape[0]
    tn = min(tn, n6)
    est = 2 * _nbytes((d, tn), F32) + 4 * _nbytes((rows, tn), F32) + _nbytes((rows, d), F32) * 2
    return pl.pallas_call(
        _adaln_kernel,
        grid=(depth, n6 // tn),
        in_specs=[
            pl.BlockSpec((rows, d), lambda l, j: (0, 0)),
            pl.BlockSpec((1, d, tn), lambda l, j: (l, 0, j)),
            pl.BlockSpec((1, 1, tn), lambda l, j: (l, 0, j)),
        ],
        out_specs=pl.BlockSpec((1, rows, tn), lambda l, j: (l, 0, j)),
        out_shape=jax.ShapeDtypeStruct((depth, rows, n6), F32),
        compiler_params=_cparams(("parallel", "parallel"), est),
        name="adaln",
    )(c8, ada_w, ada_b.reshape(depth, 1, n6))


def _rms_modulate(xf, g, shift, scale, eps):
    ms = jnp.mean(xf * xf, axis=-1, keepdims=True)
    y = xf * lax.rsqrt(ms + eps) * g
    if shift is not None:
        y = y * (1.0 + scale) + shift
    return y


def _norm_kernel(*refs, eps, modulate, router, add_pos, emit_h):
    x_ref, g_ref = refs[0], refs[1]
    k = 2
    xf = x_ref[...]
    if add_pos:
        xf = xf + refs[k][...]
        k += 1
        refs[-1][...] = xf
    shift = scale = None
    if modulate:
        shift, scale = refs[k][0], refs[k + 1][0]
        k += 2
    y = _rms_modulate(xf, g_ref[...], shift, scale, eps)
    if router:
        wr_ref = refs[k]
        k += 1
    if emit_h:
        refs[k][...] = y.astype(refs[k].dtype)
        k += 1
    if router:
        refs[k][...] = lax.dot_general(
            wr_ref[...], y.astype(BF16), (((1,), (1,)), ((), ())), preferred_element_type=F32)


def _norm(x2, g, shift=None, scale=None, rows_per_batch=None, wr_t=None, out_dtype=BF16, tm=256,
          pos=None, emit_h=True):
    m, d = x2.shape
    modulate = shift is not None
    router = wr_t is not None
    add_pos = pos is not None
    rpb = rows_per_batch if rows_per_batch is not None else m
    tm = min(tm, rpb)
    tpb = rpb // tm
    in_specs = [pl.BlockSpec((tm, d), lambda i: (i, 0)), pl.BlockSpec((1, d), lambda i: (0, 0))]
    args = [x2, g.reshape(1, d)]
    if add_pos:
        in_specs.append(pl.BlockSpec((tm, d), lambda i: (i % tpb, 0)))
        args.append(pos)
    if modulate:
        in_specs += [pl.BlockSpec((1, 1, d), lambda i: (i // tpb, 0, 0))] * 2
        args += [shift, scale]
    out_specs, out_shape = [], []
    if emit_h:
        out_specs.append(pl.BlockSpec((tm, d), lambda i: (i, 0)))
        out_shape.append(jax.ShapeDtypeStruct((m, d), out_dtype))
    if router:
        e = wr_t.shape[0]
        in_specs.append(pl.BlockSpec((e, d), lambda i: (0, 0)))
        args.append(wr_t)
        out_specs.append(pl.BlockSpec((e, tm), lambda i: (0, i)))
        out_shape.append(jax.ShapeDtypeStruct((e, m), F32))
    est = 2 * _nbytes((tm, d), F32) + 2 * _nbytes((tm, d), out_dtype) + 3 * _nbytes((tm, d), F32)
    if add_pos:
        out_specs.append(pl.BlockSpec((tm, d), lambda i: (i, 0)))
        out_shape.append(jax.ShapeDtypeStruct((m, d), F32))
        est += 4 * _nbytes((tm, d), F32)
    res = pl.pallas_call(
        functools.partial(_norm_kernel, eps=OP["rms_eps"], modulate=modulate, router=router,
                          add_pos=add_pos, emit_h=emit_h),
        grid=(m // tm,),
        in_specs=in_specs,
        out_specs=out_specs,
        out_shape=out_shape,
        compiler_params=_cparams(("parallel",), est),
        name="rmsnorm",
    )(*args)
    return res if len(res) > 1 else res[0]


def _mm_kernel(*refs, n_pairs, kinds, epilogue):
    accs = []
    for k in range(n_pairs):
        a = refs[2 * k][...].astype(BF16)
        w = refs[2 * k + 1][...].astype(BF16)
        accs.append(jnp.dot(a, w, preferred_element_type=F32))
    ex = []
    for k, kind in enumerate(kinds):
        r = refs[2 * n_pairs + k]
        ex.append(r[0] if kind == "row" else r[...])
    o_ref = refs[-1]
    o_ref[...] = epilogue(accs, ex).astype(o_ref.dtype)


def _mm(pairs, extras, epilogue, n_out, out_dtype, tm, tn, rows_per_batch=None, alias_extra=None,
        name="mm"):
    m = pairs[0][0].shape[0]
    rpb = rows_per_batch if rows_per_batch is not None else m
    tm = min(tm, rpb)
    tn = min(tn, n_out)
    tpb = rpb // tm
    in_specs, args = [], []
    est = 2 * _nbytes((tm, tn), out_dtype) + 2 * _nbytes((tm, tn), F32) * max(1, len(pairs))
    for a, w, layer, off in pairs:
        kdim = a.shape[1]
        in_specs.append(pl.BlockSpec((tm, kdim), lambda i, j: (i, 0)))
        in_specs.append(pl.BlockSpec((None, kdim, tn),
                                     lambda i, j, off=off, layer=layer: (layer, 0, j + off)))
        args += [a, w]
        est += 2 * _nbytes((tm, kdim), a.dtype) + 2 * _nbytes((kdim, tn), w.dtype)
        if w.dtype != BF16:
            est += _nbytes((kdim, tn), BF16)
    kinds = []
    for kind, arr, off in extras:
        kinds.append(kind)
        if kind == "tile":
            in_specs.append(pl.BlockSpec((tm, tn), lambda i, j, off=off: (i, j + off)))
            est += 2 * _nbytes((tm, tn), arr.dtype)
        else:
            in_specs.append(pl.BlockSpec((1, 1, tn), lambda i, j, off=off: (i // tpb, 0, j + off)))
        args.append(arr)
    aliases = {}
    if alias_extra is not None:
        aliases = {2 * len(pairs) + alias_extra: 0}
    return pl.pallas_call(
        functools.partial(_mm_kernel, n_pairs=len(pairs), kinds=tuple(kinds), epilogue=epilogue),
        grid=(m // tm, n_out // tn),
        in_specs=in_specs,
        out_specs=pl.BlockSpec((tm, tn), lambda i, j: (i, j)),
        out_shape=jax.ShapeDtypeStruct((m, n_out), out_dtype),
        input_output_aliases=aliases,
        compiler_params=_cparams(("parallel", "parallel"), est),
        name=name,
    )(*args)


def _s5_operators(lam_re, lam_im, log_dt, b_re, b_im, c_re, c_im, d_skip, t):
    n = OP["s5_group_dim"]
    lr = jnp.minimum(lam_re.astype(F32), OP["lambda_re_max"])
    li = lam_im.astype(F32)
    dt = jnp.exp(log_dt.astype(F32))[..., None]
    g, p = lr.shape[1], lr.shape[2]
    gt = S5_TILE_GROUPS
    j = g // gt
    mag = jnp.exp(lr * dt)
    a_re = mag * jnp.cos(li * dt)
    a_im = mag * jnp.sin(li * dt)
    num_re, num_im = a_re - 1.0, a_im
    den = lr * lr + li * li
    f_re = (num_re * lr + num_im * li) / den
    f_im = (num_im * lr - num_re * li) / den
    br, bi = b_re.astype(F32), b_im.astype(F32)
    bb_re = f_re[..., None] * br - f_im[..., None] * bi
    bb_im = f_re[..., None] * bi + f_im[..., None] * br
    cr, ci = c_re.astype(F32), c_im.astype(F32)

    k = jnp.arange(t + 1, dtype=F32)[:, None, None, None]
    pw_mag = jnp.exp(lr[None] * dt[None] * k)
    pw_re = pw_mag * jnp.cos(li[None] * dt[None] * k)
    pw_im = pw_mag * jnp.sin(li[None] * dt[None] * k)

    ca_re = cr[None] * pw_re[:t, :, :, None, :] - ci[None] * pw_im[:t, :, :, None, :]
    ca_im = cr[None] * pw_im[:t, :, :, None, :] + ci[None] * pw_re[:t, :, :, None, :]
    bt_re, bt_im = bb_re.transpose(0, 1, 3, 2), bb_im.transpose(0, 1, 3, 2)
    kk = jnp.sum(ca_re[:, :, :, :, None, :] * bt_re[None, :, :, None, :, :]
                 - ca_im[:, :, :, :, None, :] * bt_im[None, :, :, None, :, :], axis=-1)
    kf, kb = kk[:, 0], kk[:, 1]
    k0 = kf[0] + kb[0] + d_skip.astype(F32).reshape(g, n)[:, :, None] * jnp.eye(n, dtype=F32)
    kall = jnp.concatenate([kb[1:][::-1], k0[None], kf[1:]], axis=0)
    kc = kall.reshape(2 * t - 1, j, gt, n, n).transpose(1, 0, 4, 2, 3)
    kc = kc.reshape(j, 2 * t - 1, n, gt * n).astype(BF16)
    lane = jnp.arange(gt * n)
    same_group = (lane[:, None] // n) == (lane[None, :] // n)
    bd = jnp.where(same_group, jnp.tile(kc, (1, 1, gt, 1)), 0)

    ps_re = jnp.stack([pw_re[:t][::-1, 0], pw_re[:t, 1]], axis=0)
    ps_im = jnp.stack([pw_im[:t][::-1, 0], pw_im[:t, 1]], axis=0)
    w_re = ps_re[..., None] * bb_re[:, None] - ps_im[..., None] * bb_im[:, None]
    w_im = ps_re[..., None] * bb_im[:, None] + ps_im[..., None] * bb_re[:, None]
    w6 = jnp.stack([w_re, w_im], axis=1).reshape(2, 2, t, j, gt, p, n)
    wc = w6.transpose(3, 2, 6, 0, 1, 4, 5).reshape(j, t, n, 4 * gt * p).astype(BF16)
    col_group = (jnp.arange(4 * gt * p) // p) % gt
    mask_w = ((lane[:, None] // n) == col_group[None, :]).astype(BF16)

    po_re = jnp.stack([pw_re[1:, 0], pw_re[1:][::-1, 1]], axis=0)
    po_im = jnp.stack([pw_im[1:, 0], pw_im[1:][::-1, 1]], axis=0)
    co_re = cr[:, None] * po_re[:, :, :, None, :] - ci[:, None] * po_im[:, :, :, None, :]
    co_im = cr[:, None] * po_im[:, :, :, None, :] + ci[:, None] * po_re[:, :, :, None, :]
    o6 = jnp.stack([co_re, -co_im], axis=1).reshape(2, 2, t, j, gt, n, p)
    oc = o6.transpose(3, 0, 1, 6, 2, 4, 5).reshape(j, 4, p, t * gt * n).astype(BF16)
    out_group = (jnp.arange(t * gt * n) // n) % gt
    mask_o = ((jnp.arange(gt * p)[:, None] // p) == out_group[None, :]).astype(BF16)

    at = jnp.stack([pw_re[t, 0], pw_im[t, 0], pw_re[t, 1], pw_im[t, 1]], axis=0)
    at = at.reshape(4, j, gt * p).transpose(1, 0, 2).reshape(j, 1, 4 * gt * p)
    return bd, wc, oc, mask_w, mask_o, at


def _gelu_tanh(y):
    return 0.5 * y * (1.0 + jnp.tanh(0.7978845608028654 * (y + 0.044715 * y * y * y)))


def _s5_kernel(u_ref, bd_ref, wc_ref, oc_ref, mw_ref, mo_ref, at_ref, h0_ref, o_ref, hfin_ref,
               mt_scr, wst_scr, wout_scr, s_scr, hin_scr, *, n_chunks, half, t):
    lanes = u_ref.shape[1]

    @pl.when(pl.program_id(1) == 0)
    def _():
        for s in range(t):
            for k in range(t):
                mt_scr[s * lanes:(s + 1) * lanes, k * lanes:(k + 1) * lanes] = bd_ref[0, k - s + t - 1]
            rep = lanes // wc_ref.shape[2]
            wst_scr[s * lanes:(s + 1) * lanes, :] = jnp.tile(wc_ref[0, s], (rep, 1)) * mw_ref[...]
        rows = mo_ref.shape[0]
        for q in range(oc_ref.shape[1]):
            rep = rows // oc_ref.shape[2]
            wout_scr[q * rows:(q + 1) * rows, :] = jnp.tile(oc_ref[0, q], (rep, 1)) * mo_ref[...]

    u = jnp.concatenate([u_ref[pl.ds(s, n_chunks, stride=t), :] for s in range(t)],
                        axis=1).astype(BF16)
    s_scr[...] = jnp.dot(u, wst_scr[...], preferred_element_type=F32)
    at = at_ref[0]
    afr, afi = at[:, 0:half], at[:, half:2 * half]
    abr, abi = at[:, 2 * half:3 * half], at[:, 3 * half:4 * half]
    h0 = h0_ref[0, 0]
    init = (h0[:, 0:half], h0[:, half:2 * half], h0[:, 2 * half:3 * half], h0[:, 3 * half:4 * half])

    def body(i, carry):
        hfr, hfi, hbr, hbi = carry
        cb = n_chunks - 1 - i
        hin_scr[pl.ds(i, 1), 0:half] = hfr
        hin_scr[pl.ds(i, 1), half:2 * half] = hfi
        hin_scr[pl.ds(cb, 1), 2 * half:3 * half] = hbr
        hin_scr[pl.ds(cb, 1), 3 * half:4 * half] = hbi
        sf = s_scr[pl.ds(i, 1), 0:2 * half]
        sb = s_scr[pl.ds(cb, 1), 2 * half:4 * half]
        nfr = afr * hfr - afi * hfi + sf[:, 0:half]
        nfi = afr * hfi + afi * hfr + sf[:, half:2 * half]
        nbr = abr * hbr - abi * hbi + sb[:, 0:half]
        nbi = abr * hbi + abi * hbr + sb[:, half:2 * half]
        return nfr, nfi, nbr, nbi

    hfr, hfi, hbr, hbi = lax.fori_loop(0, n_chunks, body, init)
    hfin_ref[0, 0] = jnp.concatenate([hfr, hfi, hbr, hbi], axis=1)
    y = (jnp.dot(u, mt_scr[...], preferred_element_type=F32)
         + jnp.dot(hin_scr[...].astype(BF16), wout_scr[...], preferred_element_type=F32))
    for k in range(t):
        o_ref[pl.ds(k, n_chunks, stride=t), :] = _gelu_tanh(
            y[:, k * lanes:(k + 1) * lanes]).astype(o_ref.dtype)


def _s5(u2, ds5, batch, ops, h0):
    bd, wc, oc, mask_w, mask_o, at = ops
    m = u2.shape[0]
    l = m // batch
    t = S5_CHUNK
    c = l // t
    j = ds5 // V7X_LANES
    w = t * V7X_LANES
    sw = wc.shape[3]
    compact = sum(_nbytes(a.shape[1:], BF16) for a in (bd, wc, oc)) + _nbytes(mask_w.shape, BF16) \
        + _nbytes(mask_o.shape, BF16)
    est = (3 * _nbytes((w, sw), BF16) + 2 * compact + 4 * _nbytes((l, V7X_LANES), F32)
           + 2 * _nbytes((c, sw), F32) + 3 * _nbytes((c, w), F32) + _nbytes((c, sw), BF16)
           + 2 * _nbytes((c, w), BF16))
    return pl.pallas_call(
        functools.partial(_s5_kernel, n_chunks=c, half=sw // 4, t=t),
        grid=(j, batch),
        in_specs=[
            pl.BlockSpec((l, V7X_LANES), lambda jj, b: (b, jj)),
            pl.BlockSpec((1,) + bd.shape[1:], lambda jj, b: (jj, 0, 0, 0)),
            pl.BlockSpec((1,) + wc.shape[1:], lambda jj, b: (jj, 0, 0, 0)),
            pl.BlockSpec((1,) + oc.shape[1:], lambda jj, b: (jj, 0, 0, 0)),
            pl.BlockSpec(mask_w.shape, lambda jj, b: (0, 0)),
            pl.BlockSpec(mask_o.shape, lambda jj, b: (0, 0)),
            pl.BlockSpec((1, 1, sw), lambda jj, b: (jj, 0, 0)),
            pl.BlockSpec((1, 1, 1, sw), lambda jj, b: (jj, b, 0, 0)),
        ],
        out_specs=[
            pl.BlockSpec((l, V7X_LANES), lambda jj, b: (b, jj)),
            pl.BlockSpec((1, 1, 1, sw), lambda jj, b: (jj, b, 0, 0)),
        ],
        out_shape=[
            jax.ShapeDtypeStruct((m, ds5), F32),
            jax.ShapeDtypeStruct((j, batch, 1, sw), F32),
        ],
        scratch_shapes=[pltpu.VMEM((w, w), BF16), pltpu.VMEM((w, sw), BF16), pltpu.VMEM((sw, w), BF16),
                        pltpu.VMEM((c, sw), F32), pltpu.VMEM((c, sw), F32)],
        compiler_params=_cparams(("arbitrary", "arbitrary"), est),
        name="s5",
    )(u2, bd, wc, oc, mask_w, mask_o, at, h0)


def _dft_mats(n):
    hi = n // V7X_LANES
    j = jnp.arange(n, dtype=jnp.int32)[:, None]
    a1 = ((j * jnp.arange(V7X_LANES, dtype=jnp.int32)[None, :]) % n).astype(F32) * (2.0 * math.pi / n)
    a2 = ((j * jnp.arange(hi, dtype=jnp.int32)[None, :]) % hi).astype(F32) * (2.0 * math.pi / hi)
    c1, s1 = jnp.cos(a1)[:, None, :], jnp.sin(a1)[:, None, :]
    c2, s2 = jnp.cos(a2)[:, :, None], jnp.sin(a2)[:, :, None]
    s = 1.0 / math.sqrt(n)
    cos = ((c2 * c1 - s2 * s1) * s).astype(BF16).reshape(n, n)
    sin = ((s2 * c1 + c2 * s1) * s).astype(BF16).reshape(n, n)
    return cos, sin


def _chan_dft_kernel(v_ref, w_ref, xc_ref, xs_ref, *, gd):
    r = jnp.dot(v_ref[...].astype(BF16), w_ref[...], preferred_element_type=F32)
    xc_ref[...] = r[:, :gd].astype(xc_ref.dtype)
    xs_ref[...] = r[:, gd:].astype(xs_ref.dtype)


def _chan_dft(v2, col_off, dft, wc, tm=1024):
    m = v2.shape[0]
    gd = wc.shape[0]
    tm = min(tm, m)
    goff = col_off // gd
    spec = pl.BlockSpec((tm, gd), lambda i, g: (i, g))
    est = (2 * _nbytes((tm, gd), F32) + 5 * _nbytes((tm, gd), BF16) + 2 * _nbytes((gd, 2 * gd), BF16)
           + 2 * _nbytes((tm, 2 * gd), F32))
    return pl.pallas_call(
        functools.partial(_chan_dft_kernel, gd=gd),
        grid=(m // tm, dft // gd),
        in_specs=[pl.BlockSpec((tm, gd), lambda i, g: (i, goff + g)),
                  pl.BlockSpec((gd, 2 * gd), lambda i, g: (0, 0))],
        out_specs=[spec, spec],
        out_shape=[jax.ShapeDtypeStruct((m, dft), BF16)] * 2,
        compiler_params=_cparams(("parallel", "parallel"), est),
        name="chan_dft",
    )(v2, wc)


def _seq_dft_kernel(cl_ref, sl_ref, xc_ref, xs_ref, o_ref):
    y = (jnp.dot(cl_ref[...], xc_ref[0], preferred_element_type=F32)
         - jnp.dot(sl_ref[...], xs_ref[0], preferred_element_type=F32))
    o_ref[0] = y.astype(o_ref.dtype)


def _seq_dft(xc, xs, cl, sl, batch, tm=512, tn=512):
    m, dft = xc.shape
    l = m // batch
    tm = min(tm, l)
    tn = min(tn, dft)
    xc3, xs3 = xc.reshape(batch, l, dft), xs.reshape(batch, l, dft)
    est = 4 * _nbytes((tm, l), BF16) + 4 * _nbytes((l, tn), BF16) + 4 * _nbytes((tm, tn), F32)
    out = pl.pallas_call(
        _seq_dft_kernel,
        grid=(batch, dft // tn, l // tm),
        in_specs=[
            pl.BlockSpec((tm, l), lambda b, jn, i: (i, 0)),
            pl.BlockSpec((tm, l), lambda b, jn, i: (i, 0)),
            pl.BlockSpec((1, l, tn), lambda b, jn, i: (b, 0, jn)),
            pl.BlockSpec((1, l, tn), lambda b, jn, i: (b, 0, jn)),
        ],
        out_specs=pl.BlockSpec((1, tm, tn), lambda b, jn, i: (b, i, jn)),
        out_shape=jax.ShapeDtypeStruct((batch, l, dft), BF16),
        compiler_params=_cparams(("parallel", "parallel", "parallel"), est),
        name="seq_dft",
    )(cl, sl, xc3, xs3)
    return out.reshape(m, dft)


def _prefix_count(mask_f32, tri):
    e, l = mask_f32.shape
    carry = jnp.zeros((e, 1), F32)
    outs = []
    for k in range(l // CUMSUM_BLOCK):
        blk = mask_f32[:, k * CUMSUM_BLOCK:(k + 1) * CUMSUM_BLOCK]
        outs.append(jnp.dot(blk.astype(BF16), tri, preferred_element_type=F32) + carry)
        carry = carry + jnp.sum(blk, axis=1, keepdims=True)
    return outs[0] if len(outs) == 1 else jnp.concatenate(outs, axis=1)


def _route_kernel(lg_ref, posm_ref, aff_ref, bnd_ref, *, cap, tile):
    lg = lg_ref[...]
    mx = jnp.max(lg, axis=0, keepdims=True)
    ex = jnp.exp(lg - mx)
    aff = ex / jnp.sum(ex, axis=0, keepdims=True)
    bits = pltpu.bitcast(aff, jnp.int32)
    e = lg.shape[0]
    v = jnp.zeros((e, 1), jnp.int32)
    for bit in range(30, -1, -1):
        cand = v | (1 << bit)
        cnt = jnp.sum(jnp.where(bits >= cand, 1.0, 0.0), axis=1, keepdims=True)
        v = jnp.where(cnt >= cap, cand, v)
    gt = bits > v
    eq = bits == v
    n_gt = jnp.sum(gt.astype(F32), axis=1, keepdims=True)
    r = lax.broadcasted_iota(jnp.int32, (CUMSUM_BLOCK, CUMSUM_BLOCK), 0)
    c = lax.broadcasted_iota(jnp.int32, (CUMSUM_BLOCK, CUMSUM_BLOCK), 1)
    tri = jnp.where(r < c, 1.0, 0.0).astype(BF16)
    eq_f = jnp.where(eq, 1.0, 0.0)
    tie_ok = _prefix_count(eq_f, tri) < (cap - n_gt)
    sel_f = jnp.where(gt, 1.0, jnp.where(tie_ok, eq_f, 0.0))
    pos = _prefix_count(sel_f, tri)
    posm_ref[0] = jnp.where(sel_f > 0.5, pos, -1.0).astype(jnp.int32)
    aff_ref[0] = aff
    l = lg.shape[1]
    starts = [pos[:, k * tile:k * tile + 1] for k in range(l // tile)]
    starts.append(jnp.full((e, 1), float(cap), F32))
    bnd_ref[0] = jnp.concatenate(starts, axis=1).astype(jnp.int32)


def _route(lg_t, batch, cap, tile):
    e, m = lg_t.shape
    l = m // batch
    nt = l // tile
    est = 16 * _nbytes((e, l), F32)
    return pl.pallas_call(
        functools.partial(_route_kernel, cap=cap, tile=tile),
        grid=(batch,),
        in_specs=[pl.BlockSpec((e, l), lambda b: (0, b))],
        out_specs=[pl.BlockSpec((1, e, l), lambda b: (b, 0, 0))] * 2
        + [pl.BlockSpec((1, e, nt + 1), lambda b: (b, 0, 0))],
        out_shape=[jax.ShapeDtypeStruct((batch, e, l), jnp.int32),
                   jax.ShapeDtypeStruct((batch, e, l), F32),
                   jax.ShapeDtypeStruct((batch, e, nt + 1), jnp.int32)],
        compiler_params=_cparams(("parallel",), est),
        name="route",
    )(lg_t)


def _slots_kernel(posm_ref, aff_ref, *rest, cap):
    idx_ref, gs_ref = rest[-2], rest[-1]
    pm = posm_ref[0, 0]
    l = pm.shape[1]
    slot = lax.broadcasted_iota(jnp.int32, (cap, l), 0)
    tok = lax.broadcasted_iota(jnp.int32, (1, l), 1).astype(F32)
    hit = pm == slot
    idx_ref[0, 0] = jnp.sum(jnp.where(hit, tok, 0.0), axis=1, keepdims=True).astype(jnp.int32)
    gs_ref[0] = jnp.sum(jnp.where(hit, aff_ref[0, 0], 0.0), axis=1, keepdims=True)


def _slots(posm, aff, cap, n_rows, row_off, gs_all=None):
    batch, e, l = posm.shape
    blk = row_off // cap
    in_specs = [pl.BlockSpec((1, 1, 1, l), lambda b, ee: (b, ee, 0, 0))] * 2
    args = [posm.reshape(batch, e, 1, l), aff.reshape(batch, e, 1, l)]
    aliases = {}
    if gs_all is not None:
        in_specs.append(pl.BlockSpec(memory_space=pl.ANY))
        args.append(gs_all)
        aliases = {2: 1}
    est = 6 * _nbytes((cap, l), F32) + 4 * _nbytes((cap, V7X_LANES), F32)
    return pl.pallas_call(
        functools.partial(_slots_kernel, cap=cap),
        grid=(batch, e),
        in_specs=in_specs,
        out_specs=[pl.BlockSpec((1, 1, cap, 1), lambda b, ee: (b, ee, 0, 0)),
                   pl.BlockSpec((1, cap, 1), lambda b, ee: (ee, blk + b, 0))],
        out_shape=[jax.ShapeDtypeStruct((batch, e, cap, 1), jnp.int32),
                   jax.ShapeDtypeStruct((e, n_rows, 1), F32)],
        input_output_aliases=aliases,
        compiler_params=_cparams(("parallel", "parallel"), est),
        name="moe_slots",
    )(*args)


GATHER_ROW_CHUNK = 64


def _gather_kernel(idx_ref, x_hbm, g_ref, sh_ref, sc_ref, *rest, cap, seq, n_exp, n_steps, eps):
    o_ref, rows_scr, sems = rest[-3], rest[-2], rest[-1]
    n = pl.program_id(0) * n_exp + pl.program_id(1)
    cur = n % 2
    nxt_step = jnp.minimum(n + 1, n_steps - 1)
    ch = min(GATHER_ROW_CHUNK, cap)

    def issue_rows(step, buf, lo):
        base = step * cap
        row0 = (step // n_exp) * seq
        for k in range(ch):
            src = x_hbm.at[pl.ds(row0 + idx_ref[base + lo + k], 1)]
            pltpu.make_async_copy(src, rows_scr.at[buf, pl.ds(lo + k, 1)], sems.at[buf]).start()

    def wait_rows(buf):
        pltpu.make_async_copy(x_hbm.at[pl.ds(0, cap)], rows_scr.at[buf], sems.at[buf]).wait()

    @pl.when(n == 0)
    def _():
        def first(c, carry):
            issue_rows(0, 0, c * ch)
            return carry
        lax.fori_loop(0, cap // ch, first, 0)

    wait_rows(cur)
    g, shift, scale = g_ref[...], sh_ref[0], sc_ref[0]

    def body(c, carry):
        r = pl.multiple_of(c * ch, ch)
        issue_rows(nxt_step, 1 - cur, r)
        y = _rms_modulate(rows_scr[cur, pl.ds(r, ch), :], g, shift, scale, eps)
        o_ref[0, pl.ds(r, ch), :] = y.astype(o_ref.dtype)
        return carry

    lax.fori_loop(0, cap // ch, body, 0)

    @pl.when(n == n_steps - 1)
    def _():
        wait_rows(1 - cur)


def _gather(idx, x2, g, shift, scale, cap, n_rows, row_off, xg_all=None):
    batch, e = idx.shape[0], idx.shape[1]
    m, d = x2.shape
    seq = m // batch
    blk = row_off // cap
    in_specs = [
        pl.BlockSpec(memory_space=pl.ANY),
        pl.BlockSpec((1, d), lambda b, ee, ix: (0, 0)),
        pl.BlockSpec((1, 1, d), lambda b, ee, ix: (b, 0, 0)),
        pl.BlockSpec((1, 1, d), lambda b, ee, ix: (b, 0, 0)),
    ]
    args = [idx.reshape(-1), x2, g.reshape(1, d), shift, scale]
    aliases = {}
    if xg_all is not None:
        in_specs.append(pl.BlockSpec(memory_space=pl.ANY))
        args.append(xg_all)
        aliases = {5: 0}
    est = 2 * _nbytes((cap, d), F32) + 2 * _nbytes((cap, d), BF16) + 8 * _nbytes((GATHER_ROW_CHUNK, d), F32)
    return pl.pallas_call(
        functools.partial(_gather_kernel, cap=cap, seq=seq, n_exp=e, n_steps=batch * e,
                          eps=OP["rms_eps"]),
        grid_spec=pltpu.PrefetchScalarGridSpec(
            num_scalar_prefetch=1,
            grid=(batch, e),
            in_specs=in_specs,
            out_specs=pl.BlockSpec((1, cap, d), lambda b, ee, ix: (ee, blk + b, 0)),
            scratch_shapes=[pltpu.VMEM((2, cap, d), F32), pltpu.SemaphoreType.DMA((2,))],
        ),
        out_shape=jax.ShapeDtypeStruct((e, n_rows, d), BF16),
        input_output_aliases=aliases,
        compiler_params=_cparams(("arbitrary", "arbitrary"), est),
        name="moe_gather",
    )(*args)


EXPERT_ROW_BLOCKS = 2


def _up_kernel(xg_ref, wg_ref, wu_ref, o_ref, wg_bf, wu_bf):
    @pl.when(pl.program_id(2) == 0)
    def _():
        wg_bf[...] = wg_ref[0, 0].astype(BF16)
        wu_bf[...] = wu_ref[0, 0].astype(BF16)

    x = xg_ref[0]
    g = jnp.dot(x, wg_bf[...], preferred_element_type=F32)
    u = jnp.dot(x, wu_bf[...], preferred_element_type=F32)
    o_ref[0] = (g * jax.nn.sigmoid(g) * u).astype(o_ref.dtype)


def _expert_up(xg, w_gate, w_up, layer, tf=256):
    e, r, d = xg.shape
    f = w_gate.shape[3]
    tf = min(tf, f)
    tr = r // EXPERT_ROW_BLOCKS
    est = (4 * _nbytes((d, tf), F32) + 3 * _nbytes((d, tf), BF16) + 2 * _nbytes((tr, d), BF16)
           + 6 * _nbytes((tr, tf), F32))
    return pl.pallas_call(
        _up_kernel,
        grid=(e, f // tf, EXPERT_ROW_BLOCKS),
        in_specs=[
            pl.BlockSpec((1, tr, d), lambda ee, fj, rr: (ee, rr, 0)),
            pl.BlockSpec((1, 1, d, tf), lambda ee, fj, rr: (layer, ee, 0, fj)),
            pl.BlockSpec((1, 1, d, tf), lambda ee, fj, rr: (layer, ee, 0, fj)),
        ],
        out_specs=pl.BlockSpec((1, tr, tf), lambda ee, fj, rr: (ee, rr, fj)),
        out_shape=jax.ShapeDtypeStruct((e, r, f), BF16),
        scratch_shapes=[pltpu.VMEM((d, tf), BF16), pltpu.VMEM((d, tf), BF16)],
        compiler_params=_cparams(("parallel", "parallel", "arbitrary"), est),
        name="moe_up",
    )(xg, w_gate, w_up)


def _down_kernel(h_ref, wd_ref, gs_ref, o_ref, wd_bf):
    @pl.when(pl.program_id(2) == 0)
    def _():
        wd_bf[...] = wd_ref[0, 0].astype(BF16)

    y = jnp.dot(h_ref[0], wd_bf[...], preferred_element_type=F32)
    o_ref[0] = (y * gs_ref[0]).astype(o_ref.dtype)


def _expert_down(hid, w_down, gslot, layer, td=1024):
    e, r, f = hid.shape
    d = w_down.shape[3]
    td = min(td, d)
    tr = r // EXPERT_ROW_BLOCKS
    est = (2 * _nbytes((f, td), F32) + 2 * _nbytes((f, td), BF16) + 2 * _nbytes((tr, f), BF16)
           + 4 * _nbytes((tr, td), F32) + 2 * _nbytes((tr, V7X_LANES), F32))
    return pl.pallas_call(
        _down_kernel,
        grid=(e, d // td, EXPERT_ROW_BLOCKS),
        in_specs=[
            pl.BlockSpec((1, tr, f), lambda ee, dj, rr: (ee, rr, 0)),
            pl.BlockSpec((1, 1, f, td), lambda ee, dj, rr: (layer, ee, 0, dj)),
            pl.BlockSpec((1, tr, 1), lambda ee, dj, rr: (ee, rr, 0)),
        ],
        out_specs=pl.BlockSpec((1, tr, td), lambda ee, dj, rr: (ee, rr, dj)),
        out_shape=jax.ShapeDtypeStruct((e, r, d), BF16),
        scratch_shapes=[pltpu.VMEM((f, td), BF16)],
        compiler_params=_cparams(("parallel", "parallel", "arbitrary"), est),
        name="moe_down",
    )(hid, w_down, gslot)


COMBINE_TILE = 256
DMA_ROW_ALIGN = 16


def combine_window(cap, seq, tile):
    return min(cap, max(DMA_ROW_ALIGN, 2 * tile * cap // seq))


def _combine_kernel(tbl_ref, pt_ref, y_hbm, x_ref, g_ref, o_ref, stage, extra, acc_scr, sems, xsem,
                    *, n_exp, cap, win, nt, row_off, n_steps):
    n = pl.program_id(0) * nt + pl.program_id(1)
    cur = n % 2
    nxt_step = jnp.minimum(n + 1, n_steps - 1)

    def window_starts(step):
        b, i = step // nt, step % nt
        t0 = (b * (nt + 1) + i) * n_exp
        out = []
        for ee in range(n_exp):
            s0a = jnp.minimum((tbl_ref[t0 + ee] // DMA_ROW_ALIGN) * DMA_ROW_ALIGN, cap - win)
            out.append(pl.multiple_of(s0a, DMA_ROW_ALIGN))
        return row_off + b * cap, out, t0

    def issue_windows(step, buf):
        row_base, starts, _ = window_starts(step)
        for ee in range(n_exp):
            pltpu.make_async_copy(y_hbm.at[ee, pl.ds(row_base + starts[ee], win), :],
                                  stage.at[buf, pl.ds(ee * win, win), :], sems.at[buf]).start()

    def wait_windows(buf):
        pltpu.make_async_copy(y_hbm.at[0, pl.ds(0, n_exp * win), :], stage.at[buf],
                              sems.at[buf]).wait()

    @pl.when(n == 0)
    def _():
        issue_windows(0, 0)

    issue_windows(nxt_step, 1 - cur)
    row_base, starts, t0 = window_starts(n)
    pt = pt_ref[...]
    lane = lax.broadcasted_iota(jnp.int32, (pt.shape[0], win), 1)
    p = jnp.concatenate(
        [jnp.where(pt[:, ee:ee + 1] - starts[ee] == lane, 1.0, 0.0).astype(BF16)
         for ee in range(n_exp)], axis=1)
    wait_windows(cur)
    acc_scr[...] = jnp.dot(p, stage[cur], preferred_element_type=F32)

    for ee in range(n_exp):
        first = starts[ee] + win
        n_extra = jnp.maximum(0, (tbl_ref[t0 + n_exp + ee] - first + win - 1) // win)

        def extra_window(w, carry, ee=ee, first=first):
            lo = first + w * win
            src0 = pl.multiple_of(jnp.minimum(lo, cap - win), DMA_ROW_ALIGN)
            cp = pltpu.make_async_copy(y_hbm.at[ee, pl.ds(row_base + src0, win), :], extra, xsem)
            cp.start()
            cp.wait()
            col = pt[:, ee:ee + 1]
            hit = jnp.where(col >= lo, col - src0, -1) == lane
            acc_scr[...] += jnp.dot(jnp.where(hit, 1.0, 0.0).astype(BF16), extra[...],
                                    preferred_element_type=F32)
            return carry

        lax.fori_loop(0, n_extra, extra_window, 0)

    o_ref[...] = x_ref[...] + g_ref[0] * acc_scr[...]

    @pl.when(n == n_steps - 1)
    def _():
        wait_windows(1 - cur)


def _combine(posm_t, bounds, y, x2, gate, batch, cap, row_off):
    m, d = x2.shape
    l = m // batch
    e = posm_t.shape[1]
    tm = min(COMBINE_TILE, l)
    nt = l // tm
    win = combine_window(cap, l, tm)
    assert cap % DMA_ROW_ALIGN == 0 and win % DMA_ROW_ALIGN == 0 and bounds.shape == (batch, e, nt + 1)
    tbl = bounds.transpose(0, 2, 1).reshape(-1)
    est = (2 * _nbytes((e * win, d), BF16) + _nbytes((win, d), BF16) + 7 * _nbytes((tm, d), F32)
           + 2 * _nbytes((tm, e * win), BF16) + 2 * _nbytes((tm, V7X_LANES), F32))
    return pl.pallas_call(
        functools.partial(_combine_kernel, n_exp=e, cap=cap, win=win, nt=nt, row_off=row_off,
                          n_steps=batch * nt),
        grid_spec=pltpu.PrefetchScalarGridSpec(
            num_scalar_prefetch=1,
            grid=(batch, nt),
            in_specs=[
                pl.BlockSpec((tm, e), lambda b, i, t: (b * nt + i, 0)),
                pl.BlockSpec(memory_space=pl.ANY),
                pl.BlockSpec((tm, d), lambda b, i, t: (b * nt + i, 0)),
                pl.BlockSpec((1, 1, d), lambda b, i, t: (b, 0, 0)),
            ],
            out_specs=pl.BlockSpec((tm, d), lambda b, i, t: (b * nt + i, 0)),
            scratch_shapes=[pltpu.VMEM((2, e * win, d), BF16), pltpu.VMEM((win, d), BF16),
                            pltpu.VMEM((tm, d), F32), pltpu.SemaphoreType.DMA((2,)),
                            pltpu.SemaphoreType.DMA(())],
        ),
        out_shape=jax.ShapeDtypeStruct((m, d), F32),
        input_output_aliases={3: 0},
        compiler_params=_cparams(("arbitrary", "arbitrary"), est),
        name="moe_combine",
    )(tbl, posm_t, y, x2, gate)


def _sigmoid(z):
    return jax.nn.sigmoid(z)


def _in_proj(h2, w_in, layer, n_cols):
    return _mm([(h2, w_in, layer, 0)], [], lambda accs, ex: accs[0], n_cols, F32, 1024,
               math.gcd(512, n_cols), name="in_proj")


def _mixer_out(x2, a2, px, g_mix, wts, layer, dfts, batch, rpb):
    w_glu, w_s5o, w_fto, w_out = wts
    wc, cl, sl = dfts
    d = x2.shape[1]
    d_s5 = a2.shape[1]
    d_ft = w_fto.shape[1]
    xc, xs = _chan_dft(px, d_s5, d_ft, wc)
    yf = _seq_dft(xc, xs, cl, sl, batch)
    glu = _mm([(a2, w_glu, layer, 0)], [("tile", a2, 0)],
              lambda accs, ex: ex[0] * _sigmoid(accs[0]),
              d_s5, BF16, 1024, 512, name="glu")
    tn = math.gcd(512, d_s5 + d_ft, d)
    g0 = (d_s5 + d_ft) // tn
    merged = _mm([(glu, w_s5o, layer, 0), (yf, w_fto, layer, 0)],
                 [("tile", px, g0), ("tile", px, g0 + d // tn)],
                 lambda accs, ex: _sigmoid(ex[0]) * accs[0] + _sigmoid(ex[1]) * accs[1],
                 d, BF16, 1024, tn, name="merge")
    return _mm([(merged, w_out, layer, 0)], [("tile", x2, 0), ("row", g_mix, 0)],
               lambda accs, ex: ex[0] + ex[1] * accs[0],
               d, F32, 1024, tn, rows_per_batch=rpb, alias_extra=0, name="out_proj")


def _moe(streams, w_gate, w_up, w_down, layer):
    e = streams[0][6].shape[0]
    caps = [OP["capacity_factor"] * (s[0].shape[0] // s[1]) // e for s in streams]
    offs, n_rows = [], 0
    for s, cap in zip(streams, caps):
        assert n_rows % cap == 0
        offs.append(n_rows)
        n_rows += s[1] * cap
    xg = gslot = None
    if len(streams) > 1:
        xg = jnp.zeros((e, n_rows, streams[0][0].shape[1]), BF16)
        gslot = jnp.zeros((e, n_rows, 1), F32)
    routed = []
    for (x2, batch, g, shift, scale, _, lg_t), cap, off in zip(streams, caps, offs):
        tile = min(COMBINE_TILE, x2.shape[0] // batch)
        posm, aff, bounds = _route(lg_t, batch, cap, tile)
        idx, gslot = _slots(posm, aff, cap, n_rows, off, gslot)
        xg = _gather(idx, x2, g, shift, scale, cap, n_rows, off, xg)
        routed.append((posm, bounds))
    y = _expert_down(_expert_up(xg, w_gate, w_up, layer), w_down, gslot, layer)
    outs = []
    for (x2, batch, _, _, _, gate, _), cap, off, (posm, bounds) in zip(streams, caps, offs, routed):
        posm_t = posm.transpose(0, 2, 1).reshape(x2.shape[0], e)
        outs.append(_combine(posm_t, bounds, y, x2, gate, batch, cap, off))
    return outs


def _grid_posembed(n_tok, d):
    gw = OP["grid_w"]
    rows = n_tok // gw
    quarter = d // 4
    inv_freq = OP["pos_base"] ** (-jnp.arange(quarter, dtype=F32) / quarter)
    ang_r = jnp.arange(rows, dtype=F32)[:, None] * inv_freq
    ang_c = jnp.arange(gw, dtype=F32)[:, None] * inv_freq
    shape = (rows, gw, quarter)
    parts = [jnp.broadcast_to(jnp.sin(ang_r)[:, None, :], shape),
             jnp.broadcast_to(jnp.cos(ang_r)[:, None, :], shape),
             jnp.broadcast_to(jnp.sin(ang_c)[None, :, :], shape),
             jnp.broadcast_to(jnp.cos(ang_c)[None, :, :], shape)]
    return jnp.concatenate(parts, axis=-1).reshape(n_tok, d)


def kernel(x, c, ctx, c_ctx, ada_w, ada_b, norm_mix_g, norm_ffn_g, w_in, s5_lam_re, s5_lam_im,
           s5_log_dt, s5_b_re, s5_b_im, s5_c_re, s5_c_im, s5_d, w_glu, w_s5_out, w_ft_out, w_out,
           w_router, w_gate, w_up, w_down, norm_final_g):
    batch, seq, d = x.shape
    ctx_len = ctx.shape[1]
    depth = ada_w.shape[0]
    d_s5 = w_glu.shape[1]
    d_ft = w_ft_out.shape[1]
    n_mod = OP["n_mod"]
    m_x, m_c = batch * seq, batch * ctx_len

    x2 = x.reshape(m_x, d)
    pos = _grid_posembed(seq, d).astype(x.dtype)
    c2 = ctx.reshape(m_c, d)

    rows = -(-(batch + 1) // 8) * 8
    c8 = jnp.zeros((rows, d), F32).at[:batch].set(c).at[batch].set(c_ctx)
    mod = _adaln(c8, ada_w, ada_b)

    gd = d_ft // OP["ft_groups"]
    wc_c, wc_s = _dft_mats(gd)
    wc = jnp.concatenate([wc_c, wc_s], axis=1)
    dft_x = (wc,) + _dft_mats(seq)
    dft_c = (wc,) + _dft_mats(ctx_len)

    sw = 4 * S5_TILE_GROUPS * OP["s5_state"]
    h_zero = jnp.zeros((d_s5 // V7X_LANES, batch, 1, sw), F32)
    d_in = w_in.shape[2]
    mix_w = (w_glu, w_s5_out, w_ft_out, w_out)

    for i in range(depth):
        last = i == depth - 1
        mx = [mod[i, :batch, k * d:(k + 1) * d].reshape(batch, 1, d) for k in range(n_mod)]
        mc = [jnp.broadcast_to(mod[i, batch:batch + 1, k * d:(k + 1) * d].reshape(1, 1, d),
                               (batch, 1, d)) for k in range(n_mod)]
        wr_t = w_router[i].T.astype(BF16)
        ops = _s5_operators(s5_lam_re[i], s5_lam_im[i], s5_log_dt[i], s5_b_re[i], s5_b_im[i],
                            s5_c_re[i], s5_c_im[i], s5_d[i], S5_CHUNK)

        hc = _norm(c2, norm_mix_g[i], mc[0], mc[1], rows_per_batch=ctx_len)
        pc = _in_proj(hc, w_in, i, d_s5 if last else d_in)
        ac, h_ctx = _s5(pc, d_s5, batch, ops, h_zero)

        if i == 0:
            hx, x2 = _norm(x2, norm_mix_g[i], mx[0], mx[1], rows_per_batch=seq, pos=pos)
        else:
            hx = _norm(x2, norm_mix_g[i], mx[0], mx[1], rows_per_batch=seq)
        px = _in_proj(hx, w_in, i, d_in)
        ax, _ = _s5(px, d_s5, batch, ops, h_ctx)
        x2 = _mixer_out(x2, ax, px, mx[2], mix_w, i, dft_x, batch, seq)
        lgx = _norm(x2, norm_ffn_g[i], mx[3], mx[4], rows_per_batch=seq, wr_t=wr_t, emit_h=False)
        streams = [(x2, batch, norm_ffn_g[i], mx[3], mx[4], mx[5], lgx)]
        if not last:
            c2 = _mixer_out(c2, ac, pc, mc[2], mix_w, i, dft_c, batch, ctx_len)
            lgc = _norm(c2, norm_ffn_g[i], mc[3], mc[4], rows_per_batch=ctx_len, wr_t=wr_t,
                        emit_h=False)
            streams.append((c2, batch, norm_ffn_g[i], mc[3], mc[4], mc[5], lgc))
        outs = _moe(streams, w_gate, w_up, w_down, i)
        x2 = outs[0]
        if not last:
            c2 = outs[1]

    out = _norm(x2, norm_final_g, out_dtype=x.dtype)
    return out.reshape(batch, seq, d)
```

```python
import functools
import math

import jax
import jax.numpy as jnp
from jax import lax
from jax.experimental import pallas as pl
from jax.experimental.pallas import tpu as pltpu

BF16 = jnp.bfloat16
F32 = jnp.float32
HI = lax.Precision.HIGHEST

OP = dict(
    s5_group_dim=16,
    s5_state=64,
    ft_groups=4,
    capacity_factor=2,
    n_mod=6,
    rms_eps=1e-6,
    pos_base=10000.0,
    grid_w=64,
    lambda_re_max=-1e-4,
)

V7X_VMEM_BYTES = 64 * 1024 * 1024
V7X_LANES = 128
S5_TILE_GROUPS = V7X_LANES // 16
S5_CHUNK = 16
CUMSUM_BLOCK = 256


def _cparams(sem, vmem_est):
    limit = int(min(max(vmem_est * 5 // 4 + (4 << 20), 32 << 20), V7X_VMEM_BYTES - (6 << 20)))
    return pltpu.CompilerParams(dimension_semantics=sem, vmem_limit_bytes=limit)


def _nbytes(shape, dtype):
    return math.prod(shape) * jnp.dtype(dtype).itemsize


def _adaln_kernel(c_ref, w_ref, b_ref, o_ref):
    cv = c_ref[...]
    a = (cv * jax.nn.sigmoid(cv)).astype(BF16)
    o_ref[0] = jnp.dot(a, w_ref[0].astype(BF16), preferred_element_type=F32) + b_ref[0]


def _adaln(c8, ada_w, ada_b, tn=512):
    depth, d, n6 = ada_w.shape
    rows = c8.shape[0]
    tn = min(tn, n6)
    est = 2 * _nbytes((d, tn), F32) + 4 * _nbytes((rows, tn), F32) + _nbytes((rows, d), F32) * 2
    return pl.pallas_call(
        _adaln_kernel,
        grid=(depth, n6 // tn),
        in_specs=[
            pl.BlockSpec((rows, d), lambda l, j: (0, 0)),
            pl.BlockSpec((1, d, tn), lambda l, j: (l, 0, j)),
            pl.BlockSpec((1, 1, tn), lambda l, j: (l, 0, j)),
        ],
        out_specs=pl.BlockSpec((1, rows, tn), lambda l, j: (l, 0, j)),
        out_shape=jax.ShapeDtypeStruct((depth, rows, n6), F32),
        compiler_params=_cparams(("parallel", "parallel"), est),
        name="adaln",
    )(c8, ada_w, ada_b.reshape(depth, 1, n6))


def _rms_modulate(xf, g, shift, scale, eps):
    ms = jnp.mean(xf * xf, axis=-1, keepdims=True)
    y = xf * lax.rsqrt(ms + eps) * g
    if shift is not None:
        y = y * (1.0 + scale) + shift
    return y


def _norm_kernel(*refs, eps, modulate, router, add_pos, emit_h):
    x_ref, g_ref = refs[0], refs[1]
    k = 2
    xf = x_ref[...]
    if add_pos:
        xf = xf + refs[k][...]
        k += 1
        refs[-1][...] = xf
    shift = scale = None
    if modulate:
        shift, scale = refs[k][0], refs[k + 1][0]
        k += 2
    y = _rms_modulate(xf, g_ref[...], shift, scale, eps)
    if router:
        wr_ref = refs[k]
        k += 1
    if emit_h:
        refs[k][...] = y.astype(refs[k].dtype)
        k += 1
    if router:
        refs[k][...] = lax.dot_general(
            wr_ref[...], y.astype(BF16), (((1,), (1,)), ((), ())), preferred_element_type=F32)


def _norm(x2, g, shift=None, scale=None, rows_per_batch=None, wr_t=None, out_dtype=BF16, tm=256,
          pos=None, emit_h=True):
    m, d = x2.shape
    modulate = shift is not None
    router = wr_t is not None
    add_pos = pos is not None
    rpb = rows_per_batch if rows_per_batch is not None else m
    tm = min(tm, rpb)
    tpb = rpb // tm
    in_specs = [pl.BlockSpec((tm, d), lambda i: (i, 0)), pl.BlockSpec((1, d), lambda i: (0, 0))]
    args = [x2, g.reshape(1, d)]
    if add_pos:
        in_specs.append(pl.BlockSpec((tm, d), lambda i: (i % tpb, 0)))
        args.append(pos)
    if modulate:
        in_specs += [pl.BlockSpec((1, 1, d), lambda i: (i // tpb, 0, 0))] * 2
        args += [shift, scale]
    out_specs, out_shape = [], []
    if emit_h:
        out_specs.append(pl.BlockSpec((tm, d), lambda i: (i, 0)))
        out_shape.append(jax.ShapeDtypeStruct((m, d), out_dtype))
    if router:
        e = wr_t.shape[0]
        in_specs.append(pl.BlockSpec((e, d), lambda i: (0, 0)))
        args.append(wr_t)
        out_specs.append(pl.BlockSpec((e, tm), lambda i: (0, i)))
        out_shape.append(jax.ShapeDtypeStruct((e, m), F32))
    est = 2 * _nbytes((tm, d), F32) + 2 * _nbytes((tm, d), out_dtype) + 3 * _nbytes((tm, d), F32)
    if add_pos:
        out_specs.append(pl.BlockSpec((tm, d), lambda i: (i, 0)))
        out_shape.append(jax.ShapeDtypeStruct((m, d), F32))
        est += 4 * _nbytes((tm, d), F32)
    res = pl.pallas_call(
        functools.partial(_norm_kernel, eps=OP["rms_eps"], modulate=modulate, router=router,
                          add_pos=add_pos, emit_h=emit_h),
        grid=(m // tm,),
        in_specs=in_specs,
        out_specs=out_specs,
        out_shape=out_shape,
        compiler_params=_cparams(("parallel",), est),
        name="rmsnorm",
    )(*args)
    return res if len(res) > 1 else res[0]


def _mm_kernel(*refs, n_pairs, kinds, epilogue):
    n_in = 2 * n_pairs + len(kinds)
    o_ref = refs[n_in]
    w_bf = refs[n_in + 1:]

    @pl.when(pl.program_id(1) == 0)
    def _():
        for k in range(n_pairs):
            w_bf[k][...] = refs[2 * k + 1][...].astype(BF16)

    accs = [jnp.dot(refs[2 * k][...].astype(BF16), w_bf[k][...], preferred_element_type=F32)
            for k in range(n_pairs)]
    ex = []
    for k, kind in enumerate(kinds):
        r = refs[2 * n_pairs + k]
        ex.append(r[0] if kind == "row" else r[...])
    o_ref[...] = epilogue(accs, ex).astype(o_ref.dtype)


def _mm(pairs, extras, epilogue, n_out, out_dtype, tm, tn, rows_per_batch=None, alias_extra=None,
        name="mm"):
    m = pairs[0][0].shape[0]
    rpb = rows_per_batch if rows_per_batch is not None else m
    tm = min(tm, rpb)
    tn = min(tn, n_out)
    tpb = rpb // tm
    in_specs, args, scratch = [], [], []
    est = 2 * _nbytes((tm, tn), out_dtype) + 2 * _nbytes((tm, tn), F32) * max(1, len(pairs))
    for a, w, layer, off in pairs:
        kdim = a.shape[1]
        in_specs.append(pl.BlockSpec((tm, kdim), lambda j, i: (i, 0)))
        in_specs.append(pl.BlockSpec((None, kdim, tn),
                                     lambda j, i, off=off, layer=layer: (layer, 0, j + off)))
        scratch.append(pltpu.VMEM((kdim, tn), BF16))
        args += [a, w]
        est += (2 * _nbytes((tm, kdim), a.dtype) + 2 * _nbytes((kdim, tn), w.dtype)
                + _nbytes((kdim, tn), BF16))
        if a.dtype != BF16:
            est += _nbytes((tm, kdim), BF16)
    kinds = []
    for kind, arr, off in extras:
        kinds.append(kind)
        if kind == "tile":
            in_specs.append(pl.BlockSpec((tm, tn), lambda j, i, off=off: (i, j + off)))
            est += 2 * _nbytes((tm, tn), arr.dtype)
        else:
            in_specs.append(pl.BlockSpec((1, 1, tn), lambda j, i, off=off: (i // tpb, 0, j + off)))
        args.append(arr)
    aliases = {}
    if alias_extra is not None:
        aliases = {2 * len(pairs) + alias_extra: 0}
    return pl.pallas_call(
        functools.partial(_mm_kernel, n_pairs=len(pairs), kinds=tuple(kinds), epilogue=epilogue),
        grid=(n_out // tn, m // tm),
        in_specs=in_specs,
        out_specs=pl.BlockSpec((tm, tn), lambda j, i: (i, j)),
        out_shape=jax.ShapeDtypeStruct((m, n_out), out_dtype),
        scratch_shapes=scratch,
        input_output_aliases=aliases,
        compiler_params=_cparams(("parallel", "arbitrary"), est),
        name=name,
    )(*args)


def _s5_operators(lam_re, lam_im, log_dt, b_re, b_im, c_re, c_im, d_skip, t):
    n = OP["s5_group_dim"]
    lr = jnp.minimum(lam_re.astype(F32), OP["lambda_re_max"])
    li = lam_im.astype(F32)
    dt = jnp.exp(log_dt.astype(F32))[..., None]
    g, p = lr.shape[1], lr.shape[2]
    gt = S5_TILE_GROUPS
    j = g // gt
    mag = jnp.exp(lr * dt)
    a_re = mag * jnp.cos(li * dt)
    a_im = mag * jnp.sin(li * dt)
    num_re, num_im = a_re - 1.0, a_im
    den = lr * lr + li * li
    f_re = (num_re * lr + num_im * li) / den
    f_im = (num_im * lr - num_re * li) / den
    br, bi = b_re.astype(F32), b_im.astype(F32)
    bb_re = f_re[..., None] * br - f_im[..., None] * bi
    bb_im = f_re[..., None] * bi + f_im[..., None] * br
    cr, ci = c_re.astype(F32), c_im.astype(F32)

    k = jnp.arange(t + 1, dtype=F32)[:, None, None, None]
    pw_mag = jnp.exp(lr[None] * dt[None] * k)
    pw_re = pw_mag * jnp.cos(li[None] * dt[None] * k)
    pw_im = pw_mag * jnp.sin(li[None] * dt[None] * k)

    ca_re = cr[None] * pw_re[:t, :, :, None, :] - ci[None] * pw_im[:t, :, :, None, :]
    ca_im = cr[None] * pw_im[:t, :, :, None, :] + ci[None] * pw_re[:t, :, :, None, :]
    bt_re, bt_im = bb_re.transpose(0, 1, 3, 2), bb_im.transpose(0, 1, 3, 2)
    kk = jnp.sum(ca_re[:, :, :, :, None, :] * bt_re[None, :, :, None, :, :]
                 - ca_im[:, :, :, :, None, :] * bt_im[None, :, :, None, :, :], axis=-1)
    kf, kb = kk[:, 0], kk[:, 1]
    k0 = kf[0] + kb[0] + d_skip.astype(F32).reshape(g, n)[:, :, None] * jnp.eye(n, dtype=F32)
    kall = jnp.concatenate([kb[1:][::-1], k0[None], kf[1:]], axis=0)
    kc = kall.reshape(2 * t - 1, j, gt, n, n).transpose(1, 0, 4, 2, 3)
    kc = kc.reshape(j, 2 * t - 1, n, gt * n).astype(BF16)
    lane = jnp.arange(gt * n)
    same_group = (lane[:, None] // n) == (lane[None, :] // n)
    bd = jnp.where(same_group, jnp.tile(kc, (1, 1, gt, 1)), 0)

    ps_re = jnp.stack([pw_re[:t][::-1, 0], pw_re[:t, 1]], axis=0)
    ps_im = jnp.stack([pw_im[:t][::-1, 0], pw_im[:t, 1]], axis=0)
    w_re = ps_re[..., None] * bb_re[:, None] - ps_im[..., None] * bb_im[:, None]
    w_im = ps_re[..., None] * bb_im[:, None] + ps_im[..., None] * bb_re[:, None]
    w6 = jnp.stack([w_re, w_im], axis=1).reshape(2, 2, t, j, gt, p, n)
    wc = w6.transpose(3, 2, 6, 0, 1, 4, 5).reshape(j, t, n, 4 * gt * p).astype(BF16)
    col_group = (jnp.arange(4 * gt * p) // p) % gt
    mask_w = ((lane[:, None] // n) == col_group[None, :]).astype(BF16)

    po_re = jnp.stack([pw_re[1:, 0], pw_re[1:][::-1, 1]], axis=0)
    po_im = jnp.stack([pw_im[1:, 0], pw_im[1:][::-1, 1]], axis=0)
    co_re = cr[:, None] * po_re[:, :, :, None, :] - ci[:, None] * po_im[:, :, :, None, :]
    co_im = cr[:, None] * po_im[:, :, :, None, :] + ci[:, None] * po_re[:, :, :, None, :]
    o6 = jnp.stack([co_re, -co_im], axis=1).reshape(2, 2, t, j, gt, n, p)
    oc = o6.transpose(3, 0, 1, 6, 2, 4, 5).reshape(j, 4, p, t * gt * n).astype(BF16)
    out_group = (jnp.arange(t * gt * n) // n) % gt
    mask_o = ((jnp.arange(gt * p)[:, None] // p) == out_group[None, :]).astype(BF16)

    at = jnp.stack([pw_re[t, 0], pw_im[t, 0], pw_re[t, 1], pw_im[t, 1]], axis=0)
    at = at.reshape(4, j, gt * p).transpose(1, 0, 2).reshape(j, 1, 4 * gt * p)
    return bd, wc, oc, mask_w, mask_o, at


def _gelu_tanh(y):
    return 0.5 * y * (1.0 + jnp.tanh(0.7978845608028654 * (y + 0.044715 * y * y * y)))


def _s5_kernel(u_ref, bd_ref, wc_ref, oc_ref, mw_ref, mo_ref, at_ref, h0_ref, o_ref, hfin_ref,
               mt_scr, wst_scr, wout_scr, s_scr, hin_scr, *, n_chunks, half, t):
    lanes = u_ref.shape[1]

    @pl.when(pl.program_id(1) == 0)
    def _():
        for s in range(t):
            for k in range(t):
                mt_scr[s * lanes:(s + 1) * lanes, k * lanes:(k + 1) * lanes] = bd_ref[0, k - s + t - 1]
            rep = lanes // wc_ref.shape[2]
            wst_scr[s * lanes:(s + 1) * lanes, :] = jnp.tile(wc_ref[0, s], (rep, 1)) * mw_ref[...]
        rows = mo_ref.shape[0]
        for q in range(oc_ref.shape[1]):
            rep = rows // oc_ref.shape[2]
            wout_scr[q * rows:(q + 1) * rows, :] = jnp.tile(oc_ref[0, q], (rep, 1)) * mo_ref[...]

    u = jnp.concatenate([u_ref[pl.ds(s, n_chunks, stride=t), :] for s in range(t)],
                        axis=1).astype(BF16)
    s_scr[...] = jnp.dot(u, wst_scr[...], preferred_element_type=F32)
    at = at_ref[0]
    afr, afi = at[:, 0:half], at[:, half:2 * half]
    abr, abi = at[:, 2 * half:3 * half], at[:, 3 * half:4 * half]
    h0 = h0_ref[0, 0]
    init = (h0[:, 0:half], h0[:, half:2 * half], h0[:, 2 * half:3 * half], h0[:, 3 * half:4 * half])

    def body(i, carry):
        hfr, hfi, hbr, hbi = carry
        cb = n_chunks - 1 - i
        hin_scr[pl.ds(i, 1), 0:half] = hfr
        hin_scr[pl.ds(i, 1), half:2 * half] = hfi
        hin_scr[pl.ds(cb, 1), 2 * half:3 * half] = hbr
        hin_scr[pl.ds(cb, 1), 3 * half:4 * half] = hbi
        sf = s_scr[pl.ds(i, 1), 0:2 * half]
        sb = s_scr[pl.ds(cb, 1), 2 * half:4 * half]
        nfr = afr * hfr - afi * hfi + sf[:, 0:half]
        nfi = afr * hfi + afi * hfr + sf[:, half:2 * half]
        nbr = abr * hbr - abi * hbi + sb[:, 0:half]
        nbi = abr * hbi + abi * hbr + sb[:, half:2 * half]
        return nfr, nfi, nbr, nbi

    hfr, hfi, hbr, hbi = lax.fori_loop(0, n_chunks, body, init)
    hfin_ref[0, 0] = jnp.concatenate([hfr, hfi, hbr, hbi], axis=1)
    y = (jnp.dot(u, mt_scr[...], preferred_element_type=F32)
         + jnp.dot(hin_scr[...].astype(BF16), wout_scr[...], preferred_element_type=F32))
    for k in range(t):
        o_ref[pl.ds(k, n_chunks, stride=t), :] = _gelu_tanh(
            y[:, k * lanes:(k + 1) * lanes]).astype(o_ref.dtype)


def _s5(u2, ds5, batch, ops, h0):
    bd, wc, oc, mask_w, mask_o, at = ops
    m = u2.shape[0]
    l = m // batch
    t = S5_CHUNK
    c = l // t
    j = ds5 // V7X_LANES
    w = t * V7X_LANES
    sw = wc.shape[3]
    compact = sum(_nbytes(a.shape[1:], BF16) for a in (bd, wc, oc)) + _nbytes(mask_w.shape, BF16) \
        + _nbytes(mask_o.shape, BF16)
    est = (3 * _nbytes((w, sw), BF16) + 2 * compact + 4 * _nbytes((l, V7X_LANES), F32)
           + 2 * _nbytes((c, sw), F32) + 3 * _nbytes((c, w), F32) + _nbytes((c, sw), BF16)
           + 2 * _nbytes((c, w), BF16))
    return pl.pallas_call(
        functools.partial(_s5_kernel, n_chunks=c, half=sw // 4, t=t),
        grid=(j, batch),
        in_specs=[
            pl.BlockSpec((l, V7X_LANES), lambda jj, b: (b, jj)),
            pl.BlockSpec((1,) + bd.shape[1:], lambda jj, b: (jj, 0, 0, 0)),
            pl.BlockSpec((1,) + wc.shape[1:], lambda jj, b: (jj, 0, 0, 0)),
            pl.BlockSpec((1,) + oc.shape[1:], lambda jj, b: (jj, 0, 0, 0)),
            pl.BlockSpec(mask_w.shape, lambda jj, b: (0, 0)),
            pl.BlockSpec(mask_o.shape, lambda jj, b: (0, 0)),
            pl.BlockSpec((1, 1, sw), lambda jj, b: (jj, 0, 0)),
            pl.BlockSpec((1, 1, 1, sw), lambda jj, b: (jj, b, 0, 0)),
        ],
        out_specs=[
            pl.BlockSpec((l, V7X_LANES), lambda jj, b: (b, jj)),
            pl.BlockSpec((1, 1, 1, sw), lambda jj, b: (jj, b, 0, 0)),
        ],
        out_shape=[
            jax.ShapeDtypeStruct((m, ds5), F32),
            jax.ShapeDtypeStruct((j, batch, 1, sw), F32),
        ],
        scratch_shapes=[pltpu.VMEM((w, w), BF16), pltpu.VMEM((w, sw), BF16), pltpu.VMEM((sw, w), BF16),
                        pltpu.VMEM((c, sw), F32), pltpu.VMEM((c, sw), F32)],
        compiler_params=_cparams(("arbitrary", "arbitrary"), est),
        name="s5",
    )(u2, bd, wc, oc, mask_w, mask_o, at, h0)


def _dft_mats(n):
    hi = n // V7X_LANES
    j = jnp.arange(n, dtype=jnp.int32)[:, None]
    a1 = ((j * jnp.arange(V7X_LANES, dtype=jnp.int32)[None, :]) % n).astype(F32) * (2.0 * math.pi / n)
    a2 = ((j * jnp.arange(hi, dtype=jnp.int32)[None, :]) % hi).astype(F32) * (2.0 * math.pi / hi)
    c1, s1 = jnp.cos(a1)[:, None, :], jnp.sin(a1)[:, None, :]
    c2, s2 = jnp.cos(a2)[:, :, None], jnp.sin(a2)[:, :, None]
    s = 1.0 / math.sqrt(n)
    cos = ((c2 * c1 - s2 * s1) * s).astype(BF16).reshape(n, n)
    sin = ((s2 * c1 + c2 * s1) * s).astype(BF16).reshape(n, n)
    return cos, sin


def _chan_dft_kernel(v_ref, w_ref, xc_ref, xs_ref, *, gd):
    r = jnp.dot(v_ref[...].astype(BF16), w_ref[...], preferred_element_type=F32)
    xc_ref[...] = r[:, :gd].astype(xc_ref.dtype)
    xs_ref[...] = r[:, gd:].astype(xs_ref.dtype)


def _chan_dft(v2, col_off, dft, wc, tm=1024):
    m = v2.shape[0]
    gd = wc.shape[0]
    tm = min(tm, m)
    goff = col_off // gd
    spec = pl.BlockSpec((tm, gd), lambda i, g: (i, g))
    est = (2 * _nbytes((tm, gd), F32) + 5 * _nbytes((tm, gd), BF16) + 2 * _nbytes((gd, 2 * gd), BF16)
           + 2 * _nbytes((tm, 2 * gd), F32))
    return pl.pallas_call(
        functools.partial(_chan_dft_kernel, gd=gd),
        grid=(m // tm, dft // gd),
        in_specs=[pl.BlockSpec((tm, gd), lambda i, g: (i, goff + g)),
                  pl.BlockSpec((gd, 2 * gd), lambda i, g: (0, 0))],
        out_specs=[spec, spec],
        out_shape=[jax.ShapeDtypeStruct((m, dft), BF16)] * 2,
        compiler_params=_cparams(("parallel", "parallel"), est),
        name="chan_dft",
    )(v2, wc)


def _seq_dft_kernel(cl_ref, sl_ref, xc_ref, xs_ref, o_ref):
    y = (jnp.dot(cl_ref[...], xc_ref[0], preferred_element_type=F32)
         - jnp.dot(sl_ref[...], xs_ref[0], preferred_element_type=F32))
    o_ref[0] = y.astype(o_ref.dtype)


def _seq_dft(xc, xs, cl, sl, batch, tm=512, tn=512):
    m, dft = xc.shape
    l = m // batch
    tm = min(tm, l)
    tn = min(tn, dft)
    xc3, xs3 = xc.reshape(batch, l, dft), xs.reshape(batch, l, dft)
    est = 4 * _nbytes((tm, l), BF16) + 4 * _nbytes((l, tn), BF16) + 4 * _nbytes((tm, tn), F32)
    out = pl.pallas_call(
        _seq_dft_kernel,
        grid=(batch, dft // tn, l // tm),
        in_specs=[
            pl.BlockSpec((tm, l), lambda b, jn, i: (i, 0)),
            pl.BlockSpec((tm, l), lambda b, jn, i: (i, 0)),
            pl.BlockSpec((1, l, tn), lambda b, jn, i: (b, 0, jn)),
            pl.BlockSpec((1, l, tn), lambda b, jn, i: (b, 0, jn)),
        ],
        out_specs=pl.BlockSpec((1, tm, tn), lambda b, jn, i: (b, i, jn)),
        out_shape=jax.ShapeDtypeStruct((batch, l, dft), BF16),
        compiler_params=_cparams(("parallel", "parallel", "parallel"), est),
        name="seq_dft",
    )(cl, sl, xc3, xs3)
    return out.reshape(m, dft)


def _prefix_count(mask_f32, tri):
    e, l = mask_f32.shape
    carry = jnp.zeros((e, 1), F32)
    outs = []
    for k in range(l // CUMSUM_BLOCK):
        blk = mask_f32[:, k * CUMSUM_BLOCK:(k + 1) * CUMSUM_BLOCK]
        outs.append(jnp.dot(blk.astype(BF16), tri, preferred_element_type=F32) + carry)
        carry = carry + jnp.sum(blk, axis=1, keepdims=True)
    return outs[0] if len(outs) == 1 else jnp.concatenate(outs, axis=1)


def _route_kernel(lg_ref, posm_ref, aff_ref, bnd_ref, *, cap, tile):
    lg = lg_ref[...]
    mx = jnp.max(lg, axis=0, keepdims=True)
    ex = jnp.exp(lg - mx)
    aff = ex / jnp.sum(ex, axis=0, keepdims=True)
    bits = pltpu.bitcast(aff, jnp.int32)
    e = lg.shape[0]
    v = jnp.zeros((e, 1), jnp.int32)
    for bit in range(30, -1, -1):
        cand = v | (1 << bit)
        cnt = jnp.sum(jnp.where(bits >= cand, 1.0, 0.0), axis=1, keepdims=True)
        v = jnp.where(cnt >= cap, cand, v)
    gt = bits > v
    eq = bits == v
    n_gt = jnp.sum(gt.astype(F32), axis=1, keepdims=True)
    r = lax.broadcasted_iota(jnp.int32, (CUMSUM_BLOCK, CUMSUM_BLOCK), 0)
    c = lax.broadcasted_iota(jnp.int32, (CUMSUM_BLOCK, CUMSUM_BLOCK), 1)
    tri = jnp.where(r < c, 1.0, 0.0).astype(BF16)
    eq_f = jnp.where(eq, 1.0, 0.0)
    tie_ok = _prefix_count(eq_f, tri) < (cap - n_gt)
    sel_f = jnp.where(gt, 1.0, jnp.where(tie_ok, eq_f, 0.0))
    pos = _prefix_count(sel_f, tri)
    posm_ref[0] = jnp.where(sel_f > 0.5, pos, -1.0).astype(jnp.int32)
    aff_ref[0] = aff
    l = lg.shape[1]
    starts = [pos[:, k * tile:k * tile + 1] for k in range(l // tile)]
    starts.append(jnp.full((e, 1), float(cap), F32))
    bnd_ref[0] = jnp.concatenate(starts, axis=1).astype(jnp.int32)


def _route(lg_t, batch, cap, tile):
    e, m = lg_t.shape
    l = m // batch
    nt = l // tile
    est = 16 * _nbytes((e, l), F32)
    return pl.pallas_call(
        functools.partial(_route_kernel, cap=cap, tile=tile),
        grid=(batch,),
        in_specs=[pl.BlockSpec((e, l), lambda b: (0, b))],
        out_specs=[pl.BlockSpec((1, e, l), lambda b: (b, 0, 0))] * 2
        + [pl.BlockSpec((1, e, nt + 1), lambda b: (b, 0, 0))],
        out_shape=[jax.ShapeDtypeStruct((batch, e, l), jnp.int32),
                   jax.ShapeDtypeStruct((batch, e, l), F32),
                   jax.ShapeDtypeStruct((batch, e, nt + 1), jnp.int32)],
        compiler_params=_cparams(("parallel",), est),
        name="route",
    )(lg_t)


def _slots_kernel(posm_ref, aff_ref, *rest, cap):
    idx_ref, gs_ref = rest[-2], rest[-1]
    pm = posm_ref[0, 0]
    l = pm.shape[1]
    slot = lax.broadcasted_iota(jnp.int32, (cap, l), 0)
    tok = lax.broadcasted_iota(jnp.int32, (1, l), 1).astype(F32)
    hit = pm == slot
    idx_ref[0, 0] = jnp.sum(jnp.where(hit, tok, 0.0), axis=1, keepdims=True).astype(jnp.int32)
    gs_ref[0] = jnp.sum(jnp.where(hit, aff_ref[0, 0], 0.0), axis=1, keepdims=True)


def _slots(posm, aff, cap, n_rows, row_off, gs_all=None):
    batch, e, l = posm.shape
    blk = row_off // cap
    in_specs = [pl.BlockSpec((1, 1, 1, l), lambda b, ee: (b, ee, 0, 0))] * 2
    args = [posm.reshape(batch, e, 1, l), aff.reshape(batch, e, 1, l)]
    aliases = {}
    if gs_all is not None:
        in_specs.append(pl.BlockSpec(memory_space=pl.ANY))
        args.append(gs_all)
        aliases = {2: 1}
    est = 6 * _nbytes((cap, l), F32) + 4 * _nbytes((cap, V7X_LANES), F32)
    return pl.pallas_call(
        functools.partial(_slots_kernel, cap=cap),
        grid=(batch, e),
        in_specs=in_specs,
        out_specs=[pl.BlockSpec((1, 1, cap, 1), lambda b, ee: (b, ee, 0, 0)),
                   pl.BlockSpec((1, cap, 1), lambda b, ee: (ee, blk + b, 0))],
        out_shape=[jax.ShapeDtypeStruct((batch, e, cap, 1), jnp.int32),
                   jax.ShapeDtypeStruct((e, n_rows, 1), F32)],
        input_output_aliases=aliases,
        compiler_params=_cparams(("parallel", "parallel"), est),
        name="moe_slots",
    )(*args)


GATHER_ROW_CHUNK = 64


def _gather_kernel(idx_ref, x_hbm, g_ref, sh_ref, sc_ref, *rest, cap, seq, n_exp, n_steps, eps):
    o_ref, rows_scr, sems = rest[-3], rest[-2], rest[-1]
    n = pl.program_id(0) * n_exp + pl.program_id(1)
    cur = n % 2
    nxt_step = jnp.minimum(n + 1, n_steps - 1)
    ch = min(GATHER_ROW_CHUNK, cap)

    def issue_rows(step, buf, lo):
        base = step * cap
        row0 = (step // n_exp) * seq
        for k in range(ch):
            src = x_hbm.at[pl.ds(row0 + idx_ref[base + lo + k], 1)]
            pltpu.make_async_copy(src, rows_scr.at[buf, pl.ds(lo + k, 1)], sems.at[buf]).start()

    def wait_rows(buf):
        pltpu.make_async_copy(x_hbm.at[pl.ds(0, cap)], rows_scr.at[buf], sems.at[buf]).wait()

    @pl.when(n == 0)
    def _():
        def first(c, carry):
            issue_rows(0, 0, c * ch)
            return carry
        lax.fori_loop(0, cap // ch, first, 0)

    wait_rows(cur)
    g, shift, scale = g_ref[...], sh_ref[0], sc_ref[0]

    def body(c, carry):
        r = pl.multiple_of(c * ch, ch)
        issue_rows(nxt_step, 1 - cur, r)
        y = _rms_modulate(rows_scr[cur, pl.ds(r, ch), :], g, shift, scale, eps)
        o_ref[0, pl.ds(r, ch), :] = y.astype(o_ref.dtype)
        return carry

    lax.fori_loop(0, cap // ch, body, 0)

    @pl.when(n == n_steps - 1)
    def _():
        wait_rows(1 - cur)


def _gather(idx, x2, g, shift, scale, cap, n_rows, row_off, xg_all=None):
    batch, e = idx.shape[0], idx.shape[1]
    m, d = x2.shape
    seq = m // batch
    blk = row_off // cap
    in_specs = [
        pl.BlockSpec(memory_space=pl.ANY),
        pl.BlockSpec((1, d), lambda b, ee, ix: (0, 0)),
        pl.BlockSpec((1, 1, d), lambda b, ee, ix: (b, 0, 0)),
        pl.BlockSpec((1, 1, d), lambda b, ee, ix: (b, 0, 0)),
    ]
    args = [idx.reshape(-1), x2, g.reshape(1, d), shift, scale]
    aliases = {}
    if xg_all is not None:
        in_specs.append(pl.BlockSpec(memory_space=pl.ANY))
        args.append(xg_all)
        aliases = {5: 0}
    est = 2 * _nbytes((cap, d), F32) + 2 * _nbytes((cap, d), BF16) + 8 * _nbytes((GATHER_ROW_CHUNK, d), F32)
    return pl.pallas_call(
        functools.partial(_gather_kernel, cap=cap, seq=seq, n_exp=e, n_steps=batch * e,
                          eps=OP["rms_eps"]),
        grid_spec=pltpu.PrefetchScalarGridSpec(
            num_scalar_prefetch=1,
            grid=(batch, e),
            in_specs=in_specs,
            out_specs=pl.BlockSpec((1, cap, d), lambda b, ee, ix: (ee, blk + b, 0)),
            scratch_shapes=[pltpu.VMEM((2, cap, d), F32), pltpu.SemaphoreType.DMA((2,))],
        ),
        out_shape=jax.ShapeDtypeStruct((e, n_rows, d), BF16),
        input_output_aliases=aliases,
        compiler_params=_cparams(("arbitrary", "arbitrary"), est),
        name="moe_gather",
    )(*args)


EXPERT_ROW_BLOCKS = 2


def _up_kernel(xg_ref, wg_ref, wu_ref, o_ref, wg_bf, wu_bf):
    @pl.when(pl.program_id(2) == 0)
    def _():
        wg_bf[...] = wg_ref[0, 0].astype(BF16)
        wu_bf[...] = wu_ref[0, 0].astype(BF16)

    x = xg_ref[0]
    g = jnp.dot(x, wg_bf[...], preferred_element_type=F32)
    u = jnp.dot(x, wu_bf[...], preferred_element_type=F32)
    o_ref[0] = (g * jax.nn.sigmoid(g) * u).astype(o_ref.dtype)


def _expert_up(xg, w_gate, w_up, layer, tf=256):
    e, r, d = xg.shape
    f = w_gate.shape[3]
    tf = min(tf, f)
    tr = r // EXPERT_ROW_BLOCKS
    est = (4 * _nbytes((d, tf), F32) + 3 * _nbytes((d, tf), BF16) + 2 * _nbytes((tr, d), BF16)
           + 6 * _nbytes((tr, tf), F32))
    return pl.pallas_call(
        _up_kernel,
        grid=(e, f // tf, EXPERT_ROW_BLOCKS),
        in_specs=[
            pl.BlockSpec((1, tr, d), lambda ee, fj, rr: (ee, rr, 0)),
            pl.BlockSpec((1, 1, d, tf), lambda ee, fj, rr: (layer, ee, 0, fj)),
            pl.BlockSpec((1, 1, d, tf), lambda ee, fj, rr: (layer, ee, 0, fj)),
        ],
        out_specs=pl.BlockSpec((1, tr, tf), lambda ee, fj, rr: (ee, rr, fj)),
        out_shape=jax.ShapeDtypeStruct((e, r, f), BF16),
        scratch_shapes=[pltpu.VMEM((d, tf), BF16), pltpu.VMEM((d, tf), BF16)],
        compiler_params=_cparams(("parallel", "parallel", "arbitrary"), est),
        name="moe_up",
    )(xg, w_gate, w_up)


def _down_kernel(h_ref, wd_ref, gs_ref, o_ref, wd_bf):
    @pl.when(pl.program_id(2) == 0)
    def _():
        wd_bf[...] = wd_ref[0, 0].astype(BF16)

    y = jnp.dot(h_ref[0], wd_bf[...], preferred_element_type=F32)
    o_ref[0] = (y * gs_ref[0]).astype(o_ref.dtype)


def _expert_down(hid, w_down, gslot, layer, td=1024):
    e, r, f = hid.shape
    d = w_down.shape[3]
    td = min(td, d)
    tr = r // EXPERT_ROW_BLOCKS
    est = (2 * _nbytes((f, td), F32) + 2 * _nbytes((f, td), BF16) + 2 * _nbytes((tr, f), BF16)
           + 4 * _nbytes((tr, td), F32) + 2 * _nbytes((tr, V7X_LANES), F32))
    return pl.pallas_call(
        _down_kernel,
        grid=(e, d // td, EXPERT_ROW_BLOCKS),
        in_specs=[
            pl.BlockSpec((1, tr, f), lambda ee, dj, rr: (ee, rr, 0)),
            pl.BlockSpec((1, 1, f, td), lambda ee, dj, rr: (layer, ee, 0, dj)),
            pl.BlockSpec((1, tr, 1), lambda ee, dj, rr: (ee, rr, 0)),
        ],
        out_specs=pl.BlockSpec((1, tr, td), lambda ee, dj, rr: (ee, rr, dj)),
        out_shape=jax.ShapeDtypeStruct((e, r, d), BF16),
        scratch_shapes=[pltpu.VMEM((f, td), BF16)],
        compiler_params=_cparams(("parallel", "parallel", "arbitrary"), est),
        name="moe_down",
    )(hid, w_down, gslot)


COMBINE_TILE = 256
DMA_ROW_ALIGN = 16


def combine_window(cap, seq, tile):
    return min(cap, max(DMA_ROW_ALIGN, 2 * tile * cap // seq))


def _combine_kernel(tbl_ref, pt_ref, y_hbm, x_ref, g_ref, *rest, n_exp, cap, win, nt, row_off,
                    n_steps, emit_x, post_norm, post_mod, eps):
    stage, extra, acc_scr, sems, xsem = rest[-5:]
    n_post_in = (1 + 2 * post_mod) if post_norm else 0
    post_in = rest[:n_post_in]
    outs = rest[n_post_in:-5]
    n = pl.program_id(0) * nt + pl.program_id(1)
    cur = n % 2
    nxt_step = jnp.minimum(n + 1, n_steps - 1)

    def window_starts(step):
        b, i = step // nt, step % nt
        t0 = (b * (nt + 1) + i) * n_exp
        out = []
        for ee in range(n_exp):
            s0a = jnp.minimum((tbl_ref[t0 + ee] // DMA_ROW_ALIGN) * DMA_ROW_ALIGN, cap - win)
            out.append(pl.multiple_of(s0a, DMA_ROW_ALIGN))
        return row_off + b * cap, out, t0

    def issue_windows(step, buf):
        row_base, starts, _ = window_starts(step)
        for ee in range(n_exp):
            pltpu.make_async_copy(y_hbm.at[ee, pl.ds(row_base + starts[ee], win), :],
                                  stage.at[buf, pl.ds(ee * win, win), :], sems.at[buf]).start()

    def wait_windows(buf):
        pltpu.make_async_copy(y_hbm.at[0, pl.ds(0, n_exp * win), :], stage.at[buf],
                              sems.at[buf]).wait()

    @pl.when(n == 0)
    def _():
        issue_windows(0, 0)

    issue_windows(nxt_step, 1 - cur)
    row_base, starts, t0 = window_starts(n)
    pt = pt_ref[...]
    lane = lax.broadcasted_iota(jnp.int32, (pt.shape[0], win), 1)
    p = jnp.concatenate(
        [jnp.where(pt[:, ee:ee + 1] - starts[ee] == lane, 1.0, 0.0).astype(BF16)
         for ee in range(n_exp)], axis=1)
    wait_windows(cur)
    acc_scr[...] = jnp.dot(p, stage[cur], preferred_element_type=F32)

    for ee in range(n_exp):
        first = starts[ee] + win
        n_extra = jnp.maximum(0, (tbl_ref[t0 + n_exp + ee] - first + win - 1) // win)

        def extra_window(w, carry, ee=ee, first=first):
            lo = first + w * win
            src0 = pl.multiple_of(jnp.minimum(lo, cap - win), DMA_ROW_ALIGN)
            cp = pltpu.make_async_copy(y_hbm.at[ee, pl.ds(row_base + src0, win), :], extra, xsem)
            cp.start()
            cp.wait()
            col = pt[:, ee:ee + 1]
            hit = jnp.where(col >= lo, col - src0, -1) == lane
            acc_scr[...] += jnp.dot(jnp.where(hit, 1.0, 0.0).astype(BF16), extra[...],
                                    preferred_element_type=F32)
            return carry

        lax.fori_loop(0, n_extra, extra_window, 0)

    xn = x_ref[...] + g_ref[0] * acc_scr[...]
    if emit_x:
        outs[0][...] = xn
    if post_norm:
        shift, scale = (post_in[1][0], post_in[2][0]) if post_mod else (None, None)
        h_ref = outs[-1]
        h_ref[...] = _rms_modulate(xn, post_in[0][...], shift, scale, eps).astype(h_ref.dtype)

    @pl.when(n == n_steps - 1)
    def _():
        wait_windows(1 - cur)


def _combine(posm_t, bounds, y, x2, gate, batch, cap, row_off, post=None, emit_x=True):
    m, d = x2.shape
    l = m // batch
    e = posm_t.shape[1]
    tm = min(COMBINE_TILE, l)
    nt = l // tm
    win = combine_window(cap, l, tm)
    assert cap % DMA_ROW_ALIGN == 0 and win % DMA_ROW_ALIGN == 0 and bounds.shape == (batch, e, nt + 1)
    tbl = bounds.transpose(0, 2, 1).reshape(-1)
    est = (2 * _nbytes((e * win, d), BF16) + _nbytes((win, d), BF16) + 9 * _nbytes((tm, d), F32)
           + 2 * _nbytes((tm, e * win), BF16) + 2 * _nbytes((tm, V7X_LANES), F32))
    row_spec = pl.BlockSpec((tm, d), lambda b, i, t: (b * nt + i, 0))
    vec_spec = pl.BlockSpec((1, 1, d), lambda b, i, t: (b, 0, 0))
    in_specs = [pl.BlockSpec((tm, e), lambda b, i, t: (b * nt + i, 0)),
                pl.BlockSpec(memory_space=pl.ANY), row_spec, vec_spec]
    args = [tbl, posm_t, y, x2, gate]
    out_specs, out_shape = [], []
    if emit_x:
        out_specs.append(row_spec)
        out_shape.append(jax.ShapeDtypeStruct((m, d), F32))
    post_mod = False
    if post is not None:
        p_gain, p_shift, p_scale, p_dtype = post
        post_mod = p_shift is not None
        in_specs.append(pl.BlockSpec((1, d), lambda b, i, t: (0, 0)))
        args.append(p_gain.reshape(1, d))
        if post_mod:
            in_specs += [vec_spec, vec_spec]
            args += [p_shift, p_scale]
        out_specs.append(row_spec)
        out_shape.append(jax.ShapeDtypeStruct((m, d), p_dtype))
        est += 2 * _nbytes((tm, d), p_dtype)
    return pl.pallas_call(
        functools.partial(_combine_kernel, n_exp=e, cap=cap, win=win, nt=nt, row_off=row_off,
                          n_steps=batch * nt, emit_x=emit_x, post_norm=post is not None,
                          post_mod=post_mod, eps=OP["rms_eps"]),
        grid_spec=pltpu.PrefetchScalarGridSpec(
            num_scalar_prefetch=1,
            grid=(batch, nt),
            in_specs=in_specs,
            out_specs=out_specs,
            scratch_shapes=[pltpu.VMEM((2, e * win, d), BF16), pltpu.VMEM((win, d), BF16),
                            pltpu.VMEM((tm, d), F32), pltpu.SemaphoreType.DMA((2,)),
                            pltpu.SemaphoreType.DMA(())],
        ),
        out_shape=out_shape,
        input_output_aliases={3: 0} if emit_x else {},
        compiler_params=_cparams(("arbitrary", "arbitrary"), est),
        name="moe_combine",
    )(*args)


def _sigmoid(z):
    return jax.nn.sigmoid(z)


def _in_proj(h2, w_in, layer, n_cols):
    return _mm([(h2, w_in, layer, 0)], [], lambda accs, ex: accs[0], n_cols, F32, 1024,
               math.gcd(512, n_cols), name="in_proj")


def _mixer_out(x2, a2, px, g_mix, wts, layer, dfts, batch, rpb):
    w_glu, w_s5o, w_fto, w_out = wts
    wc, cl, sl = dfts
    d = x2.shape[1]
    d_s5 = a2.shape[1]
    d_ft = w_fto.shape[1]
    xc, xs = _chan_dft(px, d_s5, d_ft, wc)
    yf = _seq_dft(xc, xs, cl, sl, batch)
    glu = _mm([(a2, w_glu, layer, 0)], [("tile", a2, 0)],
              lambda accs, ex: ex[0] * _sigmoid(accs[0]),
              d_s5, BF16, 1024, 512, name="glu")
    tn = math.gcd(512, d_s5 + d_ft, d)
    g0 = (d_s5 + d_ft) // tn
    merged = _mm([(glu, w_s5o, layer, 0), (yf, w_fto, layer, 0)],
                 [("tile", px, g0), ("tile", px, g0 + d // tn)],
                 lambda accs, ex: _sigmoid(ex[0]) * accs[0] + _sigmoid(ex[1]) * accs[1],
                 d, BF16, 1024, tn, name="merge")
    return _mm([(merged, w_out, layer, 0)], [("tile", x2, 0), ("row", g_mix, 0)],
               lambda accs, ex: ex[0] + ex[1] * accs[0],
               d, F32, 1024, tn, rows_per_batch=rpb, alias_extra=0, name="out_proj")


def _moe(streams, w_gate, w_up, w_down, layer):
    e = streams[0][6].shape[0]
    caps = [OP["capacity_factor"] * (s[0].shape[0] // s[1]) // e for s in streams]
    offs, n_rows = [], 0
    for s, cap in zip(streams, caps):
        assert n_rows % cap == 0
        offs.append(n_rows)
        n_rows += s[1] * cap
    xg = gslot = None
    if len(streams) > 1:
        xg = jnp.zeros((e, n_rows, streams[0][0].shape[1]), BF16)
        gslot = jnp.zeros((e, n_rows, 1), F32)
    routed = []
    for (x2, batch, g, shift, scale, _, lg_t, _, _), cap, off in zip(streams, caps, offs):
        tile = min(COMBINE_TILE, x2.shape[0] // batch)
        posm, aff, bounds = _route(lg_t, batch, cap, tile)
        idx, gslot = _slots(posm, aff, cap, n_rows, off, gslot)
        xg = _gather(idx, x2, g, shift, scale, cap, n_rows, off, xg)
        routed.append((posm, bounds))
    y = _expert_down(_expert_up(xg, w_gate, w_up, layer), w_down, gslot, layer)
    outs = []
    for (x2, batch, _, _, _, gate, _, post, emit_x), cap, off, (posm, bounds) in zip(
            streams, caps, offs, routed):
        posm_t = posm.transpose(0, 2, 1).reshape(x2.shape[0], e)
        outs.append(_combine(posm_t, bounds, y, x2, gate, batch, cap, off, post, emit_x))
    return outs


def _grid_posembed(n_tok, d):
    gw = OP["grid_w"]
    rows = n_tok // gw
    quarter = d // 4
    inv_freq = OP["pos_base"] ** (-jnp.arange(quarter, dtype=F32) / quarter)
    ang_r = jnp.arange(rows, dtype=F32)[:, None] * inv_freq
    ang_c = jnp.arange(gw, dtype=F32)[:, None] * inv_freq
    shape = (rows, gw, quarter)
    parts = [jnp.broadcast_to(jnp.sin(ang_r)[:, None, :], shape),
             jnp.broadcast_to(jnp.cos(ang_r)[:, None, :], shape),
             jnp.broadcast_to(jnp.sin(ang_c)[None, :, :], shape),
             jnp.broadcast_to(jnp.cos(ang_c)[None, :, :], shape)]
    return jnp.concatenate(parts, axis=-1).reshape(n_tok, d)


def kernel(x, c, ctx, c_ctx, ada_w, ada_b, norm_mix_g, norm_ffn_g, w_in, s5_lam_re, s5_lam_im,
           s5_log_dt, s5_b_re, s5_b_im, s5_c_re, s5_c_im, s5_d, w_glu, w_s5_out, w_ft_out, w_out,
           w_router, w_gate, w_up, w_down, norm_final_g):
    batch, seq, d = x.shape
    ctx_len = ctx.shape[1]
    depth = ada_w.shape[0]
    d_s5 = w_glu.shape[1]
    d_ft = w_ft_out.shape[1]
    n_mod = OP["n_mod"]
    m_x, m_c = batch * seq, batch * ctx_len

    x2 = x.reshape(m_x, d)
    pos = _grid_posembed(seq, d).astype(x.dtype)
    c2 = ctx.reshape(m_c, d)

    rows = -(-(batch + 1) // 8) * 8
    c8 = jnp.zeros((rows, d), F32).at[:batch].set(c).at[batch].set(c_ctx)
    mod = _adaln(c8, ada_w, ada_b)

    gd = d_ft // OP["ft_groups"]
    wc_c, wc_s = _dft_mats(gd)
    wc = jnp.concatenate([wc_c, wc_s], axis=1)
    dft_x = (wc,) + _dft_mats(seq)
    dft_c = (wc,) + _dft_mats(ctx_len)

    sw = 4 * S5_TILE_GROUPS * OP["s5_state"]
    h_zero = jnp.zeros((d_s5 // V7X_LANES, batch, 1, sw), F32)
    d_in = w_in.shape[2]
    mix_w = (w_glu, w_s5_out, w_ft_out, w_out)

    mods_x = [[mod[i, :batch, k * d:(k + 1) * d].reshape(batch, 1, d) for k in range(n_mod)]
              for i in range(depth)]
    mods_c = [[jnp.broadcast_to(mod[i, batch:batch + 1, k * d:(k + 1) * d].reshape(1, 1, d),
                                (batch, 1, d)) for k in range(n_mod)] for i in range(depth)]

    hx = hc = None
    for i in range(depth):
        last = i == depth - 1
        mx, mc = mods_x[i], mods_c[i]
        wr_t = w_router[i].T.astype(BF16)
        ops = _s5_operators(s5_lam_re[i], s5_lam_im[i], s5_log_dt[i], s5_b_re[i], s5_b_im[i],
                            s5_c_re[i], s5_c_im[i], s5_d[i], S5_CHUNK)

        if i == 0:
            hc = _norm(c2, norm_mix_g[i], mc[0], mc[1], rows_per_batch=ctx_len)
        pc = _in_proj(hc, w_in, i, d_s5 if last else d_in)
        ac, h_ctx = _s5(pc, d_s5, batch, ops, h_zero)

        if i == 0:
            hx, x2 = _norm(x2, norm_mix_g[i], mx[0], mx[1], rows_per_batch=seq, pos=pos)
        px = _in_proj(hx, w_in, i, d_in)
        ax, _ = _s5(px, d_s5, batch, ops, h_ctx)
        x2 = _mixer_out(x2, ax, px, mx[2], mix_w, i, dft_x, batch, seq)
        lgx = _norm(x2, norm_ffn_g[i], mx[3], mx[4], rows_per_batch=seq, wr_t=wr_t, emit_h=False)
        if last:
            post_x, post_c = (norm_final_g, None, None, x.dtype), None
        else:
            post_x = (norm_mix_g[i + 1], mods_x[i + 1][0], mods_x[i + 1][1], BF16)
            post_c = (norm_mix_g[i + 1], mods_c[i + 1][0], mods_c[i + 1][1], BF16)
        streams = [(x2, batch, norm_ffn_g[i], mx[3], mx[4], mx[5], lgx, post_x, not last)]
        if not last:
            c2 = _mixer_out(c2, ac, pc, mc[2], mix_w, i, dft_c, batch, ctx_len)
            lgc = _norm(c2, norm_ffn_g[i], mc[3], mc[4], rows_per_batch=ctx_len, wr_t=wr_t,
                        emit_h=False)
            streams.append((c2, batch, norm_ffn_g[i], mc[3], mc[4], mc[5], lgc, post_c, True))
        outs = _moe(streams, w_gate, w_up, w_down, i)
        if last:
            return outs[0][0].reshape(batch, seq, d)
        x2, hx = outs[0]
        c2, hc = outs[1]
```

```python
import functools
import math

import jax
import jax.numpy as jnp
from jax import lax
from jax.experimental import pallas as pl
from jax.experimental.pallas import tpu as pltpu

BF16 = jnp.bfloat16
F32 = jnp.float32
HI = lax.Precision.HIGHEST

OP = dict(
    s5_group_dim=16,
    s5_state=64,
    ft_groups=4,
    capacity_factor=2,
    n_mod=6,
    rms_eps=1e-6,
    pos_base=10000.0,
    grid_w=64,
    lambda_re_max=-1e-4,
)

V7X_VMEM_BYTES = 64 * 1024 * 1024
V7X_LANES = 128
S5_TILE_GROUPS = V7X_LANES // 16
S5_CHUNK = 16
CUMSUM_BLOCK = 256


def _cparams(sem, vmem_est):
    limit = int(min(max(vmem_est * 5 // 4 + (4 << 20), 32 << 20), V7X_VMEM_BYTES - (6 << 20)))
    return pltpu.CompilerParams(dimension_semantics=sem, vmem_limit_bytes=limit)


def _nbytes(shape, dtype):
    return math.prod(shape) * jnp.dtype(dtype).itemsize


def _adaln_kernel(c_ref, w_ref, b_ref, o_ref):
    cv = c_ref[...]
    a = (cv * jax.nn.sigmoid(cv)).astype(BF16)
    o_ref[0] = jnp.dot(a, w_ref[0].astype(BF16), preferred_element_type=F32) + b_ref[0]


def _adaln(c8, ada_w, ada_b, tn=512):
    depth, d, n6 = ada_w.shape
    rows = c8.shape[0]
    tn = min(tn, n6)
    est = 2 * _nbytes((d, tn), F32) + 4 * _nbytes((rows, tn), F32) + _nbytes((rows, d), F32) * 2
    return pl.pallas_call(
        _adaln_kernel,
        grid=(depth, n6 // tn),
        in_specs=[
            pl.BlockSpec((rows, d), lambda l, j: (0, 0)),
            pl.BlockSpec((1, d, tn), lambda l, j: (l, 0, j)),
            pl.BlockSpec((1, 1, tn), lambda l, j: (l, 0, j)),
        ],
        out_specs=pl.BlockSpec((1, rows, tn), lambda l, j: (l, 0, j)),
        out_shape=jax.ShapeDtypeStruct((depth, rows, n6), F32),
        compiler_params=_cparams(("parallel", "parallel"), est),
        name="adaln",
    )(c8, ada_w, ada_b.reshape(depth, 1, n6))


def _rms_modulate(xf, g, shift, scale, eps):
    ms = jnp.mean(xf * xf, axis=-1, keepdims=True)
    y = xf * lax.rsqrt(ms + eps) * g
    if shift is not None:
        y = y * (1.0 + scale) + shift
    return y


def _norm_kernel(*refs, eps, modulate, router, add_pos, emit_h):
    x_ref, g_ref = refs[0], refs[1]
    k = 2
    xf = x_ref[...]
    if add_pos:
        xf = xf + refs[k][...]
        k += 1
        refs[-1][...] = xf
    shift = scale = None
    if modulate:
        shift, scale = refs[k][0], refs[k + 1][0]
        k += 2
    y = _rms_modulate(xf, g_ref[...], shift, scale, eps)
    if router:
        wr_ref = refs[k]
        k += 1
    if emit_h:
        refs[k][...] = y.astype(refs[k].dtype)
        k += 1
    if router:
        refs[k][...] = lax.dot_general(
            wr_ref[...], y.astype(BF16), (((1,), (1,)), ((), ())), preferred_element_type=F32)


def _norm(x2, g, shift=None, scale=None, rows_per_batch=None, wr_t=None, out_dtype=BF16, tm=256,
          pos=None, emit_h=True):
    m, d = x2.shape
    modulate = shift is not None
    router = wr_t is not None
    add_pos = pos is not None
    rpb = rows_per_batch if rows_per_batch is not None else m
    tm = min(tm, rpb)
    tpb = rpb // tm
    in_specs = [pl.BlockSpec((tm, d), lambda i: (i, 0)), pl.BlockSpec((1, d), lambda i: (0, 0))]
    args = [x2, g.reshape(1, d)]
    if add_pos:
        in_specs.append(pl.BlockSpec((tm, d), lambda i: (i % tpb, 0)))
        args.append(pos)
    if modulate:
        in_specs += [pl.BlockSpec((1, 1, d), lambda i: (i // tpb, 0, 0))] * 2
        args += [shift, scale]
    out_specs, out_shape = [], []
    if emit_h:
        out_specs.append(pl.BlockSpec((tm, d), lambda i: (i, 0)))
        out_shape.append(jax.ShapeDtypeStruct((m, d), out_dtype))
    if router:
        e = wr_t.shape[0]
        in_specs.append(pl.BlockSpec((e, d), lambda i: (0, 0)))
        args.append(wr_t)
        out_specs.append(pl.BlockSpec((e, tm), lambda i: (0, i)))
        out_shape.append(jax.ShapeDtypeStruct((e, m), F32))
    est = 2 * _nbytes((tm, d), F32) + 2 * _nbytes((tm, d), out_dtype) + 3 * _nbytes((tm, d), F32)
    if add_pos:
        out_specs.append(pl.BlockSpec((tm, d), lambda i: (i, 0)))
        out_shape.append(jax.ShapeDtypeStruct((m, d), F32))
        est += 4 * _nbytes((tm, d), F32)
    res = pl.pallas_call(
        functools.partial(_norm_kernel, eps=OP["rms_eps"], modulate=modulate, router=router,
                          add_pos=add_pos, emit_h=emit_h),
        grid=(m // tm,),
        in_specs=in_specs,
        out_specs=out_specs,
        out_shape=out_shape,
        compiler_params=_cparams(("parallel",), est),
        name="rmsnorm",
    )(*args)
    return res if len(res) > 1 else res[0]


def _mm_kernel(*refs, n_pairs, kinds, epilogue):
    n_in = 2 * n_pairs + len(kinds)
    o_ref = refs[n_in]
    w_bf = refs[n_in + 1:]

    @pl.when(pl.program_id(1) == 0)
    def _():
        for k in range(n_pairs):
            w_bf[k][...] = refs[2 * k + 1][...].astype(BF16)

    accs = [jnp.dot(refs[2 * k][...].astype(BF16), w_bf[k][...], preferred_element_type=F32)
            for k in range(n_pairs)]
    ex = []
    for k, kind in enumerate(kinds):
        r = refs[2 * n_pairs + k]
        ex.append(r[0] if kind == "row" else r[...])
    o_ref[...] = epilogue(accs, ex).astype(o_ref.dtype)


def _mm(pairs, extras, epilogue, n_out, out_dtype, tm, tn, rows_per_batch=None, alias_extra=None,
        name="mm"):
    m = pairs[0][0].shape[0]
    rpb = rows_per_batch if rows_per_batch is not None else m
    tm = min(tm, rpb)
    tn = min(tn, n_out)
    tpb = rpb // tm
    in_specs, args, scratch = [], [], []
    est = 2 * _nbytes((tm, tn), out_dtype) + 2 * _nbytes((tm, tn), F32) * max(1, len(pairs))
    for a, w, layer, off in pairs:
        kdim = a.shape[1]
        in_specs.append(pl.BlockSpec((tm, kdim), lambda j, i: (i, 0)))
        in_specs.append(pl.BlockSpec((None, kdim, tn),
                                     lambda j, i, off=off, layer=layer: (layer, 0, j + off)))
        scratch.append(pltpu.VMEM((kdim, tn), BF16))
        args += [a, w]
        est += (2 * _nbytes((tm, kdim), a.dtype) + 2 * _nbytes((kdim, tn), w.dtype)
                + _nbytes((kdim, tn), BF16))
        if a.dtype != BF16:
            est += _nbytes((tm, kdim), BF16)
    kinds = []
    for kind, arr, off in extras:
        kinds.append(kind)
        if kind == "tile":
            in_specs.append(pl.BlockSpec((tm, tn), lambda j, i, off=off: (i, j + off)))
            est += 2 * _nbytes((tm, tn), arr.dtype)
        else:
            in_specs.append(pl.BlockSpec((1, 1, tn), lambda j, i, off=off: (i // tpb, 0, j + off)))
        args.append(arr)
    aliases = {}
    if alias_extra is not None:
        aliases = {2 * len(pairs) + alias_extra: 0}
    return pl.pallas_call(
        functools.partial(_mm_kernel, n_pairs=len(pairs), kinds=tuple(kinds), epilogue=epilogue),
        grid=(n_out // tn, m // tm),
        in_specs=in_specs,
        out_specs=pl.BlockSpec((tm, tn), lambda j, i: (i, j)),
        out_shape=jax.ShapeDtypeStruct((m, n_out), out_dtype),
        scratch_shapes=scratch,
        input_output_aliases=aliases,
        compiler_params=_cparams(("parallel", "arbitrary"), est),
        name=name,
    )(*args)


def _s5_operators(lam_re, lam_im, log_dt, b_re, b_im, c_re, c_im, d_skip, t):
    n = OP["s5_group_dim"]
    lr = jnp.minimum(lam_re.astype(F32), OP["lambda_re_max"])
    li = lam_im.astype(F32)
    dt = jnp.exp(log_dt.astype(F32))[..., None]
    g, p = lr.shape[1], lr.shape[2]
    gt = S5_TILE_GROUPS
    j = g // gt
    mag = jnp.exp(lr * dt)
    a_re = mag * jnp.cos(li * dt)
    a_im = mag * jnp.sin(li * dt)
    num_re, num_im = a_re - 1.0, a_im
    den = lr * lr + li * li
    f_re = (num_re * lr + num_im * li) / den
    f_im = (num_im * lr - num_re * li) / den
    br, bi = b_re.astype(F32), b_im.astype(F32)
    bb_re = f_re[..., None] * br - f_im[..., None] * bi
    bb_im = f_re[..., None] * bi + f_im[..., None] * br
    cr, ci = c_re.astype(F32), c_im.astype(F32)

    k = jnp.arange(t + 1, dtype=F32)[:, None, None, None]
    pw_mag = jnp.exp(lr[None] * dt[None] * k)
    pw_re = pw_mag * jnp.cos(li[None] * dt[None] * k)
    pw_im = pw_mag * jnp.sin(li[None] * dt[None] * k)

    ca_re = cr[None] * pw_re[:t, :, :, None, :] - ci[None] * pw_im[:t, :, :, None, :]
    ca_im = cr[None] * pw_im[:t, :, :, None, :] + ci[None] * pw_re[:t, :, :, None, :]
    bt_re, bt_im = bb_re.transpose(0, 1, 3, 2), bb_im.transpose(0, 1, 3, 2)
    kk = jnp.sum(ca_re[:, :, :, :, None, :] * bt_re[None, :, :, None, :, :]
                 - ca_im[:, :, :, :, None, :] * bt_im[None, :, :, None, :, :], axis=-1)
    kf, kb = kk[:, 0], kk[:, 1]
    k0 = kf[0] + kb[0] + d_skip.astype(F32).reshape(g, n)[:, :, None] * jnp.eye(n, dtype=F32)
    kall = jnp.concatenate([kb[1:][::-1], k0[None], kf[1:]], axis=0)
    kc = kall.reshape(2 * t - 1, j, gt, n, n).transpose(1, 0, 4, 2, 3)
    kc = kc.reshape(j, 2 * t - 1, n, gt * n).astype(BF16)
    lane = jnp.arange(gt * n)
    same_group = (lane[:, None] // n) == (lane[None, :] // n)
    bd = jnp.where(same_group, jnp.tile(kc, (1, 1, gt, 1)), 0)

    ps_re = jnp.stack([pw_re[:t][::-1, 0], pw_re[:t, 1]], axis=0)
    ps_im = jnp.stack([pw_im[:t][::-1, 0], pw_im[:t, 1]], axis=0)
    w_re = ps_re[..., None] * bb_re[:, None] - ps_im[..., None] * bb_im[:, None]
    w_im = ps_re[..., None] * bb_im[:, None] + ps_im[..., None] * bb_re[:, None]
    w6 = jnp.stack([w_re, w_im], axis=1).reshape(2, 2, t, j, gt, p, n)
    wc = w6.transpose(3, 2, 6, 0, 1, 4, 5).reshape(j, t, n, 4 * gt * p).astype(BF16)

    po_re = jnp.stack([pw_re[1:, 0], pw_re[1:][::-1, 1]], axis=0)
    po_im = jnp.stack([pw_im[1:, 0], pw_im[1:][::-1, 1]], axis=0)
    co_re = cr[:, None] * po_re[:, :, :, None, :] - ci[:, None] * po_im[:, :, :, None, :]
    co_im = cr[:, None] * po_im[:, :, :, None, :] + ci[:, None] * po_re[:, :, :, None, :]
    o6 = jnp.stack([co_re, -co_im], axis=1).reshape(2, 2, t, j, gt, n, p)
    oc = o6.transpose(3, 0, 1, 6, 2, 4, 5).reshape(j, 4, p, t * gt * n).astype(BF16)
    at = jnp.stack([pw_re[t, 0], pw_im[t, 0], pw_re[t, 1], pw_im[t, 1]], axis=0)
    at = at.reshape(4, j, gt * p).transpose(1, 0, 2).reshape(j, 1, 4 * gt * p)
    return bd, wc, oc, at


def _s5_masks(t):
    n, p, gt = OP["s5_group_dim"], OP["s5_state"], S5_TILE_GROUPS
    col_group = (jnp.arange(4 * gt * p) // p) % gt
    mask_w = ((jnp.arange(gt * n)[:, None] // n) == col_group[None, :]).astype(BF16)
    out_group = (jnp.arange(t * gt * n) // n) % gt
    mask_o = ((jnp.arange(gt * p)[:, None] // p) == out_group[None, :]).astype(BF16)
    return mask_w, mask_o


def _gelu_tanh(y):
    return 0.5 * y * (1.0 + jnp.tanh(0.7978845608028654 * (y + 0.044715 * y * y * y)))


def _s5_kernel(uc_ref, ux_ref, bd_ref, wc_ref, oc_ref, mw_ref, mo_ref, at_ref, oc_out, ox_out,
               mt_scr, wst_scr, wout_scr, s_scr, hin_scr, hend_scr, *, c_ctx, c_lat, half, t, batch):
    lanes = ux_ref.shape[1]
    step = pl.program_id(1)

    @pl.when(step == 0)
    def _():
        for s in range(t):
            for k in range(t):
                mt_scr[s * lanes:(s + 1) * lanes, k * lanes:(k + 1) * lanes] = bd_ref[0, k - s + t - 1]
            rep = lanes // wc_ref.shape[2]
            wst_scr[s * lanes:(s + 1) * lanes, :] = jnp.tile(wc_ref[0, s], (rep, 1)) * mw_ref[...]
        rows = mo_ref.shape[0]
        for q in range(oc_ref.shape[1]):
            rep = rows // oc_ref.shape[2]
            wout_scr[q * rows:(q + 1) * rows, :] = jnp.tile(oc_ref[0, q], (rep, 1)) * mo_ref[...]

    at = at_ref[0]
    afr, afi = at[:, 0:half], at[:, half:2 * half]
    abr, abi = at[:, 2 * half:3 * half], at[:, 3 * half:4 * half]

    def run(u_ref, o_ref, n_chunks, h0):
        u = jnp.concatenate([u_ref[pl.ds(s, n_chunks, stride=t), :] for s in range(t)],
                            axis=1).astype(BF16)
        s_scr[0:n_chunks, :] = jnp.dot(u, wst_scr[...], preferred_element_type=F32)
        init = (h0[:, 0:half], h0[:, half:2 * half], h0[:, 2 * half:3 * half],
                h0[:, 3 * half:4 * half])

        def body(i, carry):
            hfr, hfi, hbr, hbi = carry
            cb = n_chunks - 1 - i
            hin_scr[pl.ds(i, 1), 0:half] = hfr
            hin_scr[pl.ds(i, 1), half:2 * half] = hfi
            hin_scr[pl.ds(cb, 1), 2 * half:3 * half] = hbr
            hin_scr[pl.ds(cb, 1), 3 * half:4 * half] = hbi
            sf = s_scr[pl.ds(i, 1), 0:2 * half]
            sb = s_scr[pl.ds(cb, 1), 2 * half:4 * half]
            nfr = afr * hfr - afi * hfi + sf[:, 0:half]
            nfi = afr * hfi + afi * hfr + sf[:, half:2 * half]
            nbr = abr * hbr - abi * hbi + sb[:, 0:half]
            nbi = abr * hbi + abi * hbr + sb[:, half:2 * half]
            return nfr, nfi, nbr, nbi

        end = lax.fori_loop(0, n_chunks, body, init)
        y = (jnp.dot(u, mt_scr[...], preferred_element_type=F32)
             + jnp.dot(hin_scr[0:n_chunks, :].astype(BF16), wout_scr[...],
                       preferred_element_type=F32))
        for k in range(t):
            o_ref[pl.ds(k, n_chunks, stride=t), :] = _gelu_tanh(
                y[:, k * lanes:(k + 1) * lanes]).astype(o_ref.dtype)
        return jnp.concatenate(end, axis=1)

    @pl.when(step < batch)
    def _():
        hend_scr[pl.ds(step, 1), :] = run(uc_ref, oc_out, c_ctx, jnp.zeros((1, 4 * half), F32))

    @pl.when(step >= batch)
    def _():
        run(ux_ref, ox_out, c_lat, hend_scr[pl.ds(step - batch, 1), :])


def _s5(uc2, ux2, ds5, batch, ops, layer):
    bd, wc, oc, mask_w, mask_o, at = ops
    mc, mx = uc2.shape[0], ux2.shape[0]
    lc, lx = mc // batch, mx // batch
    t = S5_CHUNK
    cc, cx = lc // t, lx // t
    j = ds5 // V7X_LANES
    w = t * V7X_LANES
    sw = wc.shape[4]
    compact = sum(_nbytes(a.shape[2:], BF16) for a in (bd, wc, oc)) + _nbytes(mask_w.shape, BF16) \
        + _nbytes(mask_o.shape, BF16)
    est = (3 * _nbytes((w, sw), BF16) + 2 * compact + 4 * _nbytes((lx + lc, V7X_LANES), F32)
           + 2 * _nbytes((cx, sw), F32) + 3 * _nbytes((cx, w), F32) + _nbytes((cx, sw), BF16)
           + 2 * _nbytes((cx, w), BF16))
    last = batch - 1
    ctx_spec = pl.BlockSpec((lc, V7X_LANES), lambda jj, s: (jnp.minimum(s, last), jj))
    lat_spec = pl.BlockSpec((lx, V7X_LANES), lambda jj, s: (jnp.maximum(s - batch, 0), jj))

    def op_spec(a):
        return pl.BlockSpec((None, 1) + a.shape[2:], lambda jj, s: (layer, jj) + (0,) * (a.ndim - 2))

    return pl.pallas_call(
        functools.partial(_s5_kernel, c_ctx=cc, c_lat=cx, half=sw // 4, t=t, batch=batch),
        grid=(j, 2 * batch),
        in_specs=[ctx_spec, lat_spec, op_spec(bd), op_spec(wc), op_spec(oc),
                  pl.BlockSpec(mask_w.shape, lambda jj, s: (0, 0)),
                  pl.BlockSpec(mask_o.shape, lambda jj, s: (0, 0)),
                  op_spec(at)],
        out_specs=[ctx_spec, lat_spec],
        out_shape=[jax.ShapeDtypeStruct((mc, ds5), F32), jax.ShapeDtypeStruct((mx, ds5), F32)],
        scratch_shapes=[pltpu.VMEM((w, w), BF16), pltpu.VMEM((w, sw), BF16), pltpu.VMEM((sw, w), BF16),
                        pltpu.VMEM((cx, sw), F32), pltpu.VMEM((cx, sw), F32),
                        pltpu.VMEM((-(-batch // 8) * 8, sw), F32)],
        compiler_params=_cparams(("arbitrary", "arbitrary"), est),
        name="s5",
    )(uc2, ux2, bd, wc, oc, mask_w, mask_o, at)


def _dft_mats(n):
    hi = n // V7X_LANES
    j = jnp.arange(n, dtype=jnp.int32)[:, None]
    a1 = ((j * jnp.arange(V7X_LANES, dtype=jnp.int32)[None, :]) % n).astype(F32) * (2.0 * math.pi / n)
    a2 = ((j * jnp.arange(hi, dtype=jnp.int32)[None, :]) % hi).astype(F32) * (2.0 * math.pi / hi)
    c1, s1 = jnp.cos(a1)[:, None, :], jnp.sin(a1)[:, None, :]
    c2, s2 = jnp.cos(a2)[:, :, None], jnp.sin(a2)[:, :, None]
    s = 1.0 / math.sqrt(n)
    cos = ((c2 * c1 - s2 * s1) * s).astype(BF16).reshape(n, n)
    sin = ((s2 * c1 + c2 * s1) * s).astype(BF16).reshape(n, n)
    return cos, sin


def _chan_dft_kernel(v_ref, w_ref, xc_ref, xs_ref, *, gd):
    r = jnp.dot(v_ref[...].astype(BF16), w_ref[...], preferred_element_type=F32)
    xc_ref[...] = r[:, :gd].astype(xc_ref.dtype)
    xs_ref[...] = r[:, gd:].astype(xs_ref.dtype)


def _chan_dft(v2, col_off, dft, wc, tm=1024):
    m = v2.shape[0]
    gd = wc.shape[0]
    tm = min(tm, m)
    goff = col_off // gd
    spec = pl.BlockSpec((tm, gd), lambda i, g: (i, g))
    est = (2 * _nbytes((tm, gd), F32) + 5 * _nbytes((tm, gd), BF16) + 2 * _nbytes((gd, 2 * gd), BF16)
           + 2 * _nbytes((tm, 2 * gd), F32))
    return pl.pallas_call(
        functools.partial(_chan_dft_kernel, gd=gd),
        grid=(m // tm, dft // gd),
        in_specs=[pl.BlockSpec((tm, gd), lambda i, g: (i, goff + g)),
                  pl.BlockSpec((gd, 2 * gd), lambda i, g: (0, 0))],
        out_specs=[spec, spec],
        out_shape=[jax.ShapeDtypeStruct((m, dft), BF16)] * 2,
        compiler_params=_cparams(("parallel", "parallel"), est),
        name="chan_dft",
    )(v2, wc)


def _seq_dft_kernel(cl_ref, sl_ref, xc_ref, xs_ref, o_ref):
    y = (jnp.dot(cl_ref[...], xc_ref[0], preferred_element_type=F32)
         - jnp.dot(sl_ref[...], xs_ref[0], preferred_element_type=F32))
    o_ref[0] = y.astype(o_ref.dtype)


def _seq_dft(xc, xs, cl, sl, batch, tm=512, tn=512):
    m, dft = xc.shape
    l = m // batch
    tm = min(tm, l)
    tn = min(tn, dft)
    xc3, xs3 = xc.reshape(batch, l, dft), xs.reshape(batch, l, dft)
    est = 4 * _nbytes((tm, l), BF16) + 4 * _nbytes((l, tn), BF16) + 4 * _nbytes((tm, tn), F32)
    out = pl.pallas_call(
        _seq_dft_kernel,
        grid=(batch, dft // tn, l // tm),
        in_specs=[
            pl.BlockSpec((tm, l), lambda b, jn, i: (i, 0)),
            pl.BlockSpec((tm, l), lambda b, jn, i: (i, 0)),
            pl.BlockSpec((1, l, tn), lambda b, jn, i: (b, 0, jn)),
            pl.BlockSpec((1, l, tn), lambda b, jn, i: (b, 0, jn)),
        ],
        out_specs=pl.BlockSpec((1, tm, tn), lambda b, jn, i: (b, i, jn)),
        out_shape=jax.ShapeDtypeStruct((batch, l, dft), BF16),
        compiler_params=_cparams(("parallel", "parallel", "parallel"), est),
        name="seq_dft",
    )(cl, sl, xc3, xs3)
    return out.reshape(m, dft)


def _prefix_count(mask_f32, tri):
    e, l = mask_f32.shape
    carry = jnp.zeros((e, 1), F32)
    outs = []
    for k in range(l // CUMSUM_BLOCK):
        blk = mask_f32[:, k * CUMSUM_BLOCK:(k + 1) * CUMSUM_BLOCK]
        outs.append(jnp.dot(blk.astype(BF16), tri, preferred_element_type=F32) + carry)
        carry = carry + jnp.sum(blk, axis=1, keepdims=True)
    return outs[0] if len(outs) == 1 else jnp.concatenate(outs, axis=1)


def _route_kernel(lg_ref, posm_ref, aff_ref, bnd_ref, *, cap, tile):
    lg = lg_ref[...]
    mx = jnp.max(lg, axis=0, keepdims=True)
    ex = jnp.exp(lg - mx)
    aff = ex / jnp.sum(ex, axis=0, keepdims=True)
    bits = pltpu.bitcast(aff, jnp.int32)
    e = lg.shape[0]
    v = jnp.zeros((e, 1), jnp.int32)
    for bit in range(30, -1, -1):
        cand = v | (1 << bit)
        cnt = jnp.sum(jnp.where(bits >= cand, 1.0, 0.0), axis=1, keepdims=True)
        v = jnp.where(cnt >= cap, cand, v)
    gt = bits > v
    eq = bits == v
    n_gt = jnp.sum(gt.astype(F32), axis=1, keepdims=True)
    r = lax.broadcasted_iota(jnp.int32, (CUMSUM_BLOCK, CUMSUM_BLOCK), 0)
    c = lax.broadcasted_iota(jnp.int32, (CUMSUM_BLOCK, CUMSUM_BLOCK), 1)
    tri = jnp.where(r < c, 1.0, 0.0).astype(BF16)
    eq_f = jnp.where(eq, 1.0, 0.0)
    tie_ok = _prefix_count(eq_f, tri) < (cap - n_gt)
    sel_f = jnp.where(gt, 1.0, jnp.where(tie_ok, eq_f, 0.0))
    pos = _prefix_count(sel_f, tri)
    posm_ref[0] = jnp.where(sel_f > 0.5, pos, -1.0).astype(jnp.int32)
    aff_ref[0] = aff
    l = lg.shape[1]
    starts = [pos[:, k * tile:k * tile + 1] for k in range(l // tile)]
    starts.append(jnp.full((e, 1), float(cap), F32))
    bnd_ref[0] = jnp.concatenate(starts, axis=1).astype(jnp.int32)


def _route(lg_t, batch, cap, tile):
    e, m = lg_t.shape
    l = m // batch
    nt = l // tile
    est = 16 * _nbytes((e, l), F32)
    return pl.pallas_call(
        functools.partial(_route_kernel, cap=cap, tile=tile),
        grid=(batch,),
        in_specs=[pl.BlockSpec((e, l), lambda b: (0, b))],
        out_specs=[pl.BlockSpec((1, e, l), lambda b: (b, 0, 0))] * 2
        + [pl.BlockSpec((1, e, nt + 1), lambda b: (b, 0, 0))],
        out_shape=[jax.ShapeDtypeStruct((batch, e, l), jnp.int32),
                   jax.ShapeDtypeStruct((batch, e, l), F32),
                   jax.ShapeDtypeStruct((batch, e, nt + 1), jnp.int32)],
        compiler_params=_cparams(("parallel",), est),
        name="route",
    )(lg_t)


def _slots_kernel(posm_ref, aff_ref, *rest, cap):
    idx_ref, gs_ref = rest[-2], rest[-1]
    pm = posm_ref[0, 0]
    l = pm.shape[1]
    slot = lax.broadcasted_iota(jnp.int32, (cap, l), 0)
    tok = lax.broadcasted_iota(jnp.int32, (1, l), 1).astype(F32)
    hit = pm == slot
    idx_ref[0, 0] = jnp.sum(jnp.where(hit, tok, 0.0), axis=1, keepdims=True).astype(jnp.int32)
    gs_ref[0] = jnp.sum(jnp.where(hit, aff_ref[0, 0], 0.0), axis=1, keepdims=True)


def _slots(posm, aff, cap, n_rows, row_off, gs_all=None):
    batch, e, l = posm.shape
    blk = row_off // cap
    in_specs = [pl.BlockSpec((1, 1, 1, l), lambda b, ee: (b, ee, 0, 0))] * 2
    args = [posm.reshape(batch, e, 1, l), aff.reshape(batch, e, 1, l)]
    aliases = {}
    if gs_all is not None:
        in_specs.append(pl.BlockSpec(memory_space=pl.ANY))
        args.append(gs_all)
        aliases = {2: 1}
    est = 6 * _nbytes((cap, l), F32) + 4 * _nbytes((cap, V7X_LANES), F32)
    return pl.pallas_call(
        functools.partial(_slots_kernel, cap=cap),
        grid=(batch, e),
        in_specs=in_specs,
        out_specs=[pl.BlockSpec((1, 1, cap, 1), lambda b, ee: (b, ee, 0, 0)),
                   pl.BlockSpec((1, cap, 1), lambda b, ee: (ee, blk + b, 0))],
        out_shape=[jax.ShapeDtypeStruct((batch, e, cap, 1), jnp.int32),
                   jax.ShapeDtypeStruct((e, n_rows, 1), F32)],
        input_output_aliases=aliases,
        compiler_params=_cparams(("parallel", "parallel"), est),
        name="moe_slots",
    )(*args)


GATHER_ROW_CHUNK = 64


def _gather_kernel(idx_ref, x_hbm, g_ref, sh_ref, sc_ref, *rest, cap, seq, n_exp, n_steps, eps):
    o_ref, rows_scr, sems = rest[-3], rest[-2], rest[-1]
    n = pl.program_id(0) * n_exp + pl.program_id(1)
    cur = n % 2
    nxt_step = jnp.minimum(n + 1, n_steps - 1)
    ch = min(GATHER_ROW_CHUNK, cap)

    def issue_rows(step, buf, lo):
        base = step * cap
        row0 = (step // n_exp) * seq
        for k in range(ch):
            src = x_hbm.at[pl.ds(row0 + idx_ref[base + lo + k], 1)]
            pltpu.make_async_copy(src, rows_scr.at[buf, pl.ds(lo + k, 1)], sems.at[buf]).start()

    def wait_rows(buf):
        pltpu.make_async_copy(x_hbm.at[pl.ds(0, cap)], rows_scr.at[buf], sems.at[buf]).wait()

    @pl.when(n == 0)
    def _():
        def first(c, carry):
            issue_rows(0, 0, c * ch)
            return carry
        lax.fori_loop(0, cap // ch, first, 0)

    wait_rows(cur)
    g, shift, scale = g_ref[...], sh_ref[0], sc_ref[0]

    def body(c, carry):
        r = pl.multiple_of(c * ch, ch)
        issue_rows(nxt_step, 1 - cur, r)
        y = _rms_modulate(rows_scr[cur, pl.ds(r, ch), :], g, shift, scale, eps)
        o_ref[0, pl.ds(r, ch), :] = y.astype(o_ref.dtype)
        return carry

    lax.fori_loop(0, cap // ch, body, 0)

    @pl.when(n == n_steps - 1)
    def _():
        wait_rows(1 - cur)


def _gather(idx, x2, g, shift, scale, cap, n_rows, row_off, xg_all=None):
    batch, e = idx.shape[0], idx.shape[1]
    m, d = x2.shape
    seq = m // batch
    blk = row_off // cap
    in_specs = [
        pl.BlockSpec(memory_space=pl.ANY),
        pl.BlockSpec((1, d), lambda b, ee, ix: (0, 0)),
        pl.BlockSpec((1, 1, d), lambda b, ee, ix: (b, 0, 0)),
        pl.BlockSpec((1, 1, d), lambda b, ee, ix: (b, 0, 0)),
    ]
    args = [idx.reshape(-1), x2, g.reshape(1, d), shift, scale]
    aliases = {}
    if xg_all is not None:
        in_specs.append(pl.BlockSpec(memory_space=pl.ANY))
        args.append(xg_all)
        aliases = {5: 0}
    est = 2 * _nbytes((cap, d), F32) + 2 * _nbytes((cap, d), BF16) + 8 * _nbytes((GATHER_ROW_CHUNK, d), F32)
    return pl.pallas_call(
        functools.partial(_gather_kernel, cap=cap, seq=seq, n_exp=e, n_steps=batch * e,
                          eps=OP["rms_eps"]),
        grid_spec=pltpu.PrefetchScalarGridSpec(
            num_scalar_prefetch=1,
            grid=(batch, e),
            in_specs=in_specs,
            out_specs=pl.BlockSpec((1, cap, d), lambda b, ee, ix: (ee, blk + b, 0)),
            scratch_shapes=[pltpu.VMEM((2, cap, d), F32), pltpu.SemaphoreType.DMA((2,))],
        ),
        out_shape=jax.ShapeDtypeStruct((e, n_rows, d), BF16),
        input_output_aliases=aliases,
        compiler_params=_cparams(("arbitrary", "arbitrary"), est),
        name="moe_gather",
    )(*args)


EXPERT_ROW_BLOCKS = 2


def _up_kernel(xg_ref, wg_ref, wu_ref, o_ref, wg_bf, wu_bf):
    @pl.when(pl.program_id(2) == 0)
    def _():
        wg_bf[...] = wg_ref[0, 0].astype(BF16)
        wu_bf[...] = wu_ref[0, 0].astype(BF16)

    x = xg_ref[0]
    g = jnp.dot(x, wg_bf[...], preferred_element_type=F32)
    u = jnp.dot(x, wu_bf[...], preferred_element_type=F32)
    o_ref[0] = (g * jax.nn.sigmoid(g) * u).astype(o_ref.dtype)


def _expert_up(xg, w_gate, w_up, layer, tf=256):
    e, r, d = xg.shape
    f = w_gate.shape[3]
    tf = min(tf, f)
    tr = r // EXPERT_ROW_BLOCKS
    est = (4 * _nbytes((d, tf), F32) + 3 * _nbytes((d, tf), BF16) + 2 * _nbytes((tr, d), BF16)
           + 6 * _nbytes((tr, tf), F32))
    return pl.pallas_call(
        _up_kernel,
        grid=(e, f // tf, EXPERT_ROW_BLOCKS),
        in_specs=[
            pl.BlockSpec((1, tr, d), lambda ee, fj, rr: (ee, rr, 0)),
            pl.BlockSpec((1, 1, d, tf), lambda ee, fj, rr: (layer, ee, 0, fj)),
            pl.BlockSpec((1, 1, d, tf), lambda ee, fj, rr: (layer, ee, 0, fj)),
        ],
        out_specs=pl.BlockSpec((1, tr, tf), lambda ee, fj, rr: (ee, rr, fj)),
        out_shape=jax.ShapeDtypeStruct((e, r, f), BF16),
        scratch_shapes=[pltpu.VMEM((d, tf), BF16), pltpu.VMEM((d, tf), BF16)],
        compiler_params=_cparams(("parallel", "parallel", "arbitrary"), est),
        name="moe_up",
    )(xg, w_gate, w_up)


def _down_kernel(h_ref, wd_ref, gs_ref, o_ref, wd_bf):
    @pl.when(pl.program_id(2) == 0)
    def _():
        wd_bf[...] = wd_ref[0, 0].astype(BF16)

    y = jnp.dot(h_ref[0], wd_bf[...], preferred_element_type=F32)
    o_ref[0] = (y * gs_ref[0]).astype(o_ref.dtype)


def _expert_down(hid, w_down, gslot, layer, td=1024):
    e, r, f = hid.shape
    d = w_down.shape[3]
    td = min(td, d)
    tr = r // EXPERT_ROW_BLOCKS
    est = (2 * _nbytes((f, td), F32) + 2 * _nbytes((f, td), BF16) + 2 * _nbytes((tr, f), BF16)
           + 4 * _nbytes((tr, td), F32) + 2 * _nbytes((tr, V7X_LANES), F32))
    return pl.pallas_call(
        _down_kernel,
        grid=(e, d // td, EXPERT_ROW_BLOCKS),
        in_specs=[
            pl.BlockSpec((1, tr, f), lambda ee, dj, rr: (ee, rr, 0)),
            pl.BlockSpec((1, 1, f, td), lambda ee, dj, rr: (layer, ee, 0, dj)),
            pl.BlockSpec((1, tr, 1), lambda ee, dj, rr: (ee, rr, 0)),
        ],
        out_specs=pl.BlockSpec((1, tr, td), lambda ee, dj, rr: (ee, rr, dj)),
        out_shape=jax.ShapeDtypeStruct((e, r, d), BF16),
        scratch_shapes=[pltpu.VMEM((f, td), BF16)],
        compiler_params=_cparams(("parallel", "parallel", "arbitrary"), est),
        name="moe_down",
    )(hid, w_down, gslot)


COMBINE_TILE = 256
DMA_ROW_ALIGN = 16


def combine_window(cap, seq, tile):
    return min(cap, max(DMA_ROW_ALIGN, 2 * tile * cap // seq))


def _combine_kernel(tbl_ref, pt_ref, y_hbm, x_ref, g_ref, *rest, n_exp, cap, win, nt, row_off,
                    n_steps, emit_x, post_norm, post_mod, eps):
    stage, extra, acc_scr, sems, xsem = rest[-5:]
    n_post_in = (1 + 2 * post_mod) if post_norm else 0
    post_in = rest[:n_post_in]
    outs = rest[n_post_in:-5]
    n = pl.program_id(0) * nt + pl.program_id(1)
    cur = n % 2
    nxt_step = jnp.minimum(n + 1, n_steps - 1)

    def window_starts(step):
        b, i = step // nt, step % nt
        t0 = (b * (nt + 1) + i) * n_exp
        out = []
        for ee in range(n_exp):
            s0a = jnp.minimum((tbl_ref[t0 + ee] // DMA_ROW_ALIGN) * DMA_ROW_ALIGN, cap - win)
            out.append(pl.multiple_of(s0a, DMA_ROW_ALIGN))
        return row_off + b * cap, out, t0

    def issue_windows(step, buf):
        row_base, starts, _ = window_starts(step)
        for ee in range(n_exp):
            pltpu.make_async_copy(y_hbm.at[ee, pl.ds(row_base + starts[ee], win), :],
                                  stage.at[buf, pl.ds(ee * win, win), :], sems.at[buf]).start()

    def wait_windows(buf):
        pltpu.make_async_copy(y_hbm.at[0, pl.ds(0, n_exp * win), :], stage.at[buf],
                              sems.at[buf]).wait()

    @pl.when(n == 0)
    def _():
        issue_windows(0, 0)

    issue_windows(nxt_step, 1 - cur)
    row_base, starts, t0 = window_starts(n)
    pt = pt_ref[...]
    lane = lax.broadcasted_iota(jnp.int32, (pt.shape[0], win), 1)
    p = jnp.concatenate(
        [jnp.where(pt[:, ee:ee + 1] - starts[ee] == lane, 1.0, 0.0).astype(BF16)
         for ee in range(n_exp)], axis=1)
    wait_windows(cur)
    acc_scr[...] = jnp.dot(p, stage[cur], preferred_element_type=F32)

    for ee in range(n_exp):
        first = starts[ee] + win
        n_extra = jnp.maximum(0, (tbl_ref[t0 + n_exp + ee] - first + win - 1) // win)

        def extra_window(w, carry, ee=ee, first=first):
            lo = first + w * win
            src0 = pl.multiple_of(jnp.minimum(lo, cap - win), DMA_ROW_ALIGN)
            cp = pltpu.make_async_copy(y_hbm.at[ee, pl.ds(row_base + src0, win), :], extra, xsem)
            cp.start()
            cp.wait()
            col = pt[:, ee:ee + 1]
            hit = jnp.where(col >= lo, col - src0, -1) == lane
            acc_scr[...] += jnp.dot(jnp.where(hit, 1.0, 0.0).astype(BF16), extra[...],
                                    preferred_element_type=F32)
            return carry

        lax.fori_loop(0, n_extra, extra_window, 0)

    xn = x_ref[...] + g_ref[0] * acc_scr[...]
    if emit_x:
        outs[0][...] = xn
    if post_norm:
        shift, scale = (post_in[1][0], post_in[2][0]) if post_mod else (None, None)
        h_ref = outs[-1]
        h_ref[...] = _rms_modulate(xn, post_in[0][...], shift, scale, eps).astype(h_ref.dtype)

    @pl.when(n == n_steps - 1)
    def _():
        wait_windows(1 - cur)


def _combine(posm_t, bounds, y, x2, gate, batch, cap, row_off, post=None, emit_x=True):
    m, d = x2.shape
    l = m // batch
    e = posm_t.shape[1]
    tm = min(COMBINE_TILE, l)
    nt = l // tm
    win = combine_window(cap, l, tm)
    assert cap % DMA_ROW_ALIGN == 0 and win % DMA_ROW_ALIGN == 0 and bounds.shape == (batch, e, nt + 1)
    tbl = bounds.transpose(0, 2, 1).reshape(-1)
    est = (2 * _nbytes((e * win, d), BF16) + _nbytes((win, d), BF16) + 9 * _nbytes((tm, d), F32)
           + 2 * _nbytes((tm, e * win), BF16) + 2 * _nbytes((tm, V7X_LANES), F32))
    row_spec = pl.BlockSpec((tm, d), lambda b, i, t: (b * nt + i, 0))
    vec_spec = pl.BlockSpec((1, 1, d), lambda b, i, t: (b, 0, 0))
    in_specs = [pl.BlockSpec((tm, e), lambda b, i, t: (b * nt + i, 0)),
                pl.BlockSpec(memory_space=pl.ANY), row_spec, vec_spec]
    args = [tbl, posm_t, y, x2, gate]
    out_specs, out_shape = [], []
    if emit_x:
        out_specs.append(row_spec)
        out_shape.append(jax.ShapeDtypeStruct((m, d), F32))
    post_mod = False
    if post is not None:
        p_gain, p_shift, p_scale, p_dtype = post
        post_mod = p_shift is not None
        in_specs.append(pl.BlockSpec((1, d), lambda b, i, t: (0, 0)))
        args.append(p_gain.reshape(1, d))
        if post_mod:
            in_specs += [vec_spec, vec_spec]
            args += [p_shift, p_scale]
        out_specs.append(row_spec)
        out_shape.append(jax.ShapeDtypeStruct((m, d), p_dtype))
        est += 2 * _nbytes((tm, d), p_dtype)
    return pl.pallas_call(
        functools.partial(_combine_kernel, n_exp=e, cap=cap, win=win, nt=nt, row_off=row_off,
                          n_steps=batch * nt, emit_x=emit_x, post_norm=post is not None,
                          post_mod=post_mod, eps=OP["rms_eps"]),
        grid_spec=pltpu.PrefetchScalarGridSpec(
            num_scalar_prefetch=1,
            grid=(batch, nt),
            in_specs=in_specs,
            out_specs=out_specs,
            scratch_shapes=[pltpu.VMEM((2, e * win, d), BF16), pltpu.VMEM((win, d), BF16),
                            pltpu.VMEM((tm, d), F32), pltpu.SemaphoreType.DMA((2,)),
                            pltpu.SemaphoreType.DMA(())],
        ),
        out_shape=out_shape,
        input_output_aliases={3: 0} if emit_x else {},
        compiler_params=_cparams(("arbitrary", "arbitrary"), est),
        name="moe_combine",
    )(*args)


def _sigmoid(z):
    return jax.nn.sigmoid(z)


def _in_proj(h2, w_in, layer, n_cols):
    return _mm([(h2, w_in, layer, 0)], [], lambda accs, ex: accs[0], n_cols, F32, 1024,
               math.gcd(512, n_cols), name="in_proj")


def _mixer_out(x2, a2, px, g_mix, wts, layer, dfts, batch, rpb):
    w_glu, w_s5o, w_fto, w_out = wts
    wc, cl, sl = dfts
    d = x2.shape[1]
    d_s5 = a2.shape[1]
    d_ft = w_fto.shape[1]
    xc, xs = _chan_dft(px, d_s5, d_ft, wc)
    yf = _seq_dft(xc, xs, cl, sl, batch)
    glu = _mm([(a2, w_glu, layer, 0)], [("tile", a2, 0)],
              lambda accs, ex: ex[0] * _sigmoid(accs[0]),
              d_s5, BF16, 1024, 512, name="glu")
    tn = math.gcd(512, d_s5 + d_ft, d)
    g0 = (d_s5 + d_ft) // tn
    merged = _mm([(glu, w_s5o, layer, 0), (yf, w_fto, layer, 0)],
                 [("tile", px, g0), ("tile", px, g0 + d // tn)],
                 lambda accs, ex: _sigmoid(ex[0]) * accs[0] + _sigmoid(ex[1]) * accs[1],
                 d, BF16, 1024, tn, name="merge")
    return _mm([(merged, w_out, layer, 0)], [("tile", x2, 0), ("row", g_mix, 0)],
               lambda accs, ex: ex[0] + ex[1] * accs[0],
               d, F32, 1024, tn, rows_per_batch=rpb, alias_extra=0, name="out_proj")


def _moe(streams, w_gate, w_up, w_down, layer):
    e = streams[0][6].shape[0]
    caps = [OP["capacity_factor"] * (s[0].shape[0] // s[1]) // e for s in streams]
    offs, n_rows = [], 0
    for s, cap in zip(streams, caps):
        assert n_rows % cap == 0
        offs.append(n_rows)
        n_rows += s[1] * cap
    xg = gslot = None
    if len(streams) > 1:
        xg = jnp.zeros((e, n_rows, streams[0][0].shape[1]), BF16)
        gslot = jnp.zeros((e, n_rows, 1), F32)
    routed = []
    for (x2, batch, g, shift, scale, _, lg_t, _, _), cap, off in zip(streams, caps, offs):
        tile = min(COMBINE_TILE, x2.shape[0] // batch)
        posm, aff, bounds = _route(lg_t, batch, cap, tile)
        idx, gslot = _slots(posm, aff, cap, n_rows, off, gslot)
        xg = _gather(idx, x2, g, shift, scale, cap, n_rows, off, xg)
        routed.append((posm, bounds))
    y = _expert_down(_expert_up(xg, w_gate, w_up, layer), w_down, gslot, layer)
    outs = []
    for (x2, batch, _, _, _, gate, _, post, emit_x), cap, off, (posm, bounds) in zip(
            streams, caps, offs, routed):
        posm_t = posm.transpose(0, 2, 1).reshape(x2.shape[0], e)
        outs.append(_combine(posm_t, bounds, y, x2, gate, batch, cap, off, post, emit_x))
    return outs


def _grid_posembed(n_tok, d):
    gw = OP["grid_w"]
    rows = n_tok // gw
    quarter = d // 4
    inv_freq = OP["pos_base"] ** (-jnp.arange(quarter, dtype=F32) / quarter)
    ang_r = jnp.arange(rows, dtype=F32)[:, None] * inv_freq
    ang_c = jnp.arange(gw, dtype=F32)[:, None] * inv_freq
    shape = (rows, gw, quarter)
    parts = [jnp.broadcast_to(jnp.sin(ang_r)[:, None, :], shape),
             jnp.broadcast_to(jnp.cos(ang_r)[:, None, :], shape),
             jnp.broadcast_to(jnp.sin(ang_c)[None, :, :], shape),
             jnp.broadcast_to(jnp.cos(ang_c)[None, :, :], shape)]
    return jnp.concatenate(parts, axis=-1).reshape(n_tok, d)


def kernel(x, c, ctx, c_ctx, ada_w, ada_b, norm_mix_g, norm_ffn_g, w_in, s5_lam_re, s5_lam_im,
           s5_log_dt, s5_b_re, s5_b_im, s5_c_re, s5_c_im, s5_d, w_glu, w_s5_out, w_ft_out, w_out,
           w_router, w_gate, w_up, w_down, norm_final_g):
    batch, seq, d = x.shape
    ctx_len = ctx.shape[1]
    depth = ada_w.shape[0]
    d_s5 = w_glu.shape[1]
    d_ft = w_ft_out.shape[1]
    n_mod = OP["n_mod"]
    m_x, m_c = batch * seq, batch * ctx_len

    x2 = x.reshape(m_x, d)
    pos = _grid_posembed(seq, d).astype(x.dtype)
    c2 = ctx.reshape(m_c, d)

    rows = -(-(batch + 1) // 8) * 8
    c8 = jnp.zeros((rows, d), F32).at[:batch].set(c).at[batch].set(c_ctx)
    mod = _adaln(c8, ada_w, ada_b)

    gd = d_ft // OP["ft_groups"]
    wc_c, wc_s = _dft_mats(gd)
    wc = jnp.concatenate([wc_c, wc_s], axis=1)
    dft_x = (wc,) + _dft_mats(seq)
    dft_c = (wc,) + _dft_mats(ctx_len)

    bd, wc_s5, oc_s5, at = jax.vmap(lambda *p: _s5_operators(*p, S5_CHUNK))(
        s5_lam_re, s5_lam_im, s5_log_dt, s5_b_re, s5_b_im, s5_c_re, s5_c_im, s5_d)
    s5_ops = (bd, wc_s5, oc_s5) + _s5_masks(S5_CHUNK) + (at,)
    d_in = w_in.shape[2]
    mix_w = (w_glu, w_s5_out, w_ft_out, w_out)

    mods_x = [[mod[i, :batch, k * d:(k + 1) * d].reshape(batch, 1, d) for k in range(n_mod)]
              for i in range(depth)]
    mods_c = [[jnp.broadcast_to(mod[i, batch:batch + 1, k * d:(k + 1) * d].reshape(1, 1, d),
                                (batch, 1, d)) for k in range(n_mod)] for i in range(depth)]

    hx = hc = None
    for i in range(depth):
        last = i == depth - 1
        mx, mc = mods_x[i], mods_c[i]
        wr_t = w_router[i].T.astype(BF16)
        if i == 0:
            hc = _norm(c2, norm_mix_g[i], mc[0], mc[1], rows_per_batch=ctx_len)
            hx, x2 = _norm(x2, norm_mix_g[i], mx[0], mx[1], rows_per_batch=seq, pos=pos)
        pc = _in_proj(hc, w_in, i, d_s5 if last else d_in)
        px = _in_proj(hx, w_in, i, d_in)
        ac, ax = _s5(pc, px, d_s5, batch, s5_ops, i)
        x2 = _mixer_out(x2, ax, px, mx[2], mix_w, i, dft_x, batch, seq)
        lgx = _norm(x2, norm_ffn_g[i], mx[3], mx[4], rows_per_batch=seq, wr_t=wr_t, emit_h=False)
        if last:
            post_x, post_c = (norm_final_g, None, None, x.dtype), None
        else:
            post_x = (norm_mix_g[i + 1], mods_x[i + 1][0], mods_x[i + 1][1], BF16)
            post_c = (norm_mix_g[i + 1], mods_c[i + 1][0], mods_c[i + 1][1], BF16)
        streams = [(x2, batch, norm_ffn_g[i], mx[3], mx[4], mx[5], lgx, post_x, not last)]
        if not last:
            c2 = _mixer_out(c2, ac, pc, mc[2], mix_w, i, dft_c, batch, ctx_len)
            lgc = _norm(c2, norm_ffn_g[i], mc[3], mc[4], rows_per_batch=ctx_len, wr_t=wr_t,
                        emit_h=False)
            streams.append((c2, batch, norm_ffn_g[i], mc[3], mc[4], mc[5], lgc, post_c, True))
        outs = _moe(streams, w_gate, w_up, w_down, i)
        if last:
            return outs[0][0].reshape(batch, seq, d)
        x2, hx = outs[0]
        c2, hc = outs[1]
```

```python
import functools
import math

import jax
import jax.numpy as jnp
from jax import lax
from jax.experimental import pallas as pl
from jax.experimental.pallas import tpu as pltpu

BF16 = jnp.bfloat16
F32 = jnp.float32
HI = lax.Precision.HIGHEST

OP = dict(
    s5_group_dim=16,
    s5_state=64,
    ft_groups=4,
    capacity_factor=2,
    n_mod=6,
    rms_eps=1e-6,
    pos_base=10000.0,
    grid_w=64,
    lambda_re_max=-1e-4,
)

V7X_VMEM_BYTES = 64 * 1024 * 1024
V7X_LANES = 128
S5_TILE_GROUPS = V7X_LANES // 16
S5_CHUNK = 16
CUMSUM_BLOCK = 256


def _cparams(sem, vmem_est):
    limit = int(min(max(vmem_est * 5 // 4 + (4 << 20), 32 << 20), V7X_VMEM_BYTES - (6 << 20)))
    return pltpu.CompilerParams(dimension_semantics=sem, vmem_limit_bytes=limit)


def _nbytes(shape, dtype):
    return math.prod(shape) * jnp.dtype(dtype).itemsize


def _adaln_kernel(c_ref, w_ref, b_ref, o_ref):
    cv = c_ref[...]
    a = (cv * jax.nn.sigmoid(cv)).astype(BF16)
    o_ref[0] = jnp.dot(a, w_ref[0].astype(BF16), preferred_element_type=F32) + b_ref[0]


def _adaln(c8, ada_w, ada_b, tn=512):
    depth, d, n6 = ada_w.shape
    rows = c8.shape[0]
    tn = min(tn, n6)
    est = 2 * _nbytes((d, tn), F32) + 4 * _nbytes((rows, tn), F32) + _nbytes((rows, d), F32) * 2
    return pl.pallas_call(
        _adaln_kernel,
        grid=(depth, n6 // tn),
        in_specs=[
            pl.BlockSpec((rows, d), lambda l, j: (0, 0)),
            pl.BlockSpec((1, d, tn), lambda l, j: (l, 0, j)),
            pl.BlockSpec((1, 1, tn), lambda l, j: (l, 0, j)),
        ],
        out_specs=pl.BlockSpec((1, rows, tn), lambda l, j: (l, 0, j)),
        out_shape=jax.ShapeDtypeStruct((depth, rows, n6), F32),
        compiler_params=_cparams(("parallel", "parallel"), est),
        name="adaln",
    )(c8, ada_w, ada_b.reshape(depth, 1, n6))


def _rms_modulate(xf, g, shift, scale, eps):
    ms = jnp.mean(xf * xf, axis=-1, keepdims=True)
    y = xf * lax.rsqrt(ms + eps) * g
    if shift is not None:
        y = y * (1.0 + scale) + shift
    return y


def _pack_bf16_pairs(y):
    half = y.shape[1] // 2
    lo = pltpu.bitcast(y[:, :half].astype(BF16).astype(F32), jnp.uint32) >> 16
    hi = pltpu.bitcast(y[:, half:].astype(BF16).astype(F32), jnp.uint32) & jnp.uint32(0xFFFF0000)
    return hi | lo


def _unpack_bf16_pairs(p):
    lo = pltpu.bitcast(p << 16, F32).astype(BF16)
    hi = pltpu.bitcast(p & jnp.uint32(0xFFFF0000), F32).astype(BF16)
    return lo, hi


def _norm_kernel(*refs, eps, modulate, router, add_pos, emit_h, pack):
    x_ref, g_ref = refs[0], refs[1]
    k = 2
    xf = x_ref[...]
    if add_pos:
        xf = xf + refs[k][...]
        k += 1
        refs[-1][...] = xf
    shift = scale = None
    if modulate:
        shift, scale = refs[k][0], refs[k + 1][0]
        k += 2
    y = _rms_modulate(xf, g_ref[...], shift, scale, eps)
    if router:
        wr_ref = refs[k]
        k += 1
    if emit_h:
        refs[k][...] = y.astype(refs[k].dtype)
        k += 1
    if router:
        refs[k][...] = lax.dot_general(
            wr_ref[...], y.astype(BF16), (((1,), (1,)), ((), ())), preferred_element_type=F32)
        k += 1
    if pack:
        refs[k][...] = _pack_bf16_pairs(y)


def _norm(x2, g, shift=None, scale=None, rows_per_batch=None, wr_t=None, out_dtype=BF16, tm=256,
          pos=None, emit_h=True, pack=False):
    m, d = x2.shape
    modulate = shift is not None
    router = wr_t is not None
    add_pos = pos is not None
    rpb = rows_per_batch if rows_per_batch is not None else m
    tm = min(tm, rpb)
    tpb = rpb // tm
    in_specs = [pl.BlockSpec((tm, d), lambda i: (i, 0)), pl.BlockSpec((1, d), lambda i: (0, 0))]
    args = [x2, g.reshape(1, d)]
    if add_pos:
        in_specs.append(pl.BlockSpec((tm, d), lambda i: (i % tpb, 0)))
        args.append(pos)
    if modulate:
        in_specs += [pl.BlockSpec((1, 1, d), lambda i: (i // tpb, 0, 0))] * 2
        args += [shift, scale]
    out_specs, out_shape = [], []
    if emit_h:
        out_specs.append(pl.BlockSpec((tm, d), lambda i: (i, 0)))
        out_shape.append(jax.ShapeDtypeStruct((m, d), out_dtype))
    if router:
        e = wr_t.shape[0]
        in_specs.append(pl.BlockSpec((e, d), lambda i: (0, 0)))
        args.append(wr_t)
        out_specs.append(pl.BlockSpec((e, tm), lambda i: (0, i)))
        out_shape.append(jax.ShapeDtypeStruct((e, m), F32))
    est = 2 * _nbytes((tm, d), F32) + 2 * _nbytes((tm, d), out_dtype) + 3 * _nbytes((tm, d), F32)
    if pack:
        out_specs.append(pl.BlockSpec((tm, d // 2), lambda i: (i, 0)))
        out_shape.append(jax.ShapeDtypeStruct((m, d // 2), jnp.uint32))
        est += 4 * _nbytes((tm, d // 2), F32)
    if add_pos:
        out_specs.append(pl.BlockSpec((tm, d), lambda i: (i, 0)))
        out_shape.append(jax.ShapeDtypeStruct((m, d), F32))
        est += 4 * _nbytes((tm, d), F32)
    res = pl.pallas_call(
        functools.partial(_norm_kernel, eps=OP["rms_eps"], modulate=modulate, router=router,
                          add_pos=add_pos, emit_h=emit_h, pack=pack),
        grid=(m // tm,),
        in_specs=in_specs,
        out_specs=out_specs,
        out_shape=out_shape,
        compiler_params=_cparams(("parallel",), est),
        name="rmsnorm",
    )(*args)
    return res if len(res) > 1 else res[0]


def _mm_kernel(*refs, n_pairs, kinds, epilogue):
    n_in = 2 * n_pairs + len(kinds)
    o_ref = refs[n_in]
    w_bf = refs[n_in + 1:]

    @pl.when(pl.program_id(1) == 0)
    def _():
        for k in range(n_pairs):
            w_bf[k][...] = refs[2 * k + 1][...].astype(BF16)

    accs = [jnp.dot(refs[2 * k][...].astype(BF16), w_bf[k][...], preferred_element_type=F32)
            for k in range(n_pairs)]
    ex = []
    for k, kind in enumerate(kinds):
        r = refs[2 * n_pairs + k]
        ex.append(r[0] if kind == "row" else r[...])
    o_ref[...] = epilogue(accs, ex).astype(o_ref.dtype)


def _mm(pairs, extras, epilogue, n_out, out_dtype, tm, tn, rows_per_batch=None, alias_extra=None,
        name="mm"):
    m = pairs[0][0].shape[0]
    rpb = rows_per_batch if rows_per_batch is not None else m
    tm = min(tm, rpb)
    tn = min(tn, n_out)
    tpb = rpb // tm
    in_specs, args, scratch = [], [], []
    est = 2 * _nbytes((tm, tn), out_dtype) + 2 * _nbytes((tm, tn), F32) * max(1, len(pairs))
    for a, w, layer, off in pairs:
        kdim = a.shape[1]
        in_specs.append(pl.BlockSpec((tm, kdim), lambda j, i: (i, 0)))
        in_specs.append(pl.BlockSpec((None, kdim, tn),
                                     lambda j, i, off=off, layer=layer: (layer, 0, j + off)))
        scratch.append(pltpu.VMEM((kdim, tn), BF16))
        args += [a, w]
        est += (2 * _nbytes((tm, kdim), a.dtype) + 2 * _nbytes((kdim, tn), w.dtype)
                + _nbytes((kdim, tn), BF16))
        if a.dtype != BF16:
            est += _nbytes((tm, kdim), BF16)
    kinds = []
    for kind, arr, off in extras:
        kinds.append(kind)
        if kind == "tile":
            in_specs.append(pl.BlockSpec((tm, tn), lambda j, i, off=off: (i, j + off)))
            est += 2 * _nbytes((tm, tn), arr.dtype)
        else:
            in_specs.append(pl.BlockSpec((1, 1, tn), lambda j, i, off=off: (i // tpb, 0, j + off)))
        args.append(arr)
    aliases = {}
    if alias_extra is not None:
        aliases = {2 * len(pairs) + alias_extra: 0}
    return pl.pallas_call(
        functools.partial(_mm_kernel, n_pairs=len(pairs), kinds=tuple(kinds), epilogue=epilogue),
        grid=(n_out // tn, m // tm),
        in_specs=in_specs,
        out_specs=pl.BlockSpec((tm, tn), lambda j, i: (i, j)),
        out_shape=jax.ShapeDtypeStruct((m, n_out), out_dtype),
        scratch_shapes=scratch,
        input_output_aliases=aliases,
        compiler_params=_cparams(("parallel", "arbitrary"), est),
        name=name,
    )(*args)


def _s5_operators(lam_re, lam_im, log_dt, b_re, b_im, c_re, c_im, d_skip, t):
    n = OP["s5_group_dim"]
    lr = jnp.minimum(lam_re.astype(F32), OP["lambda_re_max"])
    li = lam_im.astype(F32)
    dt = jnp.exp(log_dt.astype(F32))[..., None]
    g, p = lr.shape[1], lr.shape[2]
    gt = S5_TILE_GROUPS
    j = g // gt
    mag = jnp.exp(lr * dt)
    a_re = mag * jnp.cos(li * dt)
    a_im = mag * jnp.sin(li * dt)
    num_re, num_im = a_re - 1.0, a_im
    den = lr * lr + li * li
    f_re = (num_re * lr + num_im * li) / den
    f_im = (num_im * lr - num_re * li) / den
    br, bi = b_re.astype(F32), b_im.astype(F32)
    bb_re = f_re[..., None] * br - f_im[..., None] * bi
    bb_im = f_re[..., None] * bi + f_im[..., None] * br
    cr, ci = c_re.astype(F32), c_im.astype(F32)

    k = jnp.arange(t + 1, dtype=F32)[:, None, None, None]
    pw_mag = jnp.exp(lr[None] * dt[None] * k)
    pw_re = pw_mag * jnp.cos(li[None] * dt[None] * k)
    pw_im = pw_mag * jnp.sin(li[None] * dt[None] * k)

    ca_re = cr[None] * pw_re[:t, :, :, None, :] - ci[None] * pw_im[:t, :, :, None, :]
    ca_im = cr[None] * pw_im[:t, :, :, None, :] + ci[None] * pw_re[:t, :, :, None, :]
    bt_re, bt_im = bb_re.transpose(0, 1, 3, 2), bb_im.transpose(0, 1, 3, 2)
    kk = jnp.sum(ca_re[:, :, :, :, None, :] * bt_re[None, :, :, None, :, :]
                 - ca_im[:, :, :, :, None, :] * bt_im[None, :, :, None, :, :], axis=-1)
    kf, kb = kk[:, 0], kk[:, 1]
    k0 = kf[0] + kb[0] + d_skip.astype(F32).reshape(g, n)[:, :, None] * jnp.eye(n, dtype=F32)
    kall = jnp.concatenate([kb[1:][::-1], k0[None], kf[1:]], axis=0)
    kc = kall.reshape(2 * t - 1, j, gt, n, n).transpose(1, 0, 4, 2, 3)
    kc = kc.reshape(j, 2 * t - 1, n, gt * n).astype(BF16)
    lane = jnp.arange(gt * n)
    same_group = (lane[:, None] // n) == (lane[None, :] // n)
    bd = jnp.where(same_group, jnp.tile(kc, (1, 1, gt, 1)), 0)

    ps_re = jnp.stack([pw_re[:t][::-1, 0], pw_re[:t, 1]], axis=0)
    ps_im = jnp.stack([pw_im[:t][::-1, 0], pw_im[:t, 1]], axis=0)
    w_re = ps_re[..., None] * bb_re[:, None] - ps_im[..., None] * bb_im[:, None]
    w_im = ps_re[..., None] * bb_im[:, None] + ps_im[..., None] * bb_re[:, None]
    w6 = jnp.stack([w_re, w_im], axis=1).reshape(2, 2, t, j, gt, p, n)
    wc = w6.transpose(3, 2, 6, 0, 1, 4, 5).reshape(j, t, n, 4 * gt * p).astype(BF16)

    po_re = jnp.stack([pw_re[1:, 0], pw_re[1:][::-1, 1]], axis=0)
    po_im = jnp.stack([pw_im[1:, 0], pw_im[1:][::-1, 1]], axis=0)
    co_re = cr[:, None] * po_re[:, :, :, None, :] - ci[:, None] * po_im[:, :, :, None, :]
    co_im = cr[:, None] * po_im[:, :, :, None, :] + ci[:, None] * po_re[:, :, :, None, :]
    o6 = jnp.stack([co_re, -co_im], axis=1).reshape(2, 2, t, j, gt, n, p)
    oc = o6.transpose(3, 0, 1, 6, 2, 4, 5).reshape(j, 4, p, t * gt * n).astype(BF16)
    at = jnp.stack([pw_re[t, 0], pw_im[t, 0], pw_re[t, 1], pw_im[t, 1]], axis=0)
    at = at.reshape(4, j, gt * p).transpose(1, 0, 2).reshape(j, 1, 4 * gt * p)
    return bd, wc, oc, at


def _s5_masks(t):
    n, p, gt = OP["s5_group_dim"], OP["s5_state"], S5_TILE_GROUPS
    col_group = (jnp.arange(4 * gt * p) // p) % gt
    mask_w = ((jnp.arange(gt * n)[:, None] // n) == col_group[None, :]).astype(BF16)
    out_group = (jnp.arange(t * gt * n) // n) % gt
    mask_o = ((jnp.arange(gt * p)[:, None] // p) == out_group[None, :]).astype(BF16)
    return mask_w, mask_o


def _gelu_tanh(y):
    return 0.5 * y * (1.0 + jnp.tanh(0.7978845608028654 * (y + 0.044715 * y * y * y)))


def _s5_kernel(uc_ref, ux_ref, bd_ref, wc_ref, oc_ref, mw_ref, mo_ref, at_ref, oc_out, ox_out,
               mt_scr, wst_scr, wout_scr, s_scr, hin_scr, hend_scr, *, c_ctx, c_lat, half, t, batch):
    lanes = ux_ref.shape[1]
    step = pl.program_id(1)

    @pl.when(step == 0)
    def _():
        for s in range(t):
            for k in range(t):
                mt_scr[s * lanes:(s + 1) * lanes, k * lanes:(k + 1) * lanes] = bd_ref[0, k - s + t - 1]
            rep = lanes // wc_ref.shape[2]
            wst_scr[s * lanes:(s + 1) * lanes, :] = jnp.tile(wc_ref[0, s], (rep, 1)) * mw_ref[...]
        rows = mo_ref.shape[0]
        for q in range(oc_ref.shape[1]):
            rep = rows // oc_ref.shape[2]
            wout_scr[q * rows:(q + 1) * rows, :] = jnp.tile(oc_ref[0, q], (rep, 1)) * mo_ref[...]

    at = at_ref[0]
    afr, afi = at[:, 0:half], at[:, half:2 * half]
    abr, abi = at[:, 2 * half:3 * half], at[:, 3 * half:4 * half]

    def run(u_ref, o_ref, n_chunks, h0):
        u = jnp.concatenate([u_ref[pl.ds(s, n_chunks, stride=t), :] for s in range(t)],
                            axis=1).astype(BF16)
        s_scr[0:n_chunks, :] = jnp.dot(u, wst_scr[...], preferred_element_type=F32)
        init = (h0[:, 0:half], h0[:, half:2 * half], h0[:, 2 * half:3 * half],
                h0[:, 3 * half:4 * half])

        def body(i, carry):
            hfr, hfi, hbr, hbi = carry
            cb = n_chunks - 1 - i
            hin_scr[pl.ds(i, 1), 0:half] = hfr
            hin_scr[pl.ds(i, 1), half:2 * half] = hfi
            hin_scr[pl.ds(cb, 1), 2 * half:3 * half] = hbr
            hin_scr[pl.ds(cb, 1), 3 * half:4 * half] = hbi
            sf = s_scr[pl.ds(i, 1), 0:2 * half]
            sb = s_scr[pl.ds(cb, 1), 2 * half:4 * half]
            nfr = afr * hfr - afi * hfi + sf[:, 0:half]
            nfi = afr * hfi + afi * hfr + sf[:, half:2 * half]
            nbr = abr * hbr - abi * hbi + sb[:, 0:half]
            nbi = abr * hbi + abi * hbr + sb[:, half:2 * half]
            return nfr, nfi, nbr, nbi

        end = lax.fori_loop(0, n_chunks, body, init)
        y = (jnp.dot(u, mt_scr[...], preferred_element_type=F32)
             + jnp.dot(hin_scr[0:n_chunks, :].astype(BF16), wout_scr[...],
                       preferred_element_type=F32))
        for k in range(t):
            o_ref[pl.ds(k, n_chunks, stride=t), :] = _gelu_tanh(
                y[:, k * lanes:(k + 1) * lanes]).astype(o_ref.dtype)
        return jnp.concatenate(end, axis=1)

    @pl.when(step < batch)
    def _():
        hend_scr[pl.ds(step, 1), :] = run(uc_ref, oc_out, c_ctx, jnp.zeros((1, 4 * half), F32))

    @pl.when(step >= batch)
    def _():
        run(ux_ref, ox_out, c_lat, hend_scr[pl.ds(step - batch, 1), :])


def _s5(uc2, ux2, ds5, batch, ops, layer):
    bd, wc, oc, mask_w, mask_o, at = ops
    mc, mx = uc2.shape[0], ux2.shape[0]
    lc, lx = mc // batch, mx // batch
    t = S5_CHUNK
    cc, cx = lc // t, lx // t
    j = ds5 // V7X_LANES
    w = t * V7X_LANES
    sw = wc.shape[4]
    compact = sum(_nbytes(a.shape[2:], BF16) for a in (bd, wc, oc)) + _nbytes(mask_w.shape, BF16) \
        + _nbytes(mask_o.shape, BF16)
    est = (3 * _nbytes((w, sw), BF16) + 2 * compact + 4 * _nbytes((lx + lc, V7X_LANES), F32)
           + 2 * _nbytes((cx, sw), F32) + 3 * _nbytes((cx, w), F32) + _nbytes((cx, sw), BF16)
           + 2 * _nbytes((cx, w), BF16))
    last = batch - 1
    ctx_spec = pl.BlockSpec((lc, V7X_LANES), lambda jj, s: (jnp.minimum(s, last), jj))
    lat_spec = pl.BlockSpec((lx, V7X_LANES), lambda jj, s: (jnp.maximum(s - batch, 0), jj))

    def op_spec(a):
        return pl.BlockSpec((None, 1) + a.shape[2:], lambda jj, s: (layer, jj) + (0,) * (a.ndim - 2))

    return pl.pallas_call(
        functools.partial(_s5_kernel, c_ctx=cc, c_lat=cx, half=sw // 4, t=t, batch=batch),
        grid=(j, 2 * batch),
        in_specs=[ctx_spec, lat_spec, op_spec(bd), op_spec(wc), op_spec(oc),
                  pl.BlockSpec(mask_w.shape, lambda jj, s: (0, 0)),
                  pl.BlockSpec(mask_o.shape, lambda jj, s: (0, 0)),
                  op_spec(at)],
        out_specs=[ctx_spec, lat_spec],
        out_shape=[jax.ShapeDtypeStruct((mc, ds5), F32), jax.ShapeDtypeStruct((mx, ds5), F32)],
        scratch_shapes=[pltpu.VMEM((w, w), BF16), pltpu.VMEM((w, sw), BF16), pltpu.VMEM((sw, w), BF16),
                        pltpu.VMEM((cx, sw), F32), pltpu.VMEM((cx, sw), F32),
                        pltpu.VMEM((-(-batch // 8) * 8, sw), F32)],
        compiler_params=_cparams(("arbitrary", "arbitrary"), est),
        name="s5",
    )(uc2, ux2, bd, wc, oc, mask_w, mask_o, at)


def _dft_mats(n):
    hi = n // V7X_LANES
    j = jnp.arange(n, dtype=jnp.int32)[:, None]
    a1 = ((j * jnp.arange(V7X_LANES, dtype=jnp.int32)[None, :]) % n).astype(F32) * (2.0 * math.pi / n)
    a2 = ((j * jnp.arange(hi, dtype=jnp.int32)[None, :]) % hi).astype(F32) * (2.0 * math.pi / hi)
    c1, s1 = jnp.cos(a1)[:, None, :], jnp.sin(a1)[:, None, :]
    c2, s2 = jnp.cos(a2)[:, :, None], jnp.sin(a2)[:, :, None]
    s = 1.0 / math.sqrt(n)
    cos = ((c2 * c1 - s2 * s1) * s).astype(BF16).reshape(n, n)
    sin = ((s2 * c1 + c2 * s1) * s).astype(BF16).reshape(n, n)
    return cos, sin


def _chan_dft_kernel(v_ref, w_ref, xc_ref, xs_ref, *, gd):
    r = jnp.dot(v_ref[...].astype(BF16), w_ref[...], preferred_element_type=F32)
    xc_ref[...] = r[:, :gd].astype(xc_ref.dtype)
    xs_ref[...] = r[:, gd:].astype(xs_ref.dtype)


def _chan_dft(v2, col_off, dft, wc, tm=1024):
    m = v2.shape[0]
    gd = wc.shape[0]
    tm = min(tm, m)
    goff = col_off // gd
    spec = pl.BlockSpec((tm, gd), lambda i, g: (i, g))
    est = (2 * _nbytes((tm, gd), F32) + 5 * _nbytes((tm, gd), BF16) + 2 * _nbytes((gd, 2 * gd), BF16)
           + 2 * _nbytes((tm, 2 * gd), F32))
    return pl.pallas_call(
        functools.partial(_chan_dft_kernel, gd=gd),
        grid=(m // tm, dft // gd),
        in_specs=[pl.BlockSpec((tm, gd), lambda i, g: (i, goff + g)),
                  pl.BlockSpec((gd, 2 * gd), lambda i, g: (0, 0))],
        out_specs=[spec, spec],
        out_shape=[jax.ShapeDtypeStruct((m, dft), BF16)] * 2,
        compiler_params=_cparams(("parallel", "parallel"), est),
        name="chan_dft",
    )(v2, wc)


def _seq_dft_kernel(cl_ref, sl_ref, xc_ref, xs_ref, o_ref):
    y = (jnp.dot(cl_ref[...], xc_ref[0], preferred_element_type=F32)
         - jnp.dot(sl_ref[...], xs_ref[0], preferred_element_type=F32))
    o_ref[0] = y.astype(o_ref.dtype)


def _seq_dft(xc, xs, cl, sl, batch, tm=512, tn=512):
    m, dft = xc.shape
    l = m // batch
    tm = min(tm, l)
    tn = min(tn, dft)
    xc3, xs3 = xc.reshape(batch, l, dft), xs.reshape(batch, l, dft)
    est = 4 * _nbytes((tm, l), BF16) + 4 * _nbytes((l, tn), BF16) + 4 * _nbytes((tm, tn), F32)
    out = pl.pallas_call(
        _seq_dft_kernel,
        grid=(batch, dft // tn, l // tm),
        in_specs=[
            pl.BlockSpec((tm, l), lambda b, jn, i: (i, 0)),
            pl.BlockSpec((tm, l), lambda b, jn, i: (i, 0)),
            pl.BlockSpec((1, l, tn), lambda b, jn, i: (b, 0, jn)),
            pl.BlockSpec((1, l, tn), lambda b, jn, i: (b, 0, jn)),
        ],
        out_specs=pl.BlockSpec((1, tm, tn), lambda b, jn, i: (b, i, jn)),
        out_shape=jax.ShapeDtypeStruct((batch, l, dft), BF16),
        compiler_params=_cparams(("parallel", "parallel", "parallel"), est),
        name="seq_dft",
    )(cl, sl, xc3, xs3)
    return out.reshape(m, dft)


def _prefix_count(mask_f32, tri):
    e, l = mask_f32.shape
    carry = jnp.zeros((e, 1), F32)
    outs = []
    for k in range(l // CUMSUM_BLOCK):
        blk = mask_f32[:, k * CUMSUM_BLOCK:(k + 1) * CUMSUM_BLOCK]
        outs.append(jnp.dot(blk.astype(BF16), tri, preferred_element_type=F32) + carry)
        carry = carry + jnp.sum(blk, axis=1, keepdims=True)
    return outs[0] if len(outs) == 1 else jnp.concatenate(outs, axis=1)


def _route_kernel(lg_ref, posm_ref, aff_ref, bnd_ref, *, cap, tile):
    lg = lg_ref[...]
    mx = jnp.max(lg, axis=0, keepdims=True)
    ex = jnp.exp(lg - mx)
    aff = ex / jnp.sum(ex, axis=0, keepdims=True)
    bits = pltpu.bitcast(aff, jnp.int32)
    e = lg.shape[0]
    v = jnp.zeros((e, 1), jnp.int32)
    for bit in range(30, -1, -1):
        cand = v | (1 << bit)
        cnt = jnp.sum(jnp.where(bits >= cand, 1.0, 0.0), axis=1, keepdims=True)
        v = jnp.where(cnt >= cap, cand, v)
    gt = bits > v
    eq = bits == v
    n_gt = jnp.sum(gt.astype(F32), axis=1, keepdims=True)
    r = lax.broadcasted_iota(jnp.int32, (CUMSUM_BLOCK, CUMSUM_BLOCK), 0)
    c = lax.broadcasted_iota(jnp.int32, (CUMSUM_BLOCK, CUMSUM_BLOCK), 1)
    tri = jnp.where(r < c, 1.0, 0.0).astype(BF16)
    eq_f = jnp.where(eq, 1.0, 0.0)
    tie_ok = _prefix_count(eq_f, tri) < (cap - n_gt)
    sel_f = jnp.where(gt, 1.0, jnp.where(tie_ok, eq_f, 0.0))
    pos = _prefix_count(sel_f, tri)
    posm_ref[0] = jnp.where(sel_f > 0.5, pos, -1.0).astype(jnp.int32)
    aff_ref[0] = aff
    l = lg.shape[1]
    starts = [pos[:, k * tile:k * tile + 1] for k in range(l // tile)]
    starts.append(jnp.full((e, 1), float(cap), F32))
    bnd_ref[0] = jnp.concatenate(starts, axis=1).astype(jnp.int32)


def _route(lg_t, batch, cap, tile):
    e, m = lg_t.shape
    l = m // batch
    nt = l // tile
    est = 16 * _nbytes((e, l), F32)
    return pl.pallas_call(
        functools.partial(_route_kernel, cap=cap, tile=tile),
        grid=(batch,),
        in_specs=[pl.BlockSpec((e, l), lambda b: (0, b))],
        out_specs=[pl.BlockSpec((1, e, l), lambda b: (b, 0, 0))] * 2
        + [pl.BlockSpec((1, e, nt + 1), lambda b: (b, 0, 0))],
        out_shape=[jax.ShapeDtypeStruct((batch, e, l), jnp.int32),
                   jax.ShapeDtypeStruct((batch, e, l), F32),
                   jax.ShapeDtypeStruct((batch, e, nt + 1), jnp.int32)],
        compiler_params=_cparams(("parallel",), est),
        name="route",
    )(lg_t)


def _slots_kernel(posm_ref, aff_ref, *rest, cap):
    idx_ref, gs_ref = rest[-2], rest[-1]
    pm = posm_ref[0, 0]
    l = pm.shape[1]
    slot = lax.broadcasted_iota(jnp.int32, (cap, l), 0)
    tok = lax.broadcasted_iota(jnp.int32, (1, l), 1).astype(F32)
    hit = pm == slot
    idx_ref[0, 0] = jnp.sum(jnp.where(hit, tok, 0.0), axis=1, keepdims=True).astype(jnp.int32)
    gs_ref[0] = jnp.sum(jnp.where(hit, aff_ref[0, 0], 0.0), axis=1, keepdims=True)


def _slots(posm, aff, cap, n_rows, row_off, gs_all=None):
    batch, e, l = posm.shape
    blk = row_off // cap
    in_specs = [pl.BlockSpec((1, 1, 1, l), lambda b, ee: (b, ee, 0, 0))] * 2
    args = [posm.reshape(batch, e, 1, l), aff.reshape(batch, e, 1, l)]
    aliases = {}
    if gs_all is not None:
        in_specs.append(pl.BlockSpec(memory_space=pl.ANY))
        args.append(gs_all)
        aliases = {2: 1}
    est = 6 * _nbytes((cap, l), F32) + 4 * _nbytes((cap, V7X_LANES), F32)
    return pl.pallas_call(
        functools.partial(_slots_kernel, cap=cap),
        grid=(batch, e),
        in_specs=in_specs,
        out_specs=[pl.BlockSpec((1, 1, cap, 1), lambda b, ee: (b, ee, 0, 0)),
                   pl.BlockSpec((1, cap, 1), lambda b, ee: (ee, blk + b, 0))],
        out_shape=[jax.ShapeDtypeStruct((batch, e, cap, 1), jnp.int32),
                   jax.ShapeDtypeStruct((e, n_rows, 1), F32)],
        input_output_aliases=aliases,
        compiler_params=_cparams(("parallel", "parallel"), est),
        name="moe_slots",
    )(*args)


GATHER_ISSUE_UNROLL = 8


def _gather_kernel(idx_ref, hp_hbm, *rest, cap, seq, n_exp):
    o_ref, sem = rest[-2], rest[-1]
    b, ee = pl.program_id(0), pl.program_id(1)
    base = (b * n_exp + ee) * cap
    row0 = b * seq

    def issue(c, carry):
        for k in range(GATHER_ISSUE_UNROLL):
            s = c * GATHER_ISSUE_UNROLL + k
            pltpu.make_async_copy(hp_hbm.at[pl.ds(row0 + idx_ref[base + s], 1)],
                                  o_ref.at[0, pl.ds(s, 1)], sem).start()
        return carry

    lax.fori_loop(0, cap // GATHER_ISSUE_UNROLL, issue, 0)
    pltpu.make_async_copy(hp_hbm.at[pl.ds(0, cap)], o_ref.at[0], sem).wait()


def _gather(idx, hp, batch, cap, n_rows, row_off, xg_all=None):
    e = idx.shape[1]
    m, dh = hp.shape
    seq = m // batch
    blk = row_off // cap
    in_specs = [pl.BlockSpec(memory_space=pl.ANY)]
    args = [idx.reshape(-1), hp]
    aliases = {}
    if xg_all is not None:
        in_specs.append(pl.BlockSpec(memory_space=pl.ANY))
        args.append(xg_all)
        aliases = {2: 0}
    est = 2 * _nbytes((cap, dh), jnp.uint32)
    return pl.pallas_call(
        functools.partial(_gather_kernel, cap=cap, seq=seq, n_exp=e),
        grid_spec=pltpu.PrefetchScalarGridSpec(
            num_scalar_prefetch=1,
            grid=(batch, e),
            in_specs=in_specs,
            out_specs=pl.BlockSpec((1, cap, dh), lambda b, ee, ix: (ee, blk + b, 0)),
            scratch_shapes=[pltpu.SemaphoreType.DMA(())],
        ),
        out_shape=jax.ShapeDtypeStruct((e, n_rows, dh), jnp.uint32),
        input_output_aliases=aliases,
        compiler_params=_cparams(("arbitrary", "arbitrary"), est),
        name="moe_gather",
    )(*args)


EXPERT_ROW_BLOCKS = 2


def _up_kernel(xg_ref, wg_ref, wu_ref, o_ref, wg_bf, wu_bf):
    @pl.when(pl.program_id(2) == 0)
    def _():
        wg_bf[...] = wg_ref[0, 0].astype(BF16)
        wu_bf[...] = wu_ref[0, 0].astype(BF16)

    x_lo, x_hi = _unpack_bf16_pairs(xg_ref[0])
    half = x_lo.shape[1]
    g = (jnp.dot(x_lo, wg_bf[0:half, :], preferred_element_type=F32)
         + jnp.dot(x_hi, wg_bf[half:, :], preferred_element_type=F32))
    u = (jnp.dot(x_lo, wu_bf[0:half, :], preferred_element_type=F32)
         + jnp.dot(x_hi, wu_bf[half:, :], preferred_element_type=F32))
    o_ref[0] = (g * jax.nn.sigmoid(g) * u).astype(o_ref.dtype)


def _expert_up(xg, w_gate, w_up, layer, tf=256):
    e, r, dh = xg.shape
    d = 2 * dh
    f = w_gate.shape[3]
    tf = min(tf, f)
    tr = r // EXPERT_ROW_BLOCKS
    est = (4 * _nbytes((d, tf), F32) + 3 * _nbytes((d, tf), BF16) + 2 * _nbytes((tr, d), BF16)
           + 6 * _nbytes((tr, tf), F32))
    return pl.pallas_call(
        _up_kernel,
        grid=(e, f // tf, EXPERT_ROW_BLOCKS),
        in_specs=[
            pl.BlockSpec((1, tr, dh), lambda ee, fj, rr: (ee, rr, 0)),
            pl.BlockSpec((1, 1, d, tf), lambda ee, fj, rr: (layer, ee, 0, fj)),
            pl.BlockSpec((1, 1, d, tf), lambda ee, fj, rr: (layer, ee, 0, fj)),
        ],
        out_specs=pl.BlockSpec((1, tr, tf), lambda ee, fj, rr: (ee, rr, fj)),
        out_shape=jax.ShapeDtypeStruct((e, r, f), BF16),
        scratch_shapes=[pltpu.VMEM((d, tf), BF16), pltpu.VMEM((d, tf), BF16)],
        compiler_params=_cparams(("parallel", "parallel", "arbitrary"), est),
        name="moe_up",
    )(xg, w_gate, w_up)


def _down_kernel(h_ref, wd_ref, gs_ref, o_ref, wd_bf):
    @pl.when(pl.program_id(2) == 0)
    def _():
        wd_bf[...] = wd_ref[0, 0].astype(BF16)

    y = jnp.dot(h_ref[0], wd_bf[...], preferred_element_type=F32)
    o_ref[0] = (y * gs_ref[0]).astype(o_ref.dtype)


def _expert_down(hid, w_down, gslot, layer, td=1024):
    e, r, f = hid.shape
    d = w_down.shape[3]
    td = min(td, d)
    tr = r // EXPERT_ROW_BLOCKS
    est = (2 * _nbytes((f, td), F32) + 2 * _nbytes((f, td), BF16) + 2 * _nbytes((tr, f), BF16)
           + 4 * _nbytes((tr, td), F32) + 2 * _nbytes((tr, V7X_LANES), F32))
    return pl.pallas_call(
        _down_kernel,
        grid=(e, d // td, EXPERT_ROW_BLOCKS),
        in_specs=[
            pl.BlockSpec((1, tr, f), lambda ee, dj, rr: (ee, rr, 0)),
            pl.BlockSpec((1, 1, f, td), lambda ee, dj, rr: (layer, ee, 0, dj)),
            pl.BlockSpec((1, tr, 1), lambda ee, dj, rr: (ee, rr, 0)),
        ],
        out_specs=pl.BlockSpec((1, tr, td), lambda ee, dj, rr: (ee, rr, dj)),
        out_shape=jax.ShapeDtypeStruct((e, r, d), BF16),
        scratch_shapes=[pltpu.VMEM((f, td), BF16)],
        compiler_params=_cparams(("parallel", "parallel", "arbitrary"), est),
        name="moe_down",
    )(hid, w_down, gslot)


COMBINE_TILE = 256
DMA_ROW_ALIGN = 16


def combine_window(cap, seq, tile):
    return min(cap, max(DMA_ROW_ALIGN, 2 * tile * cap // seq))


def _combine_kernel(tbl_ref, pt_ref, y_hbm, x_ref, g_ref, *rest, n_exp, cap, win, nt, row_off,
                    n_steps, emit_x, post_norm, post_mod, eps):
    stage, extra, acc_scr, sems, xsem = rest[-5:]
    n_post_in = (1 + 2 * post_mod) if post_norm else 0
    post_in = rest[:n_post_in]
    outs = rest[n_post_in:-5]
    n = pl.program_id(0) * nt + pl.program_id(1)
    cur = n % 2
    nxt_step = jnp.minimum(n + 1, n_steps - 1)

    def window_starts(step):
        b, i = step // nt, step % nt
        t0 = (b * (nt + 1) + i) * n_exp
        out = []
        for ee in range(n_exp):
            s0a = jnp.minimum((tbl_ref[t0 + ee] // DMA_ROW_ALIGN) * DMA_ROW_ALIGN, cap - win)
            out.append(pl.multiple_of(s0a, DMA_ROW_ALIGN))
        return row_off + b * cap, out, t0

    def issue_windows(step, buf):
        row_base, starts, _ = window_starts(step)
        for ee in range(n_exp):
            pltpu.make_async_copy(y_hbm.at[ee, pl.ds(row_base + starts[ee], win), :],
                                  stage.at[buf, pl.ds(ee * win, win), :], sems.at[buf]).start()

    def wait_windows(buf):
        pltpu.make_async_copy(y_hbm.at[0, pl.ds(0, n_exp * win), :], stage.at[buf],
                              sems.at[buf]).wait()

    @pl.when(n == 0)
    def _():
        issue_windows(0, 0)

    issue_windows(nxt_step, 1 - cur)
    row_base, starts, t0 = window_starts(n)
    pt = pt_ref[...]
    lane = lax.broadcasted_iota(jnp.int32, (pt.shape[0], win), 1)
    p = jnp.concatenate(
        [jnp.where(pt[:, ee:ee + 1] - starts[ee] == lane, 1.0, 0.0).astype(BF16)
         for ee in range(n_exp)], axis=1)
    wait_windows(cur)
    acc_scr[...] = jnp.dot(p, stage[cur], preferred_element_type=F32)

    for ee in range(n_exp):
        first = starts[ee] + win
        n_extra = jnp.maximum(0, (tbl_ref[t0 + n_exp + ee] - first + win - 1) // win)

        def extra_window(w, carry, ee=ee, first=first):
            lo = first + w * win
            src0 = pl.multiple_of(jnp.minimum(lo, cap - win), DMA_ROW_ALIGN)
            cp = pltpu.make_async_copy(y_hbm.at[ee, pl.ds(row_base + src0, win), :], extra, xsem)
            cp.start()
            cp.wait()
            col = pt[:, ee:ee + 1]
            hit = jnp.where(col >= lo, col - src0, -1) == lane
            acc_scr[...] += jnp.dot(jnp.where(hit, 1.0, 0.0).astype(BF16), extra[...],
                                    preferred_element_type=F32)
            return carry

        lax.fori_loop(0, n_extra, extra_window, 0)

    xn = x_ref[...] + g_ref[0] * acc_scr[...]
    if emit_x:
        outs[0][...] = xn
    if post_norm:
        shift, scale = (post_in[1][0], post_in[2][0]) if post_mod else (None, None)
        h_ref = outs[-1]
        h_ref[...] = _rms_modulate(xn, post_in[0][...], shift, scale, eps).astype(h_ref.dtype)

    @pl.when(n == n_steps - 1)
    def _():
        wait_windows(1 - cur)


def _combine(posm_t, bounds, y, x2, gate, batch, cap, row_off, post=None, emit_x=True):
    m, d = x2.shape
    l = m // batch
    e = posm_t.shape[1]
    tm = min(COMBINE_TILE, l)
    nt = l // tm
    win = combine_window(cap, l, tm)
    assert cap % DMA_ROW_ALIGN == 0 and win % DMA_ROW_ALIGN == 0 and bounds.shape == (batch, e, nt + 1)
    tbl = bounds.transpose(0, 2, 1).reshape(-1)
    est = (2 * _nbytes((e * win, d), BF16) + _nbytes((win, d), BF16) + 9 * _nbytes((tm, d), F32)
           + 2 * _nbytes((tm, e * win), BF16) + 2 * _nbytes((tm, V7X_LANES), F32))
    row_spec = pl.BlockSpec((tm, d), lambda b, i, t: (b * nt + i, 0))
    vec_spec = pl.BlockSpec((1, 1, d), lambda b, i, t: (b, 0, 0))
    in_specs = [pl.BlockSpec((tm, e), lambda b, i, t: (b * nt + i, 0)),
                pl.BlockSpec(memory_space=pl.ANY), row_spec, vec_spec]
    args = [tbl, posm_t, y, x2, gate]
    out_specs, out_shape = [], []
    if emit_x:
        out_specs.append(row_spec)
        out_shape.append(jax.ShapeDtypeStruct((m, d), F32))
    post_mod = False
    if post is not None:
        p_gain, p_shift, p_scale, p_dtype = post
        post_mod = p_shift is not None
        in_specs.append(pl.BlockSpec((1, d), lambda b, i, t: (0, 0)))
        args.append(p_gain.reshape(1, d))
        if post_mod:
            in_specs += [vec_spec, vec_spec]
            args += [p_shift, p_scale]
        out_specs.append(row_spec)
        out_shape.append(jax.ShapeDtypeStruct((m, d), p_dtype))
        est += 2 * _nbytes((tm, d), p_dtype)
    return pl.pallas_call(
        functools.partial(_combine_kernel, n_exp=e, cap=cap, win=win, nt=nt, row_off=row_off,
                          n_steps=batch * nt, emit_x=emit_x, post_norm=post is not None,
                          post_mod=post_mod, eps=OP["rms_eps"]),
        grid_spec=pltpu.PrefetchScalarGridSpec(
            num_scalar_prefetch=1,
            grid=(batch, nt),
            in_specs=in_specs,
            out_specs=out_specs,
            scratch_shapes=[pltpu.VMEM((2, e * win, d), BF16), pltpu.VMEM((win, d), BF16),
                            pltpu.VMEM((tm, d), F32), pltpu.SemaphoreType.DMA((2,)),
                            pltpu.SemaphoreType.DMA(())],
        ),
        out_shape=out_shape,
        input_output_aliases={3: 0} if emit_x else {},
        compiler_params=_cparams(("arbitrary", "arbitrary"), est),
        name="moe_combine",
    )(*args)


def _sigmoid(z):
    return jax.nn.sigmoid(z)


def _in_proj(h2, w_in, layer, n_cols):
    return _mm([(h2, w_in, layer, 0)], [], lambda accs, ex: accs[0], n_cols, F32, 1024,
               math.gcd(512, n_cols), name="in_proj")


def _mixer_out(x2, a2, px, g_mix, wts, layer, dfts, batch, rpb):
    w_glu, w_s5o, w_fto, w_out = wts
    wc, cl, sl = dfts
    d = x2.shape[1]
    d_s5 = a2.shape[1]
    d_ft = w_fto.shape[1]
    xc, xs = _chan_dft(px, d_s5, d_ft, wc)
    yf = _seq_dft(xc, xs, cl, sl, batch)
    glu = _mm([(a2, w_glu, layer, 0)], [("tile", a2, 0)],
              lambda accs, ex: ex[0] * _sigmoid(accs[0]),
              d_s5, BF16, 1024, 512, name="glu")
    tn = math.gcd(512, d_s5 + d_ft, d)
    g0 = (d_s5 + d_ft) // tn
    merged = _mm([(glu, w_s5o, layer, 0), (yf, w_fto, layer, 0)],
                 [("tile", px, g0), ("tile", px, g0 + d // tn)],
                 lambda accs, ex: _sigmoid(ex[0]) * accs[0] + _sigmoid(ex[1]) * accs[1],
                 d, BF16, 1024, tn, name="merge")
    return _mm([(merged, w_out, layer, 0)], [("tile", x2, 0), ("row", g_mix, 0)],
               lambda accs, ex: ex[0] + ex[1] * accs[0],
               d, F32, 1024, tn, rows_per_batch=rpb, alias_extra=0, name="out_proj")


def _moe(streams, w_gate, w_up, w_down, layer):
    e = streams[0][4].shape[0]
    caps = [OP["capacity_factor"] * (s[0].shape[0] // s[1]) // e for s in streams]
    offs, n_rows = [], 0
    for s, cap in zip(streams, caps):
        assert n_rows % cap == 0
        offs.append(n_rows)
        n_rows += s[1] * cap
    xg = gslot = None
    if len(streams) > 1:
        xg = jnp.zeros((e, n_rows, streams[0][2].shape[1]), jnp.uint32)
        gslot = jnp.zeros((e, n_rows, 1), F32)
    routed = []
    for (x2, batch, hp, _, lg_t, _, _), cap, off in zip(streams, caps, offs):
        tile = min(COMBINE_TILE, x2.shape[0] // batch)
        posm, aff, bounds = _route(lg_t, batch, cap, tile)
        idx, gslot = _slots(posm, aff, cap, n_rows, off, gslot)
        xg = _gather(idx, hp, batch, cap, n_rows, off, xg)
        routed.append((posm, bounds))
    y = _expert_down(_expert_up(xg, w_gate, w_up, layer), w_down, gslot, layer)
    outs = []
    for (x2, batch, _, gate, _, post, emit_x), cap, off, (posm, bounds) in zip(
            streams, caps, offs, routed):
        posm_t = posm.transpose(0, 2, 1).reshape(x2.shape[0], e)
        outs.append(_combine(posm_t, bounds, y, x2, gate, batch, cap, off, post, emit_x))
    return outs


def _grid_posembed(n_tok, d):
    gw = OP["grid_w"]
    rows = n_tok // gw
    quarter = d // 4
    inv_freq = OP["pos_base"] ** (-jnp.arange(quarter, dtype=F32) / quarter)
    ang_r = jnp.arange(rows, dtype=F32)[:, None] * inv_freq
    ang_c = jnp.arange(gw, dtype=F32)[:, None] * inv_freq
    shape = (rows, gw, quarter)
    parts = [jnp.broadcast_to(jnp.sin(ang_r)[:, None, :], shape),
             jnp.broadcast_to(jnp.cos(ang_r)[:, None, :], shape),
             jnp.broadcast_to(jnp.sin(ang_c)[None, :, :], shape),
             jnp.broadcast_to(jnp.cos(ang_c)[None, :, :], shape)]
    return jnp.concatenate(parts, axis=-1).reshape(n_tok, d)


def kernel(x, c, ctx, c_ctx, ada_w, ada_b, norm_mix_g, norm_ffn_g, w_in, s5_lam_re, s5_lam_im,
           s5_log_dt, s5_b_re, s5_b_im, s5_c_re, s5_c_im, s5_d, w_glu, w_s5_out, w_ft_out, w_out,
           w_router, w_gate, w_up, w_down, norm_final_g):
    batch, seq, d = x.shape
    ctx_len = ctx.shape[1]
    depth = ada_w.shape[0]
    d_s5 = w_glu.shape[1]
    d_ft = w_ft_out.shape[1]
    n_mod = OP["n_mod"]
    m_x, m_c = batch * seq, batch * ctx_len

    x2 = x.reshape(m_x, d)
    pos = _grid_posembed(seq, d).astype(x.dtype)
    c2 = ctx.reshape(m_c, d)

    rows = -(-(batch + 1) // 8) * 8
    c8 = jnp.zeros((rows, d), F32).at[:batch].set(c).at[batch].set(c_ctx)
    mod = _adaln(c8, ada_w, ada_b)

    gd = d_ft // OP["ft_groups"]
    wc_c, wc_s = _dft_mats(gd)
    wc = jnp.concatenate([wc_c, wc_s], axis=1)
    dft_x = (wc,) + _dft_mats(seq)
    dft_c = (wc,) + _dft_mats(ctx_len)

    bd, wc_s5, oc_s5, at = jax.vmap(lambda *p: _s5_operators(*p, S5_CHUNK))(
        s5_lam_re, s5_lam_im, s5_log_dt, s5_b_re, s5_b_im, s5_c_re, s5_c_im, s5_d)
    s5_ops = (bd, wc_s5, oc_s5) + _s5_masks(S5_CHUNK) + (at,)
    d_in = w_in.shape[2]
    mix_w = (w_glu, w_s5_out, w_ft_out, w_out)

    mods_x = [[mod[i, :batch, k * d:(k + 1) * d].reshape(batch, 1, d) for k in range(n_mod)]
              for i in range(depth)]
    mods_c = [[jnp.broadcast_to(mod[i, batch:batch + 1, k * d:(k + 1) * d].reshape(1, 1, d),
                                (batch, 1, d)) for k in range(n_mod)] for i in range(depth)]

    hx = hc = None
    for i in range(depth):
        last = i == depth - 1
        mx, mc = mods_x[i], mods_c[i]
        wr_t = w_router[i].T.astype(BF16)
        if i == 0:
            hc = _norm(c2, norm_mix_g[i], mc[0], mc[1], rows_per_batch=ctx_len)
            hx, x2 = _norm(x2, norm_mix_g[i], mx[0], mx[1], rows_per_batch=seq, pos=pos)
        pc = _in_proj(hc, w_in, i, d_s5 if last else d_in)
        px = _in_proj(hx, w_in, i, d_in)
        ac, ax = _s5(pc, px, d_s5, batch, s5_ops, i)
        x2 = _mixer_out(x2, ax, px, mx[2], mix_w, i, dft_x, batch, seq)
        lgx, hpx = _norm(x2, norm_ffn_g[i], mx[3], mx[4], rows_per_batch=seq, wr_t=wr_t,
                         emit_h=False, pack=True)
        if last:
            post_x, post_c = (norm_final_g, None, None, x.dtype), None
        else:
            post_x = (norm_mix_g[i + 1], mods_x[i + 1][0], mods_x[i + 1][1], BF16)
            post_c = (norm_mix_g[i + 1], mods_c[i + 1][0], mods_c[i + 1][1], BF16)
        streams = [(x2, batch, hpx, mx[5], lgx, post_x, not last)]
        if not last:
            c2 = _mixer_out(c2, ac, pc, mc[2], mix_w, i, dft_c, batch, ctx_len)
            lgc, hpc = _norm(c2, norm_ffn_g[i], mc[3], mc[4], rows_per_batch=ctx_len, wr_t=wr_t,
                             emit_h=False, pack=True)
            streams.append((c2, batch, hpc, mc[5], lgc, post_c, True))
        outs = _moe(streams, w_gate, w_up, w_down, i)
        if last:
            return outs[0][0].reshape(batch, seq, d)
        x2, hx = outs[0]
        c2, hc = outs[1]
```

```python
import functools
import math

import jax
import jax.numpy as jnp
from jax import lax
from jax.experimental import pallas as pl
from jax.experimental.pallas import tpu as pltpu

BF16 = jnp.bfloat16
F32 = jnp.float32
HI = lax.Precision.HIGHEST

OP = dict(
    s5_group_dim=16,
    s5_state=64,
    ft_groups=4,
    capacity_factor=2,
    n_mod=6,
    rms_eps=1e-6,
    pos_base=10000.0,
    grid_w=64,
    lambda_re_max=-1e-4,
)

V7X_VMEM_BYTES = 64 * 1024 * 1024
V7X_LANES = 128
S5_TILE_GROUPS = V7X_LANES // 16
S5_CHUNK = 16
CUMSUM_BLOCK = 256


def _cparams(sem, vmem_est):
    limit = int(min(max(vmem_est * 5 // 4 + (4 << 20), 32 << 20), V7X_VMEM_BYTES - (6 << 20)))
    return pltpu.CompilerParams(dimension_semantics=sem, vmem_limit_bytes=limit)


def _nbytes(shape, dtype):
    return math.prod(shape) * jnp.dtype(dtype).itemsize


def _adaln_kernel(c_ref, w_ref, b_ref, o_ref):
    cv = c_ref[...]
    a = (cv * jax.nn.sigmoid(cv)).astype(BF16)
    o_ref[0] = jnp.dot(a, w_ref[0].astype(BF16), preferred_element_type=F32) + b_ref[0]


def _adaln(c8, ada_w, ada_b, tn=512):
    depth, d, n6 = ada_w.shape
    rows = c8.shape[0]
    tn = min(tn, n6)
    est = 2 * _nbytes((d, tn), F32) + 4 * _nbytes((rows, tn), F32) + _nbytes((rows, d), F32) * 2
    return pl.pallas_call(
        _adaln_kernel,
        grid=(depth, n6 // tn),
        in_specs=[
            pl.BlockSpec((rows, d), lambda l, j: (0, 0)),
            pl.BlockSpec((1, d, tn), lambda l, j: (l, 0, j)),
            pl.BlockSpec((1, 1, tn), lambda l, j: (l, 0, j)),
        ],
        out_specs=pl.BlockSpec((1, rows, tn), lambda l, j: (l, 0, j)),
        out_shape=jax.ShapeDtypeStruct((depth, rows, n6), F32),
        compiler_params=_cparams(("parallel", "parallel"), est),
        name="adaln",
    )(c8, ada_w, ada_b.reshape(depth, 1, n6))


def _rms_modulate(xf, g, shift, scale, eps):
    ms = jnp.mean(xf * xf, axis=-1, keepdims=True)
    y = xf * lax.rsqrt(ms + eps) * g
    if shift is not None:
        y = y * (1.0 + scale) + shift
    return y


def _pack_bf16_pairs(y):
    half = y.shape[1] // 2
    lo = pltpu.bitcast(y[:, :half].astype(BF16).astype(F32), jnp.uint32) >> 16
    hi = pltpu.bitcast(y[:, half:].astype(BF16).astype(F32), jnp.uint32) & jnp.uint32(0xFFFF0000)
    return hi | lo


def _unpack_bf16_pairs(p):
    lo = pltpu.bitcast(p << 16, F32).astype(BF16)
    hi = pltpu.bitcast(p & jnp.uint32(0xFFFF0000), F32).astype(BF16)
    return lo, hi


def _norm_kernel(*refs, eps, modulate, router, add_pos, emit_h, pack):
    x_ref, g_ref = refs[0], refs[1]
    k = 2
    xf = x_ref[...]
    if add_pos:
        pr, pc = refs[k][0], refs[k + 1][...]
        k += 2
        gw = pc.shape[0]
        pos = jnp.concatenate(
            [jnp.concatenate([jnp.broadcast_to(pr[q:q + 1], pc.shape), pc], axis=1)
             for q in range(xf.shape[0] // gw)], axis=0)
        xf = xf + pos
        refs[-1][...] = xf
    shift = scale = None
    if modulate:
        shift, scale = refs[k][0], refs[k + 1][0]
        k += 2
    y = _rms_modulate(xf, g_ref[...], shift, scale, eps)
    if router:
        wr_ref = refs[k]
        k += 1
    if emit_h:
        refs[k][...] = y.astype(refs[k].dtype)
        k += 1
    if router:
        refs[k][...] = lax.dot_general(
            wr_ref[...], y.astype(BF16), (((1,), (1,)), ((), ())), preferred_element_type=F32)
        k += 1
    if pack:
        refs[k][...] = _pack_bf16_pairs(y)


def _norm(x2, g, shift=None, scale=None, rows_per_batch=None, wr_t=None, out_dtype=BF16, tm=256,
          pos=None, emit_h=True, pack=False):
    m, d = x2.shape
    modulate = shift is not None
    router = wr_t is not None
    add_pos = pos is not None
    rpb = rows_per_batch if rows_per_batch is not None else m
    tm = min(tm, rpb)
    tpb = rpb // tm
    in_specs = [pl.BlockSpec((tm, d), lambda i: (i, 0)), pl.BlockSpec((1, d), lambda i: (0, 0))]
    args = [x2, g.reshape(1, d)]
    if add_pos:
        pos_rows, pos_cols = pos
        gw = pos_cols.shape[0]
        assert tm % gw == 0 and rpb % tm == 0
        in_specs.append(pl.BlockSpec((1, tm // gw, d // 2), lambda i: (i % tpb, 0, 0)))
        in_specs.append(pl.BlockSpec((gw, d // 2), lambda i: (0, 0)))
        args += [pos_rows.reshape(rpb // tm, tm // gw, d // 2), pos_cols]
    if modulate:
        in_specs += [pl.BlockSpec((1, 1, d), lambda i: (i // tpb, 0, 0))] * 2
        args += [shift, scale]
    out_specs, out_shape = [], []
    if emit_h:
        out_specs.append(pl.BlockSpec((tm, d), lambda i: (i, 0)))
        out_shape.append(jax.ShapeDtypeStruct((m, d), out_dtype))
    if router:
        e = wr_t.shape[0]
        in_specs.append(pl.BlockSpec((e, d), lambda i: (0, 0)))
        args.append(wr_t)
        out_specs.append(pl.BlockSpec((e, tm), lambda i: (0, i)))
        out_shape.append(jax.ShapeDtypeStruct((e, m), F32))
    est = 2 * _nbytes((tm, d), F32) + 2 * _nbytes((tm, d), out_dtype) + 3 * _nbytes((tm, d), F32)
    if pack:
        out_specs.append(pl.BlockSpec((tm, d // 2), lambda i: (i, 0)))
        out_shape.append(jax.ShapeDtypeStruct((m, d // 2), jnp.uint32))
        est += 4 * _nbytes((tm, d // 2), F32)
    if add_pos:
        out_specs.append(pl.BlockSpec((tm, d), lambda i: (i, 0)))
        out_shape.append(jax.ShapeDtypeStruct((m, d), F32))
        est += 4 * _nbytes((tm, d), F32)
    res = pl.pallas_call(
        functools.partial(_norm_kernel, eps=OP["rms_eps"], modulate=modulate, router=router,
                          add_pos=add_pos, emit_h=emit_h, pack=pack),
        grid=(m // tm,),
        in_specs=in_specs,
        out_specs=out_specs,
        out_shape=out_shape,
        compiler_params=_cparams(("parallel",), est),
        name="rmsnorm",
    )(*args)
    return res if len(res) > 1 else res[0]


def _mm_kernel(*refs, n_pairs, kinds, epilogue):
    n_in = 2 * n_pairs + len(kinds)
    o_ref = refs[n_in]
    w_bf = refs[n_in + 1:]

    @pl.when(pl.program_id(1) == 0)
    def _():
        for k in range(n_pairs):
            w_bf[k][...] = refs[2 * k + 1][...].astype(BF16)

    accs = [jnp.dot(refs[2 * k][...].astype(BF16), w_bf[k][...], preferred_element_type=F32)
            for k in range(n_pairs)]
    ex = []
    for k, kind in enumerate(kinds):
        r = refs[2 * n_pairs + k]
        ex.append(r[0] if kind == "row" else r[...])
    o_ref[...] = epilogue(accs, ex).astype(o_ref.dtype)


def _mm(pairs, extras, epilogue, n_out, out_dtype, tm, tn, rows_per_batch=None, alias_extra=None,
        name="mm"):
    m = pairs[0][0].shape[0]
    rpb = rows_per_batch if rows_per_batch is not None else m
    tm = min(tm, rpb)
    tn = min(tn, n_out)
    tpb = rpb // tm
    in_specs, args, scratch = [], [], []
    est = 2 * _nbytes((tm, tn), out_dtype) + 2 * _nbytes((tm, tn), F32) * max(1, len(pairs))
    for a, w, layer, off in pairs:
        kdim = a.shape[1]
        in_specs.append(pl.BlockSpec((tm, kdim), lambda j, i: (i, 0)))
        in_specs.append(pl.BlockSpec((None, kdim, tn),
                                     lambda j, i, off=off, layer=layer: (layer, 0, j + off)))
        scratch.append(pltpu.VMEM((kdim, tn), BF16))
        args += [a, w]
        est += (2 * _nbytes((tm, kdim), a.dtype) + 2 * _nbytes((kdim, tn), w.dtype)
                + _nbytes((kdim, tn), BF16))
        if a.dtype != BF16:
            est += _nbytes((tm, kdim), BF16)
    kinds = []
    for kind, arr, off in extras:
        kinds.append(kind)
        if kind == "tile":
            in_specs.append(pl.BlockSpec((tm, tn), lambda j, i, off=off: (i, j + off)))
            est += 2 * _nbytes((tm, tn), arr.dtype)
        else:
            in_specs.append(pl.BlockSpec((1, 1, tn), lambda j, i, off=off: (i // tpb, 0, j + off)))
        args.append(arr)
    aliases = {}
    if alias_extra is not None:
        aliases = {2 * len(pairs) + alias_extra: 0}
    return pl.pallas_call(
        functools.partial(_mm_kernel, n_pairs=len(pairs), kinds=tuple(kinds), epilogue=epilogue),
        grid=(n_out // tn, m // tm),
        in_specs=in_specs,
        out_specs=pl.BlockSpec((tm, tn), lambda j, i: (i, j)),
        out_shape=jax.ShapeDtypeStruct((m, n_out), out_dtype),
        scratch_shapes=scratch,
        input_output_aliases=aliases,
        compiler_params=_cparams(("parallel", "arbitrary"), est),
        name=name,
    )(*args)


def _s5_operators(lam_re, lam_im, log_dt, b_re, b_im, c_re, c_im, d_skip, t):
    n = OP["s5_group_dim"]
    lr = jnp.minimum(lam_re.astype(F32), OP["lambda_re_max"])
    li = lam_im.astype(F32)
    dt = jnp.exp(log_dt.astype(F32))[..., None]
    g, p = lr.shape[1], lr.shape[2]
    gt = S5_TILE_GROUPS
    j = g // gt
    mag = jnp.exp(lr * dt)
    a_re = mag * jnp.cos(li * dt)
    a_im = mag * jnp.sin(li * dt)
    num_re, num_im = a_re - 1.0, a_im
    den = lr * lr + li * li
    f_re = (num_re * lr + num_im * li) / den
    f_im = (num_im * lr - num_re * li) / den
    br, bi = b_re.astype(F32), b_im.astype(F32)
    bb_re = f_re[..., None] * br - f_im[..., None] * bi
    bb_im = f_re[..., None] * bi + f_im[..., None] * br
    cr, ci = c_re.astype(F32), c_im.astype(F32)

    k = jnp.arange(t + 1, dtype=F32)[:, None, None, None]
    pw_mag = jnp.exp(lr[None] * dt[None] * k)
    pw_re = pw_mag * jnp.cos(li[None] * dt[None] * k)
    pw_im = pw_mag * jnp.sin(li[None] * dt[None] * k)

    ca_re = cr[None] * pw_re[:t, :, :, None, :] - ci[None] * pw_im[:t, :, :, None, :]
    ca_im = cr[None] * pw_im[:t, :, :, None, :] + ci[None] * pw_re[:t, :, :, None, :]
    ca = jnp.concatenate([ca_re, -ca_im], axis=-1)
    ca = ca.transpose(1, 2, 0, 3, 4).reshape(2, g, t * n, 2 * p)
    bbs = jnp.concatenate([bb_re, bb_im], axis=2)
    kk = jnp.einsum("dgxp,dgpm->dgxm", ca, bbs, precision=HI)
    kk = kk.reshape(2, g, t, n, n).transpose(2, 0, 1, 3, 4)
    kf, kb = kk[:, 0], kk[:, 1]
    k0 = kf[0] + kb[0] + d_skip.astype(F32).reshape(g, n)[:, :, None] * jnp.eye(n, dtype=F32)
    kall = jnp.concatenate([kb[1:][::-1], k0[None], kf[1:]], axis=0)
    kc = kall.reshape(2 * t - 1, j, gt, n, n).transpose(1, 0, 4, 2, 3)
    kc = kc.reshape(j, 2 * t - 1, n, gt * n).astype(BF16)
    lane = jnp.arange(gt * n)
    same_group = (lane[:, None] // n) == (lane[None, :] // n)
    bd = jnp.where(same_group, jnp.tile(kc, (1, 1, gt, 1)), 0)

    ps_re = jnp.stack([pw_re[:t][::-1, 0], pw_re[:t, 1]], axis=0)
    ps_im = jnp.stack([pw_im[:t][::-1, 0], pw_im[:t, 1]], axis=0)
    w_re = ps_re[..., None] * bb_re[:, None] - ps_im[..., None] * bb_im[:, None]
    w_im = ps_re[..., None] * bb_im[:, None] + ps_im[..., None] * bb_re[:, None]
    w6 = jnp.stack([w_re, w_im], axis=1).reshape(2, 2, t, j, gt, p, n)
    wc = w6.transpose(3, 2, 6, 0, 1, 4, 5).reshape(j, t, n, 4 * gt * p).astype(BF16)

    po_re = jnp.stack([pw_re[1:, 0], pw_re[1:][::-1, 1]], axis=0)
    po_im = jnp.stack([pw_im[1:, 0], pw_im[1:][::-1, 1]], axis=0)
    co_re = cr[:, None] * po_re[:, :, :, None, :] - ci[:, None] * po_im[:, :, :, None, :]
    co_im = cr[:, None] * po_im[:, :, :, None, :] + ci[:, None] * po_re[:, :, :, None, :]
    o6 = jnp.stack([co_re, -co_im], axis=1).reshape(2, 2, t, j, gt, n, p)
    oc = o6.transpose(3, 0, 1, 6, 2, 4, 5).reshape(j, 4, p, t * gt * n).astype(BF16)
    at = jnp.stack([pw_re[t, 0], pw_im[t, 0], pw_re[t, 1], pw_im[t, 1]], axis=0)
    at = at.reshape(4, j, gt * p).transpose(1, 0, 2).reshape(j, 1, 4 * gt * p)
    return bd, wc, oc, at


def _s5_masks(t):
    n, p, gt = OP["s5_group_dim"], OP["s5_state"], S5_TILE_GROUPS
    col_group = (jnp.arange(4 * gt * p) // p) % gt
    mask_w = ((jnp.arange(gt * n)[:, None] // n) == col_group[None, :]).astype(BF16)
    out_group = (jnp.arange(t * gt * n) // n) % gt
    mask_o = ((jnp.arange(gt * p)[:, None] // p) == out_group[None, :]).astype(BF16)
    return mask_w, mask_o


def _gelu_tanh(y):
    return 0.5 * y * (1.0 + jnp.tanh(0.7978845608028654 * (y + 0.044715 * y * y * y)))


def _s5_kernel(uc_ref, ux_ref, bd_ref, wc_ref, oc_ref, mw_ref, mo_ref, at_ref, oc_out, ox_out,
               mt_scr, wst_scr, wout_scr, s_scr, hin_scr, hend_scr, *, c_ctx, c_lat, half, t, batch):
    lanes = ux_ref.shape[1]
    step = pl.program_id(1)

    @pl.when(step == 0)
    def _():
        for s in range(t):
            for k in range(t):
                mt_scr[s * lanes:(s + 1) * lanes, k * lanes:(k + 1) * lanes] = bd_ref[0, k - s + t - 1]
            rep = lanes // wc_ref.shape[2]
            wst_scr[s * lanes:(s + 1) * lanes, :] = jnp.tile(wc_ref[0, s], (rep, 1)) * mw_ref[...]
        rows = mo_ref.shape[0]
        for q in range(oc_ref.shape[1]):
            rep = rows // oc_ref.shape[2]
            wout_scr[q * rows:(q + 1) * rows, :] = jnp.tile(oc_ref[0, q], (rep, 1)) * mo_ref[...]

    at = at_ref[0]
    afr, afi = at[:, 0:half], at[:, half:2 * half]
    abr, abi = at[:, 2 * half:3 * half], at[:, 3 * half:4 * half]

    def run(u_ref, o_ref, n_chunks, h0):
        u = jnp.concatenate([u_ref[pl.ds(s, n_chunks, stride=t), :] for s in range(t)],
                            axis=1).astype(BF16)
        s_scr[0:n_chunks, :] = jnp.dot(u, wst_scr[...], preferred_element_type=F32)
        init = (h0[:, 0:half], h0[:, half:2 * half], h0[:, 2 * half:3 * half],
                h0[:, 3 * half:4 * half])

        def body(i, carry):
            hfr, hfi, hbr, hbi = carry
            cb = n_chunks - 1 - i
            hin_scr[pl.ds(i, 1), 0:half] = hfr
            hin_scr[pl.ds(i, 1), half:2 * half] = hfi
            hin_scr[pl.ds(cb, 1), 2 * half:3 * half] = hbr
            hin_scr[pl.ds(cb, 1), 3 * half:4 * half] = hbi
            sf = s_scr[pl.ds(i, 1), 0:2 * half]
            sb = s_scr[pl.ds(cb, 1), 2 * half:4 * half]
            nfr = afr * hfr - afi * hfi + sf[:, 0:half]
            nfi = afr * hfi + afi * hfr + sf[:, half:2 * half]
            nbr = abr * hbr - abi * hbi + sb[:, 0:half]
            nbi = abr * hbi + abi * hbr + sb[:, half:2 * half]
            return nfr, nfi, nbr, nbi

        end = lax.fori_loop(0, n_chunks, body, init)
        y = (jnp.dot(u, mt_scr[...], preferred_element_type=F32)
             + jnp.dot(hin_scr[0:n_chunks, :].astype(BF16), wout_scr[...],
                       preferred_element_type=F32))
        for k in range(t):
            o_ref[pl.ds(k, n_chunks, stride=t), :] = _gelu_tanh(
                y[:, k * lanes:(k + 1) * lanes]).astype(o_ref.dtype)
        return jnp.concatenate(end, axis=1)

    @pl.when(step < batch)
    def _():
        hend_scr[pl.ds(step, 1), :] = run(uc_ref, oc_out, c_ctx, jnp.zeros((1, 4 * half), F32))

    @pl.when(step >= batch)
    def _():
        run(ux_ref, ox_out, c_lat, hend_scr[pl.ds(step - batch, 1), :])


def _s5(uc2, ux2, ds5, batch, ops, layer):
    bd, wc, oc, mask_w, mask_o, at = ops
    mc, mx = uc2.shape[0], ux2.shape[0]
    lc, lx = mc // batch, mx // batch
    t = S5_CHUNK
    cc, cx = lc // t, lx // t
    j = ds5 // V7X_LANES
    w = t * V7X_LANES
    sw = wc.shape[4]
    compact = sum(_nbytes(a.shape[2:], BF16) for a in (bd, wc, oc)) + _nbytes(mask_w.shape, BF16) \
        + _nbytes(mask_o.shape, BF16)
    est = (3 * _nbytes((w, sw), BF16) + 2 * compact + 4 * _nbytes((lx + lc, V7X_LANES), F32)
           + 2 * _nbytes((cx, sw), F32) + 3 * _nbytes((cx, w), F32) + _nbytes((cx, sw), BF16)
           + 2 * _nbytes((cx, w), BF16))
    last = batch - 1
    ctx_spec = pl.BlockSpec((lc, V7X_LANES), lambda jj, s: (jnp.minimum(s, last), jj))
    lat_spec = pl.BlockSpec((lx, V7X_LANES), lambda jj, s: (jnp.maximum(s - batch, 0), jj))

    def op_spec(a):
        return pl.BlockSpec((None, 1) + a.shape[2:], lambda jj, s: (layer, jj) + (0,) * (a.ndim - 2))

    return pl.pallas_call(
        functools.partial(_s5_kernel, c_ctx=cc, c_lat=cx, half=sw // 4, t=t, batch=batch),
        grid=(j, 2 * batch),
        in_specs=[ctx_spec, lat_spec, op_spec(bd), op_spec(wc), op_spec(oc),
                  pl.BlockSpec(mask_w.shape, lambda jj, s: (0, 0)),
                  pl.BlockSpec(mask_o.shape, lambda jj, s: (0, 0)),
                  op_spec(at)],
        out_specs=[ctx_spec, lat_spec],
        out_shape=[jax.ShapeDtypeStruct((mc, ds5), F32), jax.ShapeDtypeStruct((mx, ds5), F32)],
        scratch_shapes=[pltpu.VMEM((w, w), BF16), pltpu.VMEM((w, sw), BF16), pltpu.VMEM((sw, w), BF16),
                        pltpu.VMEM((cx, sw), F32), pltpu.VMEM((cx, sw), F32),
                        pltpu.VMEM((-(-batch // 8) * 8, sw), F32)],
        compiler_params=_cparams(("arbitrary", "arbitrary"), est),
        name="s5",
    )(uc2, ux2, bd, wc, oc, mask_w, mask_o, at)


def _dft_mats(n):
    hi = n // V7X_LANES
    j = jnp.arange(n, dtype=jnp.int32)[:, None]
    a1 = ((j * jnp.arange(V7X_LANES, dtype=jnp.int32)[None, :]) % n).astype(F32) * (2.0 * math.pi / n)
    a2 = ((j * jnp.arange(hi, dtype=jnp.int32)[None, :]) % hi).astype(F32) * (2.0 * math.pi / hi)
    c1, s1 = jnp.cos(a1)[:, None, :], jnp.sin(a1)[:, None, :]
    c2, s2 = jnp.cos(a2)[:, :, None], jnp.sin(a2)[:, :, None]
    s = 1.0 / math.sqrt(n)
    cos = ((c2 * c1 - s2 * s1) * s).astype(BF16).reshape(n, n)
    sin = ((s2 * c1 + c2 * s1) * s).astype(BF16).reshape(n, n)
    return cos, sin


def _chan_dft_kernel(v_ref, w_ref, xc_ref, xs_ref, *, gd):
    r = jnp.dot(v_ref[...].astype(BF16), w_ref[...], preferred_element_type=F32)
    xc_ref[...] = r[:, :gd].astype(xc_ref.dtype)
    xs_ref[...] = r[:, gd:].astype(xs_ref.dtype)


def _chan_dft(v2, col_off, dft, wc, tm=1024):
    m = v2.shape[0]
    gd = wc.shape[0]
    tm = min(tm, m)
    goff = col_off // gd
    spec = pl.BlockSpec((tm, gd), lambda i, g: (i, g))
    est = (2 * _nbytes((tm, gd), F32) + 5 * _nbytes((tm, gd), BF16) + 2 * _nbytes((gd, 2 * gd), BF16)
           + 2 * _nbytes((tm, 2 * gd), F32))
    return pl.pallas_call(
        functools.partial(_chan_dft_kernel, gd=gd),
        grid=(m // tm, dft // gd),
        in_specs=[pl.BlockSpec((tm, gd), lambda i, g: (i, goff + g)),
                  pl.BlockSpec((gd, 2 * gd), lambda i, g: (0, 0))],
        out_specs=[spec, spec],
        out_shape=[jax.ShapeDtypeStruct((m, dft), BF16)] * 2,
        compiler_params=_cparams(("parallel", "parallel"), est),
        name="chan_dft",
    )(v2, wc)


def _seq_dft_kernel(cl_ref, sl_ref, xc_ref, xs_ref, o_ref):
    y = (jnp.dot(cl_ref[...], xc_ref[0], preferred_element_type=F32)
         - jnp.dot(sl_ref[...], xs_ref[0], preferred_element_type=F32))
    o_ref[0] = y.astype(o_ref.dtype)


def _seq_dft(xc, xs, cl, sl, batch, tm=512, tn=512):
    m, dft = xc.shape
    l = m // batch
    tm = min(tm, l)
    tn = min(tn, dft)
    xc3, xs3 = xc.reshape(batch, l, dft), xs.reshape(batch, l, dft)
    est = 4 * _nbytes((tm, l), BF16) + 4 * _nbytes((l, tn), BF16) + 4 * _nbytes((tm, tn), F32)
    out = pl.pallas_call(
        _seq_dft_kernel,
        grid=(batch, dft // tn, l // tm),
        in_specs=[
            pl.BlockSpec((tm, l), lambda b, jn, i: (i, 0)),
            pl.BlockSpec((tm, l), lambda b, jn, i: (i, 0)),
            pl.BlockSpec((1, l, tn), lambda b, jn, i: (b, 0, jn)),
            pl.BlockSpec((1, l, tn), lambda b, jn, i: (b, 0, jn)),
        ],
        out_specs=pl.BlockSpec((1, tm, tn), lambda b, jn, i: (b, i, jn)),
        out_shape=jax.ShapeDtypeStruct((batch, l, dft), BF16),
        compiler_params=_cparams(("parallel", "parallel", "parallel"), est),
        name="seq_dft",
    )(cl, sl, xc3, xs3)
    return out.reshape(m, dft)


def _prefix_count(mask_f32, tri):
    e, l = mask_f32.shape
    carry = jnp.zeros((e, 1), F32)
    outs = []
    for k in range(l // CUMSUM_BLOCK):
        blk = mask_f32[:, k * CUMSUM_BLOCK:(k + 1) * CUMSUM_BLOCK]
        outs.append(jnp.dot(blk.astype(BF16), tri, preferred_element_type=F32) + carry)
        carry = carry + jnp.sum(blk, axis=1, keepdims=True)
    return outs[0] if len(outs) == 1 else jnp.concatenate(outs, axis=1)


def _route_kernel(lg_ref, posm_ref, aff_ref, bnd_ref, *, cap, tile):
    lg = lg_ref[...]
    mx = jnp.max(lg, axis=0, keepdims=True)
    ex = jnp.exp(lg - mx)
    aff = ex / jnp.sum(ex, axis=0, keepdims=True)
    bits = pltpu.bitcast(aff, jnp.int32)
    e = lg.shape[0]
    v = jnp.zeros((e, 1), jnp.int32)
    for bit in range(30, -1, -1):
        cand = v | (1 << bit)
        cnt = jnp.sum(jnp.where(bits >= cand, 1.0, 0.0), axis=1, keepdims=True)
        v = jnp.where(cnt >= cap, cand, v)
    gt = bits > v
    eq = bits == v
    n_gt = jnp.sum(gt.astype(F32), axis=1, keepdims=True)
    r = lax.broadcasted_iota(jnp.int32, (CUMSUM_BLOCK, CUMSUM_BLOCK), 0)
    c = lax.broadcasted_iota(jnp.int32, (CUMSUM_BLOCK, CUMSUM_BLOCK), 1)
    tri = jnp.where(r < c, 1.0, 0.0).astype(BF16)
    eq_f = jnp.where(eq, 1.0, 0.0)
    tie_ok = _prefix_count(eq_f, tri) < (cap - n_gt)
    sel_f = jnp.where(gt, 1.0, jnp.where(tie_ok, eq_f, 0.0))
    pos = _prefix_count(sel_f, tri)
    posm_ref[0] = jnp.where(sel_f > 0.5, pos, -1.0).astype(jnp.int32)
    aff_ref[0] = aff
    l = lg.shape[1]
    starts = [pos[:, k * tile:k * tile + 1] for k in range(l // tile)]
    starts.append(jnp.full((e, 1), float(cap), F32))
    bnd_ref[0] = jnp.concatenate(starts, axis=1).astype(jnp.int32)


def _route(lg_t, batch, cap, tile):
    e, m = lg_t.shape
    l = m // batch
    nt = l // tile
    est = 16 * _nbytes((e, l), F32)
    return pl.pallas_call(
        functools.partial(_route_kernel, cap=cap, tile=tile),
        grid=(batch,),
        in_specs=[pl.BlockSpec((e, l), lambda b: (0, b))],
        out_specs=[pl.BlockSpec((1, e, l), lambda b: (b, 0, 0))] * 2
        + [pl.BlockSpec((1, e, nt + 1), lambda b: (b, 0, 0))],
        out_shape=[jax.ShapeDtypeStruct((batch, e, l), jnp.int32),
                   jax.ShapeDtypeStruct((batch, e, l), F32),
                   jax.ShapeDtypeStruct((batch, e, nt + 1), jnp.int32)],
        compiler_params=_cparams(("parallel",), est),
        name="route",
    )(lg_t)


def _slots_kernel(posm_ref, aff_ref, *rest, cap):
    idx_ref, gs_ref = rest[-2], rest[-1]
    pm = posm_ref[0, 0]
    l = pm.shape[1]
    slot = lax.broadcasted_iota(jnp.int32, (cap, l), 0)
    tok = lax.broadcasted_iota(jnp.int32, (1, l), 1).astype(F32)
    hit = pm == slot
    idx_ref[0, 0] = jnp.sum(jnp.where(hit, tok, 0.0), axis=1, keepdims=True).astype(jnp.int32)
    gs_ref[0] = jnp.sum(jnp.where(hit, aff_ref[0, 0], 0.0), axis=1, keepdims=True)


def _slots(posm, aff, cap, n_rows, row_off, gs_all=None):
    batch, e, l = posm.shape
    blk = row_off // cap
    in_specs = [pl.BlockSpec((1, 1, 1, l), lambda b, ee: (b, ee, 0, 0))] * 2
    args = [posm.reshape(batch, e, 1, l), aff.reshape(batch, e, 1, l)]
    aliases = {}
    if gs_all is not None:
        in_specs.append(pl.BlockSpec(memory_space=pl.ANY))
        args.append(gs_all)
        aliases = {2: 1}
    est = 6 * _nbytes((cap, l), F32) + 4 * _nbytes((cap, V7X_LANES), F32)
    return pl.pallas_call(
        functools.partial(_slots_kernel, cap=cap),
        grid=(batch, e),
        in_specs=in_specs,
        out_specs=[pl.BlockSpec((1, 1, cap, 1), lambda b, ee: (b, ee, 0, 0)),
                   pl.BlockSpec((1, cap, 1), lambda b, ee: (ee, blk + b, 0))],
        out_shape=[jax.ShapeDtypeStruct((batch, e, cap, 1), jnp.int32),
                   jax.ShapeDtypeStruct((e, n_rows, 1), F32)],
        input_output_aliases=aliases,
        compiler_params=_cparams(("parallel", "parallel"), est),
        name="moe_slots",
    )(*args)


GATHER_ISSUE_UNROLL = 8


def _gather_kernel(idx_ref, hp_hbm, *rest, cap, seq, n_exp):
    o_ref, sem = rest[-2], rest[-1]
    b, ee = pl.program_id(0), pl.program_id(1)
    base = (b * n_exp + ee) * cap
    row0 = b * seq

    def issue(c, carry):
        for k in range(GATHER_ISSUE_UNROLL):
            s = c * GATHER_ISSUE_UNROLL + k
            pltpu.make_async_copy(hp_hbm.at[pl.ds(row0 + idx_ref[base + s], 1)],
                                  o_ref.at[0, pl.ds(s, 1)], sem).start()
        return carry

    lax.fori_loop(0, cap // GATHER_ISSUE_UNROLL, issue, 0)
    pltpu.make_async_copy(hp_hbm.at[pl.ds(0, cap)], o_ref.at[0], sem).wait()


def _gather(idx, hp, batch, cap, n_rows, row_off, xg_all=None):
    e = idx.shape[1]
    m, dh = hp.shape
    seq = m // batch
    blk = row_off // cap
    in_specs = [pl.BlockSpec(memory_space=pl.ANY)]
    args = [idx.reshape(-1), hp]
    aliases = {}
    if xg_all is not None:
        in_specs.append(pl.BlockSpec(memory_space=pl.ANY))
        args.append(xg_all)
        aliases = {2: 0}
    est = 2 * _nbytes((cap, dh), jnp.uint32)
    return pl.pallas_call(
        functools.partial(_gather_kernel, cap=cap, seq=seq, n_exp=e),
        grid_spec=pltpu.PrefetchScalarGridSpec(
            num_scalar_prefetch=1,
            grid=(batch, e),
            in_specs=in_specs,
            out_specs=pl.BlockSpec((1, cap, dh), lambda b, ee, ix: (ee, blk + b, 0)),
            scratch_shapes=[pltpu.SemaphoreType.DMA(())],
        ),
        out_shape=jax.ShapeDtypeStruct((e, n_rows, dh), jnp.uint32),
        input_output_aliases=aliases,
        compiler_params=_cparams(("arbitrary", "arbitrary"), est),
        name="moe_gather",
    )(*args)


EXPERT_ROW_BLOCKS = 2


def _up_kernel(xg_ref, wg_ref, wu_ref, o_ref, wg_bf, wu_bf):
    @pl.when(pl.program_id(2) == 0)
    def _():
        wg_bf[...] = wg_ref[0, 0].astype(BF16)
        wu_bf[...] = wu_ref[0, 0].astype(BF16)

    x_lo, x_hi = _unpack_bf16_pairs(xg_ref[0])
    half = x_lo.shape[1]
    g = (jnp.dot(x_lo, wg_bf[0:half, :], preferred_element_type=F32)
         + jnp.dot(x_hi, wg_bf[half:, :], preferred_element_type=F32))
    u = (jnp.dot(x_lo, wu_bf[0:half, :], preferred_element_type=F32)
         + jnp.dot(x_hi, wu_bf[half:, :], preferred_element_type=F32))
    o_ref[0] = (g * jax.nn.sigmoid(g) * u).astype(o_ref.dtype)


def _expert_up(xg, w_gate, w_up, layer, tf=256):
    e, r, dh = xg.shape
    d = 2 * dh
    f = w_gate.shape[3]
    tf = min(tf, f)
    tr = r // EXPERT_ROW_BLOCKS
    est = (4 * _nbytes((d, tf), F32) + 3 * _nbytes((d, tf), BF16) + 2 * _nbytes((tr, d), BF16)
           + 6 * _nbytes((tr, tf), F32))
    return pl.pallas_call(
        _up_kernel,
        grid=(e, f // tf, EXPERT_ROW_BLOCKS),
        in_specs=[
            pl.BlockSpec((1, tr, dh), lambda ee, fj, rr: (ee, rr, 0)),
            pl.BlockSpec((1, 1, d, tf), lambda ee, fj, rr: (layer, ee, 0, fj)),
            pl.BlockSpec((1, 1, d, tf), lambda ee, fj, rr: (layer, ee, 0, fj)),
        ],
        out_specs=pl.BlockSpec((1, tr, tf), lambda ee, fj, rr: (ee, rr, fj)),
        out_shape=jax.ShapeDtypeStruct((e, r, f), BF16),
        scratch_shapes=[pltpu.VMEM((d, tf), BF16), pltpu.VMEM((d, tf), BF16)],
        compiler_params=_cparams(("parallel", "parallel", "arbitrary"), est),
        name="moe_up",
    )(xg, w_gate, w_up)


def _down_kernel(h_ref, wd_ref, gs_ref, o_ref, wd_bf):
    @pl.when(pl.program_id(2) == 0)
    def _():
        wd_bf[...] = wd_ref[0, 0].astype(BF16)

    y = jnp.dot(h_ref[0], wd_bf[...], preferred_element_type=F32)
    o_ref[0] = (y * gs_ref[0]).astype(o_ref.dtype)


def _expert_down(hid, w_down, gslot, layer, td=1024):
    e, r, f = hid.shape
    d = w_down.shape[3]
    td = min(td, d)
    tr = r // EXPERT_ROW_BLOCKS
    est = (2 * _nbytes((f, td), F32) + 2 * _nbytes((f, td), BF16) + 2 * _nbytes((tr, f), BF16)
           + 4 * _nbytes((tr, td), F32) + 2 * _nbytes((tr, V7X_LANES), F32))
    return pl.pallas_call(
        _down_kernel,
        grid=(e, d // td, EXPERT_ROW_BLOCKS),
        in_specs=[
            pl.BlockSpec((1, tr, f), lambda ee, dj, rr: (ee, rr, 0)),
            pl.BlockSpec((1, 1, f, td), lambda ee, dj, rr: (layer, ee, 0, dj)),
            pl.BlockSpec((1, tr, 1), lambda ee, dj, rr: (ee, rr, 0)),
        ],
        out_specs=pl.BlockSpec((1, tr, td), lambda ee, dj, rr: (ee, rr, dj)),
        out_shape=jax.ShapeDtypeStruct((e, r, d), BF16),
        scratch_shapes=[pltpu.VMEM((f, td), BF16)],
        compiler_params=_cparams(("parallel", "parallel", "arbitrary"), est),
        name="moe_down",
    )(hid, w_down, gslot)


COMBINE_TILE = 256
DMA_ROW_ALIGN = 16


def combine_window(cap, seq, tile):
    return min(cap, max(DMA_ROW_ALIGN, 2 * tile * cap // seq))


def _combine_kernel(tbl_ref, pt_ref, y_hbm, x_ref, g_ref, *rest, n_exp, cap, win, nt, row_off,
                    n_steps, emit_x, post_norm, post_mod, eps):
    stage, extra, acc_scr, sems, xsem = rest[-5:]
    n_post_in = (1 + 2 * post_mod) if post_norm else 0
    post_in = rest[:n_post_in]
    outs = rest[n_post_in:-5]
    n = pl.program_id(0) * nt + pl.program_id(1)
    cur = n % 2
    nxt_step = jnp.minimum(n + 1, n_steps - 1)

    def window_starts(step):
        b, i = step // nt, step % nt
        t0 = (b * (nt + 1) + i) * n_exp
        out = []
        for ee in range(n_exp):
            s0a = jnp.minimum((tbl_ref[t0 + ee] // DMA_ROW_ALIGN) * DMA_ROW_ALIGN, cap - win)
            out.append(pl.multiple_of(s0a, DMA_ROW_ALIGN))
        return row_off + b * cap, out, t0

    def issue_windows(step, buf):
        row_base, starts, _ = window_starts(step)
        for ee in range(n_exp):
            pltpu.make_async_copy(y_hbm.at[ee, pl.ds(row_base + starts[ee], win), :],
                                  stage.at[buf, pl.ds(ee * win, win), :], sems.at[buf]).start()

    def wait_windows(buf):
        pltpu.make_async_copy(y_hbm.at[0, pl.ds(0, n_exp * win), :], stage.at[buf],
                              sems.at[buf]).wait()

    @pl.when(n == 0)
    def _():
        issue_windows(0, 0)

    issue_windows(nxt_step, 1 - cur)
    row_base, starts, t0 = window_starts(n)
    pt = pt_ref[...]
    lane = lax.broadcasted_iota(jnp.int32, (pt.shape[0], win), 1)
    p = jnp.concatenate(
        [jnp.where(pt[:, ee:ee + 1] - starts[ee] == lane, 1.0, 0.0).astype(BF16)
         for ee in range(n_exp)], axis=1)
    wait_windows(cur)
    acc_scr[...] = jnp.dot(p, stage[cur], preferred_element_type=F32)

    for ee in range(n_exp):
        first = starts[ee] + win
        n_extra = jnp.maximum(0, (tbl_ref[t0 + n_exp + ee] - first + win - 1) // win)

        def extra_window(w, carry, ee=ee, first=first):
            lo = first + w * win
            src0 = pl.multiple_of(jnp.minimum(lo, cap - win), DMA_ROW_ALIGN)
            cp = pltpu.make_async_copy(y_hbm.at[ee, pl.ds(row_base + src0, win), :], extra, xsem)
            cp.start()
            cp.wait()
            col = pt[:, ee:ee + 1]
            hit = jnp.where(col >= lo, col - src0, -1) == lane
            acc_scr[...] += jnp.dot(jnp.where(hit, 1.0, 0.0).astype(BF16), extra[...],
                                    preferred_element_type=F32)
            return carry

        lax.fori_loop(0, n_extra, extra_window, 0)

    xn = x_ref[...] + g_ref[0] * acc_scr[...]
    if emit_x:
        outs[0][...] = xn
    if post_norm:
        shift, scale = (post_in[1][0], post_in[2][0]) if post_mod else (None, None)
        h_ref = outs[-1]
        h_ref[...] = _rms_modulate(xn, post_in[0][...], shift, scale, eps).astype(h_ref.dtype)

    @pl.when(n == n_steps - 1)
    def _():
        wait_windows(1 - cur)


def _combine(posm_t, bounds, y, x2, gate, batch, cap, row_off, post=None, emit_x=True):
    m, d = x2.shape
    l = m // batch
    e = posm_t.shape[1]
    tm = min(COMBINE_TILE, l)
    nt = l // tm
    win = combine_window(cap, l, tm)
    assert cap % DMA_ROW_ALIGN == 0 and win % DMA_ROW_ALIGN == 0 and bounds.shape == (batch, e, nt + 1)
    tbl = bounds.transpose(0, 2, 1).reshape(-1)
    est = (2 * _nbytes((e * win, d), BF16) + _nbytes((win, d), BF16) + 9 * _nbytes((tm, d), F32)
           + 2 * _nbytes((tm, e * win), BF16) + 2 * _nbytes((tm, V7X_LANES), F32))
    row_spec = pl.BlockSpec((tm, d), lambda b, i, t: (b * nt + i, 0))
    vec_spec = pl.BlockSpec((1, 1, d), lambda b, i, t: (b, 0, 0))
    in_specs = [pl.BlockSpec((tm, e), lambda b, i, t: (b * nt + i, 0)),
                pl.BlockSpec(memory_space=pl.ANY), row_spec, vec_spec]
    args = [tbl, posm_t, y, x2, gate]
    out_specs, out_shape = [], []
    if emit_x:
        out_specs.append(row_spec)
        out_shape.append(jax.ShapeDtypeStruct((m, d), F32))
    post_mod = False
    if post is not None:
        p_gain, p_shift, p_scale, p_dtype = post
        post_mod = p_shift is not None
        in_specs.append(pl.BlockSpec((1, d), lambda b, i, t: (0, 0)))
        args.append(p_gain.reshape(1, d))
        if post_mod:
            in_specs += [vec_spec, vec_spec]
            args += [p_shift, p_scale]
        out_specs.append(row_spec)
        out_shape.append(jax.ShapeDtypeStruct((m, d), p_dtype))
        est += 2 * _nbytes((tm, d), p_dtype)
    return pl.pallas_call(
        functools.partial(_combine_kernel, n_exp=e, cap=cap, win=win, nt=nt, row_off=row_off,
                          n_steps=batch * nt, emit_x=emit_x, post_norm=post is not None,
                          post_mod=post_mod, eps=OP["rms_eps"]),
        grid_spec=pltpu.PrefetchScalarGridSpec(
            num_scalar_prefetch=1,
            grid=(batch, nt),
            in_specs=in_specs,
            out_specs=out_specs,
            scratch_shapes=[pltpu.VMEM((2, e * win, d), BF16), pltpu.VMEM((win, d), BF16),
                            pltpu.VMEM((tm, d), F32), pltpu.SemaphoreType.DMA((2,)),
                            pltpu.SemaphoreType.DMA(())],
        ),
        out_shape=out_shape,
        input_output_aliases={3: 0} if emit_x else {},
        compiler_params=_cparams(("arbitrary", "arbitrary"), est),
        name="moe_combine",
    )(*args)


def _sigmoid(z):
    return jax.nn.sigmoid(z)


def _in_proj(h2, w_in, layer, n_cols):
    return _mm([(h2, w_in, layer, 0)], [], lambda accs, ex: accs[0], n_cols, F32, 1024,
               math.gcd(512, n_cols), name="in_proj")


def _mixer_out(x2, a2, px, g_mix, wts, layer, dfts, batch, rpb):
    w_glu, w_s5o, w_fto, w_out = wts
    wc, cl, sl = dfts
    d = x2.shape[1]
    d_s5 = a2.shape[1]
    d_ft = w_fto.shape[1]
    xc, xs = _chan_dft(px, d_s5, d_ft, wc)
    yf = _seq_dft(xc, xs, cl, sl, batch)
    glu = _mm([(a2, w_glu, layer, 0)], [("tile", a2, 0)],
              lambda accs, ex: ex[0] * _sigmoid(accs[0]),
              d_s5, BF16, 1024, 512, name="glu")
    tn = math.gcd(512, d_s5 + d_ft, d)
    g0 = (d_s5 + d_ft) // tn
    merged = _mm([(glu, w_s5o, layer, 0), (yf, w_fto, layer, 0)],
                 [("tile", px, g0), ("tile", px, g0 + d // tn)],
                 lambda accs, ex: _sigmoid(ex[0]) * accs[0] + _sigmoid(ex[1]) * accs[1],
                 d, BF16, 1024, tn, name="merge")
    return _mm([(merged, w_out, layer, 0)], [("tile", x2, 0), ("row", g_mix, 0)],
               lambda accs, ex: ex[0] + ex[1] * accs[0],
               d, F32, 1024, tn, rows_per_batch=rpb, alias_extra=0, name="out_proj")


def _moe(streams, w_gate, w_up, w_down, layer):
    e = streams[0][4].shape[0]
    caps = [OP["capacity_factor"] * (s[0].shape[0] // s[1]) // e for s in streams]
    offs, n_rows = [], 0
    for s, cap in zip(streams, caps):
        assert n_rows % cap == 0
        offs.append(n_rows)
        n_rows += s[1] * cap
    xg = gslot = None
    if len(streams) > 1:
        xg = jnp.zeros((e, n_rows, streams[0][2].shape[1]), jnp.uint32)
        gslot = jnp.zeros((e, n_rows, 1), F32)
    routed = []
    for (x2, batch, hp, _, lg_t, _, _), cap, off in zip(streams, caps, offs):
        tile = min(COMBINE_TILE, x2.shape[0] // batch)
        posm, aff, bounds = _route(lg_t, batch, cap, tile)
        idx, gslot = _slots(posm, aff, cap, n_rows, off, gslot)
        xg = _gather(idx, hp, batch, cap, n_rows, off, xg)
        routed.append((posm, bounds))
    y = _expert_down(_expert_up(xg, w_gate, w_up, layer), w_down, gslot, layer)
    outs = []
    for (x2, batch, _, gate, _, post, emit_x), cap, off, (posm, bounds) in zip(
            streams, caps, offs, routed):
        posm_t = posm.transpose(0, 2, 1).reshape(x2.shape[0], e)
        outs.append(_combine(posm_t, bounds, y, x2, gate, batch, cap, off, post, emit_x))
    return outs


def _grid_posembed(n_tok, d):
    gw = OP["grid_w"]
    rows = n_tok // gw
    quarter = d // 4
    inv_freq = OP["pos_base"] ** (-jnp.arange(quarter, dtype=F32) / quarter)
    ang_r = jnp.arange(rows, dtype=F32)[:, None] * inv_freq
    ang_c = jnp.arange(gw, dtype=F32)[:, None] * inv_freq
    return (jnp.concatenate([jnp.sin(ang_r), jnp.cos(ang_r)], axis=-1),
            jnp.concatenate([jnp.sin(ang_c), jnp.cos(ang_c)], axis=-1))


def kernel(x, c, ctx, c_ctx, ada_w, ada_b, norm_mix_g, norm_ffn_g, w_in, s5_lam_re, s5_lam_im,
           s5_log_dt, s5_b_re, s5_b_im, s5_c_re, s5_c_im, s5_d, w_glu, w_s5_out, w_ft_out, w_out,
           w_router, w_gate, w_up, w_down, norm_final_g):
    batch, seq, d = x.shape
    ctx_len = ctx.shape[1]
    depth = ada_w.shape[0]
    d_s5 = w_glu.shape[1]
    d_ft = w_ft_out.shape[1]
    n_mod = OP["n_mod"]
    m_x, m_c = batch * seq, batch * ctx_len

    x2 = x.reshape(m_x, d)
    pos = tuple(p.astype(x.dtype) for p in _grid_posembed(seq, d))
    c2 = ctx.reshape(m_c, d)

    rows = -(-(batch + 1) // 8) * 8
    c8 = jnp.zeros((rows, d), F32).at[:batch].set(c).at[batch].set(c_ctx)
    mod = _adaln(c8, ada_w, ada_b)

    gd = d_ft // OP["ft_groups"]
    wc_c, wc_s = _dft_mats(gd)
    wc = jnp.concatenate([wc_c, wc_s], axis=1)
    dft_x = (wc,) + _dft_mats(seq)
    dft_c = (wc,) + _dft_mats(ctx_len)

    bd, wc_s5, oc_s5, at = jax.vmap(lambda *p: _s5_operators(*p, S5_CHUNK))(
        s5_lam_re, s5_lam_im, s5_log_dt, s5_b_re, s5_b_im, s5_c_re, s5_c_im, s5_d)
    s5_ops = (bd, wc_s5, oc_s5) + _s5_masks(S5_CHUNK) + (at,)
    d_in = w_in.shape[2]
    mix_w = (w_glu, w_s5_out, w_ft_out, w_out)

    mods_x = [[mod[i, :batch, k * d:(k + 1) * d].reshape(batch, 1, d) for k in range(n_mod)]
              for i in range(depth)]
    mods_c = [[jnp.broadcast_to(mod[i, batch:batch + 1, k * d:(k + 1) * d].reshape(1, 1, d),
                                (batch, 1, d)) for k in range(n_mod)] for i in range(depth)]

    hx = hc = None
    for i in range(depth):
        last = i == depth - 1
        mx, mc = mods_x[i], mods_c[i]
        wr_t = w_router[i].T.astype(BF16)
        if i == 0:
            hc = _norm(c2, norm_mix_g[i], mc[0], mc[1], rows_per_batch=ctx_len)
            hx, x2 = _norm(x2, norm_mix_g[i], mx[0], mx[1], rows_per_batch=seq, pos=pos)
        pc = _in_proj(hc, w_in, i, d_s5 if last else d_in)
        px = _in_proj(hx, w_in, i, d_in)
        ac, ax = _s5(pc, px, d_s5, batch, s5_ops, i)
        x2 = _mixer_out(x2, ax, px, mx[2], mix_w, i, dft_x, batch, seq)
        lgx, hpx = _norm(x2, norm_ffn_g[i], mx[3], mx[4], rows_per_batch=seq, wr_t=wr_t,
                         emit_h=False, pack=True)
        if last:
            post_x, post_c = (norm_final_g, None, None, x.dtype), None
        else:
            post_x = (norm_mix_g[i + 1], mods_x[i + 1][0], mods_x[i + 1][1], BF16)
            post_c = (norm_mix_g[i + 1], mods_c[i + 1][0], mods_c[i + 1][1], BF16)
        streams = [(x2, batch, hpx, mx[5], lgx, post_x, not last)]
        if not last:
            c2 = _mixer_out(c2, ac, pc, mc[2], mix_w, i, dft_c, batch, ctx_len)
            lgc, hpc = _norm(c2, norm_ffn_g[i], mc[3], mc[4], rows_per_batch=ctx_len, wr_t=wr_t,
                             emit_h=False, pack=True)
            streams.append((c2, batch, hpc, mc[5], lgc, post_c, True))
        outs = _moe(streams, w_gate, w_up, w_down, i)
        if last:
            return outs[0][0].reshape(batch, seq, d)
        x2, hx = outs[0]
        c2, hc = outs[1]
```

```python
import functools
import math

import jax
import jax.numpy as jnp
from jax import lax
from jax.experimental import pallas as pl
from jax.experimental.pallas import tpu as pltpu

BF16 = jnp.bfloat16
F32 = jnp.float32
HI = lax.Precision.HIGHEST

OP = dict(
    s5_group_dim=16,
    s5_state=64,
    ft_groups=4,
    capacity_factor=2,
    n_mod=6,
    rms_eps=1e-6,
    pos_base=10000.0,
    grid_w=64,
    lambda_re_max=-1e-4,
)

V7X_VMEM_BYTES = 64 * 1024 * 1024
V7X_LANES = 128
S5_TILE_GROUPS = V7X_LANES // 16
S5_CHUNK = 16
CUMSUM_BLOCK = 256


def _cparams(sem, vmem_est):
    limit = int(min(max(vmem_est * 5 // 4 + (4 << 20), 32 << 20), V7X_VMEM_BYTES - (6 << 20)))
    return pltpu.CompilerParams(dimension_semantics=sem, vmem_limit_bytes=limit)


def _nbytes(shape, dtype):
    return math.prod(shape) * jnp.dtype(dtype).itemsize


def _adaln_kernel(c_ref, w_ref, b_ref, o_ref):
    cv = c_ref[...]
    a = (cv * jax.nn.sigmoid(cv)).astype(BF16)
    o_ref[0] = jnp.dot(a, w_ref[0].astype(BF16), preferred_element_type=F32) + b_ref[0]


def _adaln(c8, ada_w, ada_b, tn=512):
    depth, d, n6 = ada_w.shape
    rows = c8.shape[0]
    tn = min(tn, n6)
    est = 2 * _nbytes((d, tn), F32) + 4 * _nbytes((rows, tn), F32) + _nbytes((rows, d), F32) * 2
    return pl.pallas_call(
        _adaln_kernel,
        grid=(depth, n6 // tn),
        in_specs=[
            pl.BlockSpec((rows, d), lambda l, j: (0, 0)),
            pl.BlockSpec((1, d, tn), lambda l, j: (l, 0, j)),
            pl.BlockSpec((1, 1, tn), lambda l, j: (l, 0, j)),
        ],
        out_specs=pl.BlockSpec((1, rows, tn), lambda l, j: (l, 0, j)),
        out_shape=jax.ShapeDtypeStruct((depth, rows, n6), F32),
        compiler_params=_cparams(("parallel", "parallel"), est),
        name="adaln",
    )(c8, ada_w, ada_b.reshape(depth, 1, n6))


def _rms_modulate(xf, g, shift, scale, eps):
    ms = jnp.mean(xf * xf, axis=-1, keepdims=True)
    y = xf * lax.rsqrt(ms + eps) * g
    if shift is not None:
        y = y * (1.0 + scale) + shift
    return y


def _pack_bf16_pairs(y):
    half = y.shape[1] // 2
    lo = pltpu.bitcast(y[:, :half].astype(BF16).astype(F32), jnp.uint32) >> 16
    hi = pltpu.bitcast(y[:, half:].astype(BF16).astype(F32), jnp.uint32) & jnp.uint32(0xFFFF0000)
    return hi | lo


def _unpack_bf16_pairs(p):
    lo = pltpu.bitcast(p << 16, F32).astype(BF16)
    hi = pltpu.bitcast(p & jnp.uint32(0xFFFF0000), F32).astype(BF16)
    return lo, hi


def _norm_kernel(*refs, eps, modulate, router, add_pos, emit_h, pack):
    x_ref, g_ref = refs[0], refs[1]
    k = 2
    xf = x_ref[...]
    if add_pos:
        pr, pc = refs[k][0], refs[k + 1][...]
        k += 2
        gw = pc.shape[0]
        pos = jnp.concatenate(
            [jnp.concatenate([jnp.broadcast_to(pr[q:q + 1], pc.shape), pc], axis=1)
             for q in range(xf.shape[0] // gw)], axis=0)
        xf = xf + pos
        refs[-1][...] = xf
    shift = scale = None
    if modulate:
        shift, scale = refs[k][0], refs[k + 1][0]
        k += 2
    y = _rms_modulate(xf, g_ref[...], shift, scale, eps)
    if router:
        wr_ref = refs[k]
        k += 1
    if emit_h:
        refs[k][...] = y.astype(refs[k].dtype)
        k += 1
    if router:
        refs[k][...] = lax.dot_general(
            wr_ref[...], y.astype(BF16), (((1,), (1,)), ((), ())), preferred_element_type=F32)
        k += 1
    if pack:
        refs[k][...] = _pack_bf16_pairs(y)


def _norm(x2, g, shift=None, scale=None, rows_per_batch=None, wr_t=None, out_dtype=BF16, tm=256,
          pos=None, emit_h=True, pack=False):
    m, d = x2.shape
    modulate = shift is not None
    router = wr_t is not None
    add_pos = pos is not None
    rpb = rows_per_batch if rows_per_batch is not None else m
    tm = min(tm, rpb)
    tpb = rpb // tm
    in_specs = [pl.BlockSpec((tm, d), lambda i: (i, 0)), pl.BlockSpec((1, d), lambda i: (0, 0))]
    args = [x2, g.reshape(1, d)]
    if add_pos:
        pos_rows, pos_cols = pos
        gw = pos_cols.shape[0]
        assert tm % gw == 0 and rpb % tm == 0
        in_specs.append(pl.BlockSpec((1, tm // gw, d // 2), lambda i: (i % tpb, 0, 0)))
        in_specs.append(pl.BlockSpec((gw, d // 2), lambda i: (0, 0)))
        args += [pos_rows.reshape(rpb // tm, tm // gw, d // 2), pos_cols]
    if modulate:
        in_specs += [pl.BlockSpec((1, 1, d), lambda i: (i // tpb, 0, 0))] * 2
        args += [shift, scale]
    out_specs, out_shape = [], []
    if emit_h:
        out_specs.append(pl.BlockSpec((tm, d), lambda i: (i, 0)))
        out_shape.append(jax.ShapeDtypeStruct((m, d), out_dtype))
    if router:
        e = wr_t.shape[0]
        in_specs.append(pl.BlockSpec((e, d), lambda i: (0, 0)))
        args.append(wr_t)
        out_specs.append(pl.BlockSpec((e, tm), lambda i: (0, i)))
        out_shape.append(jax.ShapeDtypeStruct((e, m), F32))
    est = 2 * _nbytes((tm, d), F32) + 2 * _nbytes((tm, d), out_dtype) + 3 * _nbytes((tm, d), F32)
    if pack:
        out_specs.append(pl.BlockSpec((tm, d // 2), lambda i: (i, 0)))
        out_shape.append(jax.ShapeDtypeStruct((m, d // 2), jnp.uint32))
        est += 4 * _nbytes((tm, d // 2), F32)
    if add_pos:
        out_specs.append(pl.BlockSpec((tm, d), lambda i: (i, 0)))
        out_shape.append(jax.ShapeDtypeStruct((m, d), F32))
        est += 4 * _nbytes((tm, d), F32)
    res = pl.pallas_call(
        functools.partial(_norm_kernel, eps=OP["rms_eps"], modulate=modulate, router=router,
                          add_pos=add_pos, emit_h=emit_h, pack=pack),
        grid=(m // tm,),
        in_specs=in_specs,
        out_specs=out_specs,
        out_shape=out_shape,
        compiler_params=_cparams(("parallel",), est),
        name="rmsnorm",
    )(*args)
    return res if len(res) > 1 else res[0]


def _mm_kernel(*refs, n_pairs, kinds, epilogue):
    n_in = 2 * n_pairs + len(kinds)
    o_ref = refs[n_in]
    w_bf = refs[n_in + 1:]

    @pl.when(pl.program_id(1) == 0)
    def _():
        for k in range(n_pairs):
            w_bf[k][...] = refs[2 * k + 1][...].astype(BF16)

    accs = [jnp.dot(refs[2 * k][...].astype(BF16), w_bf[k][...], preferred_element_type=F32)
            for k in range(n_pairs)]
    ex = []
    for k, kind in enumerate(kinds):
        r = refs[2 * n_pairs + k]
        ex.append(r[0] if kind == "row" else r[...])
    o_ref[...] = epilogue(accs, ex).astype(o_ref.dtype)


def _mm(pairs, extras, epilogue, n_out, out_dtype, tm, tn, rows_per_batch=None, alias_extra=None,
        name="mm"):
    m = pairs[0][0].shape[0]
    rpb = rows_per_batch if rows_per_batch is not None else m
    tm = min(tm, rpb)
    tn = min(tn, n_out)
    tpb = rpb // tm
    in_specs, args, scratch = [], [], []
    est = 2 * _nbytes((tm, tn), out_dtype) + 2 * _nbytes((tm, tn), F32) * max(1, len(pairs))
    for a, w, layer, off in pairs:
        kdim = a.shape[1]
        in_specs.append(pl.BlockSpec((tm, kdim), lambda j, i: (i, 0)))
        in_specs.append(pl.BlockSpec((None, kdim, tn),
                                     lambda j, i, off=off, layer=layer: (layer, 0, j + off)))
        scratch.append(pltpu.VMEM((kdim, tn), BF16))
        args += [a, w]
        est += (2 * _nbytes((tm, kdim), a.dtype) + 2 * _nbytes((kdim, tn), w.dtype)
                + _nbytes((kdim, tn), BF16))
        if a.dtype != BF16:
            est += _nbytes((tm, kdim), BF16)
    kinds = []
    for kind, arr, off in extras:
        kinds.append(kind)
        if kind == "tile":
            in_specs.append(pl.BlockSpec((tm, tn), lambda j, i, off=off: (i, j + off)))
            est += 2 * _nbytes((tm, tn), arr.dtype)
        else:
            in_specs.append(pl.BlockSpec((1, 1, tn), lambda j, i, off=off: (i // tpb, 0, j + off)))
        args.append(arr)
    aliases = {}
    if alias_extra is not None:
        aliases = {2 * len(pairs) + alias_extra: 0}
    return pl.pallas_call(
        functools.partial(_mm_kernel, n_pairs=len(pairs), kinds=tuple(kinds), epilogue=epilogue),
        grid=(n_out // tn, m // tm),
        in_specs=in_specs,
        out_specs=pl.BlockSpec((tm, tn), lambda j, i: (i, j)),
        out_shape=jax.ShapeDtypeStruct((m, n_out), out_dtype),
        scratch_shapes=scratch,
        input_output_aliases=aliases,
        compiler_params=_cparams(("parallel", "arbitrary"), est),
        name=name,
    )(*args)


def _s5_operators(lam_re, lam_im, log_dt, b_re, b_im, c_re, c_im, d_skip, t):
    n = OP["s5_group_dim"]
    lr = jnp.minimum(lam_re.astype(F32), OP["lambda_re_max"])
    li = lam_im.astype(F32)
    dt = jnp.exp(log_dt.astype(F32))[..., None]
    g, p = lr.shape[1], lr.shape[2]
    gt = S5_TILE_GROUPS
    j = g // gt
    mag = jnp.exp(lr * dt)
    a_re = mag * jnp.cos(li * dt)
    a_im = mag * jnp.sin(li * dt)
    num_re, num_im = a_re - 1.0, a_im
    den = lr * lr + li * li
    f_re = (num_re * lr + num_im * li) / den
    f_im = (num_im * lr - num_re * li) / den
    br, bi = b_re.astype(F32), b_im.astype(F32)
    bb_re = f_re[..., None] * br - f_im[..., None] * bi
    bb_im = f_re[..., None] * bi + f_im[..., None] * br
    cr, ci = c_re.astype(F32), c_im.astype(F32)

    k = jnp.arange(t + 1, dtype=F32)[:, None, None, None]
    pw_mag = jnp.exp(lr[None] * dt[None] * k)
    pw_re = pw_mag * jnp.cos(li[None] * dt[None] * k)
    pw_im = pw_mag * jnp.sin(li[None] * dt[None] * k)

    ca_re = cr[None] * pw_re[:t, :, :, None, :] - ci[None] * pw_im[:t, :, :, None, :]
    ca_im = cr[None] * pw_im[:t, :, :, None, :] + ci[None] * pw_re[:t, :, :, None, :]
    ca = jnp.concatenate([ca_re, -ca_im], axis=-1)
    ca = ca.transpose(1, 2, 0, 3, 4).reshape(2, g, t * n, 2 * p)
    bbs = jnp.concatenate([bb_re, bb_im], axis=2)
    kk = jnp.einsum("dgxp,dgpm->dgxm", ca, bbs, precision=HI)
    kk = kk.reshape(2, g, t, n, n).transpose(2, 0, 1, 3, 4)
    kf, kb = kk[:, 0], kk[:, 1]
    k0 = kf[0] + kb[0] + d_skip.astype(F32).reshape(g, n)[:, :, None] * jnp.eye(n, dtype=F32)
    kall = jnp.concatenate([kb[1:][::-1], k0[None], kf[1:]], axis=0)
    kc = kall.reshape(2 * t - 1, j, gt, n, n).transpose(1, 0, 4, 2, 3)
    kc = kc.reshape(j, 2 * t - 1, n, gt * n).astype(BF16)
    lane = jnp.arange(gt * n)
    same_group = (lane[:, None] // n) == (lane[None, :] // n)
    bd = jnp.where(same_group, jnp.tile(kc, (1, 1, gt, 1)), 0)

    ps_re = jnp.stack([pw_re[:t][::-1, 0], pw_re[:t, 1]], axis=0)
    ps_im = jnp.stack([pw_im[:t][::-1, 0], pw_im[:t, 1]], axis=0)
    w_re = ps_re[..., None] * bb_re[:, None] - ps_im[..., None] * bb_im[:, None]
    w_im = ps_re[..., None] * bb_im[:, None] + ps_im[..., None] * bb_re[:, None]
    w6 = jnp.stack([w_re, w_im], axis=1).reshape(2, 2, t, j, gt, p, n)
    wc = w6.transpose(3, 2, 6, 0, 1, 4, 5).reshape(j, t, n, 4 * gt * p).astype(BF16)

    po_re = jnp.stack([pw_re[1:, 0], pw_re[1:][::-1, 1]], axis=0)
    po_im = jnp.stack([pw_im[1:, 0], pw_im[1:][::-1, 1]], axis=0)
    co_re = cr[:, None] * po_re[:, :, :, None, :] - ci[:, None] * po_im[:, :, :, None, :]
    co_im = cr[:, None] * po_im[:, :, :, None, :] + ci[:, None] * po_re[:, :, :, None, :]
    o6 = jnp.stack([co_re, -co_im], axis=1).reshape(2, 2, t, j, gt, n, p)
    oc = o6.transpose(3, 0, 1, 6, 2, 4, 5).reshape(j, 4, p, t * gt * n).astype(BF16)
    at = jnp.stack([pw_re[t, 0], pw_im[t, 0], pw_re[t, 1], pw_im[t, 1]], axis=0)
    at = at.reshape(4, j, gt * p).transpose(1, 0, 2).reshape(j, 1, 4 * gt * p)
    return bd, wc, oc, at


def _s5_masks(t):
    n, p, gt = OP["s5_group_dim"], OP["s5_state"], S5_TILE_GROUPS
    col_group = (jnp.arange(4 * gt * p) // p) % gt
    mask_w = ((jnp.arange(gt * n)[:, None] // n) == col_group[None, :]).astype(BF16)
    out_group = (jnp.arange(t * gt * n) // n) % gt
    mask_o = ((jnp.arange(gt * p)[:, None] // p) == out_group[None, :]).astype(BF16)
    return mask_w, mask_o


def _gelu_tanh(y):
    return 0.5 * y * (1.0 + jnp.tanh(0.7978845608028654 * (y + 0.044715 * y * y * y)))


def _s5_kernel(uc_ref, ux_ref, bd_ref, wc_ref, oc_ref, mw_ref, mo_ref, at_ref, oc_out, ox_out,
               mt_scr, wst_scr, wout_scr, s_scr, hin_scr, hend_scr, *, c_ctx, c_lat, half, t, batch):
    lanes = ux_ref.shape[1]
    step = pl.program_id(1)

    @pl.when(step == 0)
    def _():
        for s in range(t):
            for k in range(t):
                mt_scr[s * lanes:(s + 1) * lanes, k * lanes:(k + 1) * lanes] = bd_ref[0, k - s + t - 1]
            rep = lanes // wc_ref.shape[2]
            wst_scr[s * lanes:(s + 1) * lanes, :] = jnp.tile(wc_ref[0, s], (rep, 1)) * mw_ref[...]
        rows = mo_ref.shape[0]
        for q in range(oc_ref.shape[1]):
            rep = rows // oc_ref.shape[2]
            wout_scr[q * rows:(q + 1) * rows, :] = jnp.tile(oc_ref[0, q], (rep, 1)) * mo_ref[...]

    at = at_ref[0]
    afr, afi = at[:, 0:half], at[:, half:2 * half]
    abr, abi = at[:, 2 * half:3 * half], at[:, 3 * half:4 * half]

    def run(u_ref, o_ref, n_seq, n_chunks, h0_of):
        rows = n_seq * n_chunks
        u = jnp.concatenate([u_ref[pl.ds(s, rows, stride=t), :] for s in range(t)],
                            axis=1).astype(BF16)
        s_scr[0:rows, :] = jnp.dot(u, wst_scr[...], preferred_element_type=F32)
        ends = []
        for q in range(n_seq):
            h0 = h0_of(q)
            init = (h0[:, 0:half], h0[:, half:2 * half], h0[:, 2 * half:3 * half],
                    h0[:, 3 * half:4 * half])
            lo = q * n_chunks

            def body(i, carry, lo=lo):
                hfr, hfi, hbr, hbi = carry
                cf = lo + i
                cb = lo + n_chunks - 1 - i
                hin_scr[pl.ds(cf, 1), 0:half] = hfr
                hin_scr[pl.ds(cf, 1), half:2 * half] = hfi
                hin_scr[pl.ds(cb, 1), 2 * half:3 * half] = hbr
                hin_scr[pl.ds(cb, 1), 3 * half:4 * half] = hbi
                sf = s_scr[pl.ds(cf, 1), 0:2 * half]
                sb = s_scr[pl.ds(cb, 1), 2 * half:4 * half]
                nfr = afr * hfr - afi * hfi + sf[:, 0:half]
                nfi = afr * hfi + afi * hfr + sf[:, half:2 * half]
                nbr = abr * hbr - abi * hbi + sb[:, 0:half]
                nbi = abr * hbi + abi * hbr + sb[:, half:2 * half]
                return nfr, nfi, nbr, nbi

            ends.append(jnp.concatenate(lax.fori_loop(0, n_chunks, body, init), axis=1))
        y = (jnp.dot(u, mt_scr[...], preferred_element_type=F32)
             + jnp.dot(hin_scr[0:rows, :].astype(BF16), wout_scr[...], preferred_element_type=F32))
        for k in range(t):
            o_ref[pl.ds(k, rows, stride=t), :] = _gelu_tanh(
                y[:, k * lanes:(k + 1) * lanes]).astype(o_ref.dtype)
        return ends

    @pl.when(step == 0)
    def _():
        zero = jnp.zeros((1, 4 * half), F32)
        ends = run(uc_ref, oc_out, batch, c_ctx, lambda q: zero)
        for q in range(batch):
            hend_scr[q:q + 1, :] = ends[q]

    @pl.when(step > 0)
    def _():
        run(ux_ref, ox_out, 1, c_lat, lambda q: hend_scr[pl.ds(step - 1, 1), :])


def _s5(uc2, ux2, ds5, batch, ops, layer):
    bd, wc, oc, mask_w, mask_o, at = ops
    mc, mx = uc2.shape[0], ux2.shape[0]
    lc, lx = mc // batch, mx // batch
    t = S5_CHUNK
    cc, cx = lc // t, lx // t
    j = ds5 // V7X_LANES
    w = t * V7X_LANES
    sw = wc.shape[4]
    compact = sum(_nbytes(a.shape[2:], BF16) for a in (bd, wc, oc)) + _nbytes(mask_w.shape, BF16) \
        + _nbytes(mask_o.shape, BF16)
    est = (3 * _nbytes((w, sw), BF16) + 2 * compact + 4 * _nbytes((lx + lc, V7X_LANES), F32)
           + 2 * _nbytes((cx, sw), F32) + 3 * _nbytes((cx, w), F32) + _nbytes((cx, sw), BF16)
           + 2 * _nbytes((cx, w), BF16))
    ctx_spec = pl.BlockSpec((mc, V7X_LANES), lambda jj, s: (0, jj))
    lat_spec = pl.BlockSpec((lx, V7X_LANES), lambda jj, s: (jnp.maximum(s - 1, 0), jj))
    scan_rows = max(cx, batch * cc)

    def op_spec(a):
        return pl.BlockSpec((None, 1) + a.shape[2:], lambda jj, s: (layer, jj) + (0,) * (a.ndim - 2))

    return pl.pallas_call(
        functools.partial(_s5_kernel, c_ctx=cc, c_lat=cx, half=sw // 4, t=t, batch=batch),
        grid=(j, 1 + batch),
        in_specs=[ctx_spec, lat_spec, op_spec(bd), op_spec(wc), op_spec(oc),
                  pl.BlockSpec(mask_w.shape, lambda jj, s: (0, 0)),
                  pl.BlockSpec(mask_o.shape, lambda jj, s: (0, 0)),
                  op_spec(at)],
        out_specs=[ctx_spec, lat_spec],
        out_shape=[jax.ShapeDtypeStruct((mc, ds5), F32), jax.ShapeDtypeStruct((mx, ds5), F32)],
        scratch_shapes=[pltpu.VMEM((w, w), BF16), pltpu.VMEM((w, sw), BF16), pltpu.VMEM((sw, w), BF16),
                        pltpu.VMEM((scan_rows, sw), F32), pltpu.VMEM((scan_rows, sw), F32),
                        pltpu.VMEM((-(-batch // 8) * 8, sw), F32)],
        compiler_params=_cparams(("arbitrary", "arbitrary"), est),
        name="s5",
    )(uc2, ux2, bd, wc, oc, mask_w, mask_o, at)


def _dft_mats(n):
    hi = n // V7X_LANES
    j = jnp.arange(n, dtype=jnp.int32)[None, :]
    a1 = ((jnp.arange(V7X_LANES, dtype=jnp.int32)[:, None] * j) % n).astype(F32) * (2.0 * math.pi / n)
    a2 = ((jnp.arange(hi, dtype=jnp.int32)[:, None] * j) % hi).astype(F32) * (2.0 * math.pi / hi)
    c1, s1 = jnp.cos(a1)[None, :, :], jnp.sin(a1)[None, :, :]
    c2, s2 = jnp.cos(a2)[:, None, :], jnp.sin(a2)[:, None, :]
    s = 1.0 / math.sqrt(n)
    cos = ((c2 * c1 - s2 * s1) * s).reshape(n, n).astype(BF16)
    sin = ((s2 * c1 + c2 * s1) * s).reshape(n, n).astype(BF16)
    return cos, sin


def _chan_dft_kernel(v_ref, w_ref, xc_ref, xs_ref, *, gd):
    r = jnp.dot(v_ref[...].astype(BF16), w_ref[...], preferred_element_type=F32)
    xc_ref[...] = r[:, :gd].astype(xc_ref.dtype)
    xs_ref[...] = r[:, gd:].astype(xs_ref.dtype)


def _chan_dft(v2, col_off, dft, wc, tm=1024):
    m = v2.shape[0]
    gd = wc.shape[0]
    tm = min(tm, m)
    goff = col_off // gd
    spec = pl.BlockSpec((tm, gd), lambda i, g: (i, g))
    est = (2 * _nbytes((tm, gd), F32) + 5 * _nbytes((tm, gd), BF16) + 2 * _nbytes((gd, 2 * gd), BF16)
           + 2 * _nbytes((tm, 2 * gd), F32))
    return pl.pallas_call(
        functools.partial(_chan_dft_kernel, gd=gd),
        grid=(m // tm, dft // gd),
        in_specs=[pl.BlockSpec((tm, gd), lambda i, g: (i, goff + g)),
                  pl.BlockSpec((gd, 2 * gd), lambda i, g: (0, 0))],
        out_specs=[spec, spec],
        out_shape=[jax.ShapeDtypeStruct((m, dft), BF16)] * 2,
        compiler_params=_cparams(("parallel", "parallel"), est),
        name="chan_dft",
    )(v2, wc)


def _seq_dft_kernel(cl_ref, sl_ref, xc_ref, xs_ref, o_ref):
    y = (jnp.dot(cl_ref[...], xc_ref[0], preferred_element_type=F32)
         - jnp.dot(sl_ref[...], xs_ref[0], preferred_element_type=F32))
    o_ref[0] = y.astype(o_ref.dtype)


def _seq_dft(xc, xs, cl, sl, batch, tm=512, tn=512):
    m, dft = xc.shape
    l = m // batch
    tm = min(tm, l)
    tn = min(tn, dft)
    xc3, xs3 = xc.reshape(batch, l, dft), xs.reshape(batch, l, dft)
    est = 4 * _nbytes((tm, l), BF16) + 4 * _nbytes((l, tn), BF16) + 4 * _nbytes((tm, tn), F32)
    out = pl.pallas_call(
        _seq_dft_kernel,
        grid=(batch, dft // tn, l // tm),
        in_specs=[
            pl.BlockSpec((tm, l), lambda b, jn, i: (i, 0)),
            pl.BlockSpec((tm, l), lambda b, jn, i: (i, 0)),
            pl.BlockSpec((1, l, tn), lambda b, jn, i: (b, 0, jn)),
            pl.BlockSpec((1, l, tn), lambda b, jn, i: (b, 0, jn)),
        ],
        out_specs=pl.BlockSpec((1, tm, tn), lambda b, jn, i: (b, i, jn)),
        out_shape=jax.ShapeDtypeStruct((batch, l, dft), BF16),
        compiler_params=_cparams(("parallel", "parallel", "parallel"), est),
        name="seq_dft",
    )(cl, sl, xc3, xs3)
    return out.reshape(m, dft)


def _prefix_count(mask_f32, tri):
    e, l = mask_f32.shape
    carry = jnp.zeros((e, 1), F32)
    outs = []
    for k in range(l // CUMSUM_BLOCK):
        blk = mask_f32[:, k * CUMSUM_BLOCK:(k + 1) * CUMSUM_BLOCK]
        outs.append(jnp.dot(blk.astype(BF16), tri, preferred_element_type=F32) + carry)
        carry = carry + jnp.sum(blk, axis=1, keepdims=True)
    return outs[0] if len(outs) == 1 else jnp.concatenate(outs, axis=1)


def _route_kernel(lg_ref, posm_ref, aff_ref, bnd_ref, *, cap, tile):
    lg = lg_ref[...]
    mx = jnp.max(lg, axis=0, keepdims=True)
    ex = jnp.exp(lg - mx)
    aff = ex / jnp.sum(ex, axis=0, keepdims=True)
    bits = pltpu.bitcast(aff, jnp.int32)
    e = lg.shape[0]
    v = jnp.zeros((e, 1), jnp.int32)
    for bit in range(30, -1, -1):
        cand = v | (1 << bit)
        cnt = jnp.sum(jnp.where(bits >= cand, 1.0, 0.0), axis=1, keepdims=True)
        v = jnp.where(cnt >= cap, cand, v)
    gt = bits > v
    eq = bits == v
    n_gt = jnp.sum(gt.astype(F32), axis=1, keepdims=True)
    r = lax.broadcasted_iota(jnp.int32, (CUMSUM_BLOCK, CUMSUM_BLOCK), 0)
    c = lax.broadcasted_iota(jnp.int32, (CUMSUM_BLOCK, CUMSUM_BLOCK), 1)
    tri = jnp.where(r < c, 1.0, 0.0).astype(BF16)
    eq_f = jnp.where(eq, 1.0, 0.0)
    tie_ok = _prefix_count(eq_f, tri) < (cap - n_gt)
    sel_f = jnp.where(gt, 1.0, jnp.where(tie_ok, eq_f, 0.0))
    pos = _prefix_count(sel_f, tri)
    posm_ref[0] = jnp.where(sel_f > 0.5, pos, -1.0).astype(jnp.int32)
    aff_ref[0] = aff
    l = lg.shape[1]
    starts = [pos[:, k * tile:k * tile + 1] for k in range(l // tile)]
    starts.append(jnp.full((e, 1), float(cap), F32))
    bnd_ref[0] = jnp.concatenate(starts, axis=1).astype(jnp.int32)


def _route(lg_t, batch, cap, tile):
    e, m = lg_t.shape
    l = m // batch
    nt = l // tile
    est = 16 * _nbytes((e, l), F32)
    return pl.pallas_call(
        functools.partial(_route_kernel, cap=cap, tile=tile),
        grid=(batch,),
        in_specs=[pl.BlockSpec((e, l), lambda b: (0, b))],
        out_specs=[pl.BlockSpec((1, e, l), lambda b: (b, 0, 0))] * 2
        + [pl.BlockSpec((1, e, nt + 1), lambda b: (b, 0, 0))],
        out_shape=[jax.ShapeDtypeStruct((batch, e, l), jnp.int32),
                   jax.ShapeDtypeStruct((batch, e, l), F32),
                   jax.ShapeDtypeStruct((batch, e, nt + 1), jnp.int32)],
        compiler_params=_cparams(("parallel",), est),
        name="route",
    )(lg_t)


def _slots_kernel(posm_ref, aff_ref, *rest, cap):
    idx_ref, gs_ref = rest[-2], rest[-1]
    pm = posm_ref[0, 0]
    l = pm.shape[1]
    slot = lax.broadcasted_iota(jnp.int32, (cap, l), 0)
    tok = lax.broadcasted_iota(jnp.int32, (1, l), 1).astype(F32)
    hit = pm == slot
    idx_ref[0, 0] = jnp.sum(jnp.where(hit, tok, 0.0), axis=1, keepdims=True).astype(jnp.int32)
    gs_ref[0] = jnp.sum(jnp.where(hit, aff_ref[0, 0], 0.0), axis=1, keepdims=True)


def _slots(posm, aff, cap, n_rows, row_off, gs_all=None):
    batch, e, l = posm.shape
    blk = row_off // cap
    in_specs = [pl.BlockSpec((1, 1, 1, l), lambda b, ee: (b, ee, 0, 0))] * 2
    args = [posm.reshape(batch, e, 1, l), aff.reshape(batch, e, 1, l)]
    aliases = {}
    if gs_all is not None:
        in_specs.append(pl.BlockSpec(memory_space=pl.ANY))
        args.append(gs_all)
        aliases = {2: 1}
    est = 6 * _nbytes((cap, l), F32) + 4 * _nbytes((cap, V7X_LANES), F32)
    return pl.pallas_call(
        functools.partial(_slots_kernel, cap=cap),
        grid=(batch, e),
        in_specs=in_specs,
        out_specs=[pl.BlockSpec((1, 1, cap, 1), lambda b, ee: (b, ee, 0, 0)),
                   pl.BlockSpec((1, cap, 1), lambda b, ee: (ee, blk + b, 0))],
        out_shape=[jax.ShapeDtypeStruct((batch, e, cap, 1), jnp.int32),
                   jax.ShapeDtypeStruct((e, n_rows, 1), F32)],
        input_output_aliases=aliases,
        compiler_params=_cparams(("parallel", "parallel"), est),
        name="moe_slots",
    )(*args)


GATHER_ISSUE_UNROLL = 8


def _gather_kernel(idx_ref, hp_hbm, *rest, cap, seq, n_exp):
    o_ref, sem = rest[-2], rest[-1]
    b, ee = pl.program_id(0), pl.program_id(1)
    base = (b * n_exp + ee) * cap
    row0 = b * seq

    def issue(c, carry):
        for k in range(GATHER_ISSUE_UNROLL):
            s = c * GATHER_ISSUE_UNROLL + k
            pltpu.make_async_copy(hp_hbm.at[pl.ds(row0 + idx_ref[base + s], 1)],
                                  o_ref.at[0, pl.ds(s, 1)], sem).start()
        return carry

    lax.fori_loop(0, cap // GATHER_ISSUE_UNROLL, issue, 0)
    pltpu.make_async_copy(hp_hbm.at[pl.ds(0, cap)], o_ref.at[0], sem).wait()


def _gather(idx, hp, batch, cap, n_rows, row_off, xg_all=None):
    e = idx.shape[1]
    m, dh = hp.shape
    seq = m // batch
    blk = row_off // cap
    in_specs = [pl.BlockSpec(memory_space=pl.ANY)]
    args = [idx.reshape(-1), hp]
    aliases = {}
    if xg_all is not None:
        in_specs.append(pl.BlockSpec(memory_space=pl.ANY))
        args.append(xg_all)
        aliases = {2: 0}
    est = 2 * _nbytes((cap, dh), jnp.uint32)
    return pl.pallas_call(
        functools.partial(_gather_kernel, cap=cap, seq=seq, n_exp=e),
        grid_spec=pltpu.PrefetchScalarGridSpec(
            num_scalar_prefetch=1,
            grid=(batch, e),
            in_specs=in_specs,
            out_specs=pl.BlockSpec((1, cap, dh), lambda b, ee, ix: (ee, blk + b, 0)),
            scratch_shapes=[pltpu.SemaphoreType.DMA(())],
        ),
        out_shape=jax.ShapeDtypeStruct((e, n_rows, dh), jnp.uint32),
        input_output_aliases=aliases,
        compiler_params=_cparams(("arbitrary", "arbitrary"), est),
        name="moe_gather",
    )(*args)


EXPERT_ROW_BLOCKS = 2


def _up_kernel(xg_ref, wg_ref, wu_ref, o_ref, wg_bf, wu_bf):
    @pl.when(pl.program_id(2) == 0)
    def _():
        wg_bf[...] = wg_ref[0, 0].astype(BF16)
        wu_bf[...] = wu_ref[0, 0].astype(BF16)

    x_lo, x_hi = _unpack_bf16_pairs(xg_ref[0])
    half = x_lo.shape[1]
    g = (jnp.dot(x_lo, wg_bf[0:half, :], preferred_element_type=F32)
         + jnp.dot(x_hi, wg_bf[half:, :], preferred_element_type=F32))
    u = (jnp.dot(x_lo, wu_bf[0:half, :], preferred_element_type=F32)
         + jnp.dot(x_hi, wu_bf[half:, :], preferred_element_type=F32))
    o_ref[0] = (g * jax.nn.sigmoid(g) * u).astype(o_ref.dtype)


def _expert_up(xg, w_gate, w_up, layer, tf=256):
    e, r, dh = xg.shape
    d = 2 * dh
    f = w_gate.shape[3]
    tf = min(tf, f)
    tr = r // EXPERT_ROW_BLOCKS
    est = (4 * _nbytes((d, tf), F32) + 3 * _nbytes((d, tf), BF16) + 2 * _nbytes((tr, d), BF16)
           + 6 * _nbytes((tr, tf), F32))
    return pl.pallas_call(
        _up_kernel,
        grid=(e, f // tf, EXPERT_ROW_BLOCKS),
        in_specs=[
            pl.BlockSpec((1, tr, dh), lambda ee, fj, rr: (ee, rr, 0)),
            pl.BlockSpec((1, 1, d, tf), lambda ee, fj, rr: (layer, ee, 0, fj)),
            pl.BlockSpec((1, 1, d, tf), lambda ee, fj, rr: (layer, ee, 0, fj)),
        ],
        out_specs=pl.BlockSpec((1, tr, tf), lambda ee, fj, rr: (ee, rr, fj)),
        out_shape=jax.ShapeDtypeStruct((e, r, f), BF16),
        scratch_shapes=[pltpu.VMEM((d, tf), BF16), pltpu.VMEM((d, tf), BF16)],
        compiler_params=_cparams(("parallel", "parallel", "arbitrary"), est),
        name="moe_up",
    )(xg, w_gate, w_up)


def _down_kernel(h_ref, wd_ref, gs_ref, o_ref, wd_bf):
    @pl.when(pl.program_id(2) == 0)
    def _():
        wd_bf[...] = wd_ref[0, 0].astype(BF16)

    y = jnp.dot(h_ref[0], wd_bf[...], preferred_element_type=F32)
    o_ref[0] = (y * gs_ref[0]).astype(o_ref.dtype)


def _expert_down(hid, w_down, gslot, layer, td=2048):
    e, r, f = hid.shape
    d = w_down.shape[3]
    td = min(td, d)
    tr = r // EXPERT_ROW_BLOCKS
    est = (2 * _nbytes((f, td), F32) + 2 * _nbytes((f, td), BF16) + 2 * _nbytes((tr, f), BF16)
           + 4 * _nbytes((tr, td), F32) + 2 * _nbytes((tr, V7X_LANES), F32))
    return pl.pallas_call(
        _down_kernel,
        grid=(e, d // td, EXPERT_ROW_BLOCKS),
        in_specs=[
            pl.BlockSpec((1, tr, f), lambda ee, dj, rr: (ee, rr, 0)),
            pl.BlockSpec((1, 1, f, td), lambda ee, dj, rr: (layer, ee, 0, dj)),
            pl.BlockSpec((1, tr, 1), lambda ee, dj, rr: (ee, rr, 0)),
        ],
        out_specs=pl.BlockSpec((1, tr, td), lambda ee, dj, rr: (ee, rr, dj)),
        out_shape=jax.ShapeDtypeStruct((e, r, d), BF16),
        scratch_shapes=[pltpu.VMEM((f, td), BF16)],
        compiler_params=_cparams(("parallel", "parallel", "arbitrary"), est),
        name="moe_down",
    )(hid, w_down, gslot)


COMBINE_TILE = 256
DMA_ROW_ALIGN = 16


def combine_window(cap, seq, tile):
    return min(cap, max(DMA_ROW_ALIGN, 2 * tile * cap // seq))


def _combine_kernel(tbl_ref, pt_ref, y_hbm, x_ref, g_ref, *rest, n_exp, cap, win, nt, row_off,
                    n_steps, emit_x, post_norm, post_mod, eps):
    stage, extra, acc_scr, sems, xsem = rest[-5:]
    n_post_in = (1 + 2 * post_mod) if post_norm else 0
    post_in = rest[:n_post_in]
    outs = rest[n_post_in:-5]
    n = pl.program_id(0) * nt + pl.program_id(1)
    cur = n % 2
    nxt_step = jnp.minimum(n + 1, n_steps - 1)

    def window_starts(step):
        b, i = step // nt, step % nt
        t0 = (b * (nt + 1) + i) * n_exp
        out = []
        for ee in range(n_exp):
            s0a = jnp.minimum((tbl_ref[t0 + ee] // DMA_ROW_ALIGN) * DMA_ROW_ALIGN, cap - win)
            out.append(pl.multiple_of(s0a, DMA_ROW_ALIGN))
        return row_off + b * cap, out, t0

    def issue_windows(step, buf):
        row_base, starts, _ = window_starts(step)
        for ee in range(n_exp):
            pltpu.make_async_copy(y_hbm.at[ee, pl.ds(row_base + starts[ee], win), :],
                                  stage.at[buf, pl.ds(ee * win, win), :], sems.at[buf]).start()

    def wait_windows(buf):
        pltpu.make_async_copy(y_hbm.at[0, pl.ds(0, n_exp * win), :], stage.at[buf],
                              sems.at[buf]).wait()

    @pl.when(n == 0)
    def _():
        issue_windows(0, 0)

    issue_windows(nxt_step, 1 - cur)
    row_base, starts, t0 = window_starts(n)
    pt = pt_ref[...]
    lane = lax.broadcasted_iota(jnp.int32, (pt.shape[0], win), 1)
    p = jnp.concatenate(
        [jnp.where(pt[:, ee:ee + 1] - starts[ee] == lane, 1.0, 0.0).astype(BF16)
         for ee in range(n_exp)], axis=1)
    wait_windows(cur)
    acc_scr[...] = jnp.dot(p, stage[cur], preferred_element_type=F32)

    for ee in range(n_exp):
        first = starts[ee] + win
        n_extra = jnp.maximum(0, (tbl_ref[t0 + n_exp + ee] - first + win - 1) // win)

        def extra_window(w, carry, ee=ee, first=first):
            lo = first + w * win
            src0 = pl.multiple_of(jnp.minimum(lo, cap - win), DMA_ROW_ALIGN)
            cp = pltpu.make_async_copy(y_hbm.at[ee, pl.ds(row_base + src0, win), :], extra, xsem)
            cp.start()
            cp.wait()
            col = pt[:, ee:ee + 1]
            hit = jnp.where(col >= lo, col - src0, -1) == lane
            acc_scr[...] += jnp.dot(jnp.where(hit, 1.0, 0.0).astype(BF16), extra[...],
                                    preferred_element_type=F32)
            return carry

        lax.fori_loop(0, n_extra, extra_window, 0)

    xn = x_ref[...] + g_ref[0] * acc_scr[...]
    if emit_x:
        outs[0][...] = xn
    if post_norm:
        shift, scale = (post_in[1][0], post_in[2][0]) if post_mod else (None, None)
        h_ref = outs[-1]
        h_ref[...] = _rms_modulate(xn, post_in[0][...], shift, scale, eps).astype(h_ref.dtype)

    @pl.when(n == n_steps - 1)
    def _():
        wait_windows(1 - cur)


def _combine(posm_t, bounds, y, x2, gate, batch, cap, row_off, post=None, emit_x=True):
    m, d = x2.shape
    l = m // batch
    e = posm_t.shape[1]
    tm = min(COMBINE_TILE, l)
    nt = l // tm
    win = combine_window(cap, l, tm)
    assert cap % DMA_ROW_ALIGN == 0 and win % DMA_ROW_ALIGN == 0 and bounds.shape == (batch, e, nt + 1)
    tbl = bounds.transpose(0, 2, 1).reshape(-1)
    est = (2 * _nbytes((e * win, d), BF16) + _nbytes((win, d), BF16) + 9 * _nbytes((tm, d), F32)
           + 2 * _nbytes((tm, e * win), BF16) + 2 * _nbytes((tm, V7X_LANES), F32))
    row_spec = pl.BlockSpec((tm, d), lambda b, i, t: (b * nt + i, 0))
    vec_spec = pl.BlockSpec((1, 1, d), lambda b, i, t: (b, 0, 0))
    in_specs = [pl.BlockSpec((tm, e), lambda b, i, t: (b * nt + i, 0)),
                pl.BlockSpec(memory_space=pl.ANY), row_spec, vec_spec]
    args = [tbl, posm_t, y, x2, gate]
    out_specs, out_shape = [], []
    if emit_x:
        out_specs.append(row_spec)
        out_shape.append(jax.ShapeDtypeStruct((m, d), F32))
    post_mod = False
    if post is not None:
        p_gain, p_shift, p_scale, p_dtype = post
        post_mod = p_shift is not None
        in_specs.append(pl.BlockSpec((1, d), lambda b, i, t: (0, 0)))
        args.append(p_gain.reshape(1, d))
        if post_mod:
            in_specs += [vec_spec, vec_spec]
            args += [p_shift, p_scale]
        out_specs.append(row_spec)
        out_shape.append(jax.ShapeDtypeStruct((m, d), p_dtype))
        est += 2 * _nbytes((tm, d), p_dtype)
    return pl.pallas_call(
        functools.partial(_combine_kernel, n_exp=e, cap=cap, win=win, nt=nt, row_off=row_off,
                          n_steps=batch * nt, emit_x=emit_x, post_norm=post is not None,
                          post_mod=post_mod, eps=OP["rms_eps"]),
        grid_spec=pltpu.PrefetchScalarGridSpec(
            num_scalar_prefetch=1,
            grid=(batch, nt),
            in_specs=in_specs,
            out_specs=out_specs,
            scratch_shapes=[pltpu.VMEM((2, e * win, d), BF16), pltpu.VMEM((win, d), BF16),
                            pltpu.VMEM((tm, d), F32), pltpu.SemaphoreType.DMA((2,)),
                            pltpu.SemaphoreType.DMA(())],
        ),
        out_shape=out_shape,
        input_output_aliases={3: 0} if emit_x else {},
        compiler_params=_cparams(("arbitrary", "arbitrary"), est),
        name="moe_combine",
    )(*args)


def _sigmoid(z):
    return jax.nn.sigmoid(z)


def _in_proj(h2, w_in, layer, n_cols):
    return _mm([(h2, w_in, layer, 0)], [], lambda accs, ex: accs[0], n_cols, F32, 1024,
               math.gcd(512, n_cols), name="in_proj")


def _mixer_out(x2, a2, px, g_mix, wts, layer, dfts, batch, rpb):
    w_glu, w_s5o, w_fto, w_out = wts
    wc, cl, sl = dfts
    d = x2.shape[1]
    d_s5 = a2.shape[1]
    d_ft = w_fto.shape[1]
    xc, xs = _chan_dft(px, d_s5, d_ft, wc)
    yf = _seq_dft(xc, xs, cl, sl, batch)
    glu = _mm([(a2, w_glu, layer, 0)], [("tile", a2, 0)],
              lambda accs, ex: ex[0] * _sigmoid(accs[0]),
              d_s5, BF16, 1024, 512, name="glu")
    tn = math.gcd(512, d_s5 + d_ft, d)
    g0 = (d_s5 + d_ft) // tn
    merged = _mm([(glu, w_s5o, layer, 0), (yf, w_fto, layer, 0)],
                 [("tile", px, g0), ("tile", px, g0 + d // tn)],
                 lambda accs, ex: _sigmoid(ex[0]) * accs[0] + _sigmoid(ex[1]) * accs[1],
                 d, BF16, 1024, tn, name="merge")
    return _mm([(merged, w_out, layer, 0)], [("tile", x2, 0), ("row", g_mix, 0)],
               lambda accs, ex: ex[0] + ex[1] * accs[0],
               d, F32, 1024, tn, rows_per_batch=rpb, alias_extra=0, name="out_proj")


def _moe(streams, w_gate, w_up, w_down, layer):
    e = streams[0][4].shape[0]
    caps = [OP["capacity_factor"] * (s[0].shape[0] // s[1]) // e for s in streams]
    offs, n_rows = [], 0
    for s, cap in zip(streams, caps):
        assert n_rows % cap == 0
        offs.append(n_rows)
        n_rows += s[1] * cap
    xg = gslot = None
    if len(streams) > 1:
        xg = jnp.zeros((e, n_rows, streams[0][2].shape[1]), jnp.uint32)
        gslot = jnp.zeros((e, n_rows, 1), F32)
    routed = []
    for (x2, batch, hp, _, lg_t, _, _), cap, off in zip(streams, caps, offs):
        tile = min(COMBINE_TILE, x2.shape[0] // batch)
        posm, aff, bounds = _route(lg_t, batch, cap, tile)
        idx, gslot = _slots(posm, aff, cap, n_rows, off, gslot)
        xg = _gather(idx, hp, batch, cap, n_rows, off, xg)
        routed.append((posm, bounds))
    y = _expert_down(_expert_up(xg, w_gate, w_up, layer), w_down, gslot, layer)
    outs = []
    for (x2, batch, _, gate, _, post, emit_x), cap, off, (posm, bounds) in zip(
            streams, caps, offs, routed):
        posm_t = posm.transpose(0, 2, 1).reshape(x2.shape[0], e)
        outs.append(_combine(posm_t, bounds, y, x2, gate, batch, cap, off, post, emit_x))
    return outs


def _grid_posembed(n_tok, d):
    gw = OP["grid_w"]
    rows = n_tok // gw
    quarter = d // 4
    inv_freq = OP["pos_base"] ** (-jnp.arange(quarter, dtype=F32) / quarter)
    ang_r = jnp.arange(rows, dtype=F32)[:, None] * inv_freq
    ang_c = jnp.arange(gw, dtype=F32)[:, None] * inv_freq
    return (jnp.concatenate([jnp.sin(ang_r), jnp.cos(ang_r)], axis=-1),
            jnp.concatenate([jnp.sin(ang_c), jnp.cos(ang_c)], axis=-1))


def kernel(x, c, ctx, c_ctx, ada_w, ada_b, norm_mix_g, norm_ffn_g, w_in, s5_lam_re, s5_lam_im,
           s5_log_dt, s5_b_re, s5_b_im, s5_c_re, s5_c_im, s5_d, w_glu, w_s5_out, w_ft_out, w_out,
           w_router, w_gate, w_up, w_down, norm_final_g):
    batch, seq, d = x.shape
    ctx_len = ctx.shape[1]
    depth = ada_w.shape[0]
    d_s5 = w_glu.shape[1]
    d_ft = w_ft_out.shape[1]
    n_mod = OP["n_mod"]
    m_x, m_c = batch * seq, batch * ctx_len

    x2 = x.reshape(m_x, d)
    pos = tuple(p.astype(x.dtype) for p in _grid_posembed(seq, d))
    c2 = ctx.reshape(m_c, d)

    rows = -(-(batch + 1) // 8) * 8
    c8 = jnp.zeros((rows, d), F32).at[:batch].set(c).at[batch].set(c_ctx)
    mod = _adaln(c8, ada_w, ada_b)

    gd = d_ft // OP["ft_groups"]
    wc_c, wc_s = _dft_mats(gd)
    wc = jnp.concatenate([wc_c, wc_s], axis=1)
    dft_x = (wc,) + _dft_mats(seq)
    dft_c = (wc,) + _dft_mats(ctx_len)

    bd, wc_s5, oc_s5, at = jax.vmap(lambda *p: _s5_operators(*p, S5_CHUNK))(
        s5_lam_re, s5_lam_im, s5_log_dt, s5_b_re, s5_b_im, s5_c_re, s5_c_im, s5_d)
    s5_ops = (bd, wc_s5, oc_s5) + _s5_masks(S5_CHUNK) + (at,)
    d_in = w_in.shape[2]
    mix_w = (w_glu, w_s5_out, w_ft_out, w_out)

    mods_x = [[mod[i, :batch, k * d:(k + 1) * d].reshape(batch, 1, d) for k in range(n_mod)]
              for i in range(depth)]
    mods_c = [[jnp.broadcast_to(mod[i, batch:batch + 1, k * d:(k + 1) * d].reshape(1, 1, d),
                                (batch, 1, d)) for k in range(n_mod)] for i in range(depth)]

    hx = hc = None
    for i in range(depth):
        last = i == depth - 1
        mx, mc = mods_x[i], mods_c[i]
        wr_t = w_router[i].T.astype(BF16)
        if i == 0:
            hc = _norm(c2, norm_mix_g[i], mc[0], mc[1], rows_per_batch=ctx_len)
            hx, x2 = _norm(x2, norm_mix_g[i], mx[0], mx[1], rows_per_batch=seq, pos=pos)
        pc = _in_proj(hc, w_in, i, d_s5 if last else d_in)
        px = _in_proj(hx, w_in, i, d_in)
        ac, ax = _s5(pc, px, d_s5, batch, s5_ops, i)
        x2 = _mixer_out(x2, ax, px, mx[2], mix_w, i, dft_x, batch, seq)
        lgx, hpx = _norm(x2, norm_ffn_g[i], mx[3], mx[4], rows_per_batch=seq, wr_t=wr_t,
                         emit_h=False, pack=True)
        if last:
            post_x, post_c = (norm_final_g, None, None, x.dtype), None
        else:
            post_x = (norm_mix_g[i + 1], mods_x[i + 1][0], mods_x[i + 1][1], BF16)
            post_c = (norm_mix_g[i + 1], mods_c[i + 1][0], mods_c[i + 1][1], BF16)
        streams = [(x2, batch, hpx, mx[5], lgx, post_x, not last)]
        if not last:
            c2 = _mixer_out(c2, ac, pc, mc[2], mix_w, i, dft_c, batch, ctx_len)
            lgc, hpc = _norm(c2, norm_ffn_g[i], mc[3], mc[4], rows_per_batch=ctx_len, wr_t=wr_t,
                             emit_h=False, pack=True)
            streams.append((c2, batch, hpc, mc[5], lgc, post_c, True))
        outs = _moe(streams, w_gate, w_up, w_down, i)
        if last:
            return outs[0][0].reshape(batch, seq, d)
        x2, hx = outs[0]
        c2, hc = outs[1]
```

```python
import functools
import math

import jax
import jax.numpy as jnp
from jax import lax
from jax.experimental import pallas as pl
from jax.experimental.pallas import tpu as pltpu

BF16 = jnp.bfloat16
F32 = jnp.float32
HI = lax.Precision.HIGHEST

OP = dict(
    s5_group_dim=16,
    s5_state=64,
    ft_groups=4,
    capacity_factor=2,
    n_mod=6,
    rms_eps=1e-6,
    pos_base=10000.0,
    grid_w=64,
    lambda_re_max=-1e-4,
)

V7X_VMEM_BYTES = 64 * 1024 * 1024
V7X_LANES = 128
S5_TILE_GROUPS = V7X_LANES // 16
S5_CHUNK = 16
CUMSUM_BLOCK = 256


def _cparams(sem, vmem_est):
    limit = int(min(max(vmem_est * 5 // 4 + (4 << 20), 32 << 20), V7X_VMEM_BYTES - (6 << 20)))
    return pltpu.CompilerParams(dimension_semantics=sem, vmem_limit_bytes=limit)


def _nbytes(shape, dtype):
    return math.prod(shape) * jnp.dtype(dtype).itemsize


def _adaln_kernel(c_ref, w_ref, b_ref, o_ref):
    cv = c_ref[...]
    a = (cv * jax.nn.sigmoid(cv)).astype(BF16)
    o_ref[0] = jnp.dot(a, w_ref[0].astype(BF16), preferred_element_type=F32) + b_ref[0]


def _adaln(c8, ada_w, ada_b, tn=512):
    depth, d, n6 = ada_w.shape
    rows = c8.shape[0]
    tn = min(tn, n6)
    est = 2 * _nbytes((d, tn), F32) + 4 * _nbytes((rows, tn), F32) + _nbytes((rows, d), F32) * 2
    return pl.pallas_call(
        _adaln_kernel,
        grid=(depth, n6 // tn),
        in_specs=[
            pl.BlockSpec((rows, d), lambda l, j: (0, 0)),
            pl.BlockSpec((1, d, tn), lambda l, j: (l, 0, j)),
            pl.BlockSpec((1, 1, tn), lambda l, j: (l, 0, j)),
        ],
        out_specs=pl.BlockSpec((1, rows, tn), lambda l, j: (l, 0, j)),
        out_shape=jax.ShapeDtypeStruct((depth, rows, n6), F32),
        compiler_params=_cparams(("parallel", "parallel"), est),
        name="adaln",
    )(c8, ada_w, ada_b.reshape(depth, 1, n6))


def _rms_modulate(xf, g, shift, scale, eps):
    ms = jnp.mean(xf * xf, axis=-1, keepdims=True)
    y = xf * lax.rsqrt(ms + eps) * g
    if shift is not None:
        y = y * (1.0 + scale) + shift
    return y


def _pack_bf16_pairs(y):
    half = y.shape[1] // 2
    lo = pltpu.bitcast(y[:, :half].astype(BF16).astype(F32), jnp.uint32) >> 16
    hi = pltpu.bitcast(y[:, half:].astype(BF16).astype(F32), jnp.uint32) & jnp.uint32(0xFFFF0000)
    return hi | lo


def _unpack_bf16_pairs(p):
    lo = pltpu.bitcast(p << 16, F32).astype(BF16)
    hi = pltpu.bitcast(p & jnp.uint32(0xFFFF0000), F32).astype(BF16)
    return lo, hi


def _norm_kernel(*refs, eps, modulate, router, add_pos, emit_h, pack):
    x_ref, g_ref = refs[0], refs[1]
    k = 2
    xf = x_ref[...]
    if add_pos:
        pr, pc = refs[k][0], refs[k + 1][...]
        k += 2
        gw = pc.shape[0]
        pos = jnp.concatenate(
            [jnp.concatenate([jnp.broadcast_to(pr[q:q + 1], pc.shape), pc], axis=1)
             for q in range(xf.shape[0] // gw)], axis=0)
        xf = xf + pos
        refs[-1][...] = xf
    shift = scale = None
    if modulate:
        shift, scale = refs[k][0], refs[k + 1][0]
        k += 2
    y = _rms_modulate(xf, g_ref[...], shift, scale, eps)
    if router:
        wr_ref = refs[k]
        k += 1
    if emit_h:
        refs[k][...] = y.astype(refs[k].dtype)
        k += 1
    if router:
        refs[k][...] = lax.dot_general(
            wr_ref[...], y.astype(BF16), (((1,), (1,)), ((), ())), preferred_element_type=F32)
        k += 1
    if pack:
        refs[k][...] = _pack_bf16_pairs(y)


def _norm(x2, g, shift=None, scale=None, rows_per_batch=None, wr_t=None, out_dtype=BF16, tm=256,
          pos=None, emit_h=True, pack=False):
    m, d = x2.shape
    modulate = shift is not None
    router = wr_t is not None
    add_pos = pos is not None
    rpb = rows_per_batch if rows_per_batch is not None else m
    tm = min(tm, rpb)
    tpb = rpb // tm
    in_specs = [pl.BlockSpec((tm, d), lambda i: (i, 0)), pl.BlockSpec((1, d), lambda i: (0, 0))]
    args = [x2, g.reshape(1, d)]
    if add_pos:
        pos_rows, pos_cols = pos
        gw = pos_cols.shape[0]
        assert tm % gw == 0 and rpb % tm == 0
        in_specs.append(pl.BlockSpec((1, tm // gw, d // 2), lambda i: (i % tpb, 0, 0)))
        in_specs.append(pl.BlockSpec((gw, d // 2), lambda i: (0, 0)))
        args += [pos_rows.reshape(rpb // tm, tm // gw, d // 2), pos_cols]
    if modulate:
        in_specs += [pl.BlockSpec((1, 1, d), lambda i: (i // tpb, 0, 0))] * 2
        args += [shift, scale]
    out_specs, out_shape = [], []
    if emit_h:
        out_specs.append(pl.BlockSpec((tm, d), lambda i: (i, 0)))
        out_shape.append(jax.ShapeDtypeStruct((m, d), out_dtype))
    if router:
        e = wr_t.shape[0]
        in_specs.append(pl.BlockSpec((e, d), lambda i: (0, 0)))
        args.append(wr_t)
        out_specs.append(pl.BlockSpec((e, tm), lambda i: (0, i)))
        out_shape.append(jax.ShapeDtypeStruct((e, m), F32))
    est = 2 * _nbytes((tm, d), F32) + 2 * _nbytes((tm, d), out_dtype) + 3 * _nbytes((tm, d), F32)
    if pack:
        out_specs.append(pl.BlockSpec((tm, d // 2), lambda i: (i, 0)))
        out_shape.append(jax.ShapeDtypeStruct((m, d // 2), jnp.uint32))
        est += 4 * _nbytes((tm, d // 2), F32)
    if add_pos:
        out_specs.append(pl.BlockSpec((tm, d), lambda i: (i, 0)))
        out_shape.append(jax.ShapeDtypeStruct((m, d), F32))
        est += 4 * _nbytes((tm, d), F32)
    res = pl.pallas_call(
        functools.partial(_norm_kernel, eps=OP["rms_eps"], modulate=modulate, router=router,
                          add_pos=add_pos, emit_h=emit_h, pack=pack),
        grid=(m // tm,),
        in_specs=in_specs,
        out_specs=out_specs,
        out_shape=out_shape,
        compiler_params=_cparams(("parallel",), est),
        name="rmsnorm",
    )(*args)
    return res if len(res) > 1 else res[0]


def _mm_kernel(*refs, n_pairs, kinds, epilogue):
    n_in = 2 * n_pairs + len(kinds)
    o_ref = refs[n_in]
    w_bf = refs[n_in + 1:]

    @pl.when(pl.program_id(1) == 0)
    def _():
        for k in range(n_pairs):
            w_bf[k][...] = refs[2 * k + 1][...].astype(BF16)

    accs = [jnp.dot(refs[2 * k][...].astype(BF16), w_bf[k][...], preferred_element_type=F32)
            for k in range(n_pairs)]
    ex = []
    for k, kind in enumerate(kinds):
        r = refs[2 * n_pairs + k]
        ex.append(r[0] if kind == "row" else r[...])
    o_ref[...] = epilogue(accs, ex).astype(o_ref.dtype)


def _mm(pairs, extras, epilogue, n_out, out_dtype, tm, tn, rows_per_batch=None, alias_extra=None,
        name="mm"):
    m = pairs[0][0].shape[0]
    rpb = rows_per_batch if rows_per_batch is not None else m
    tm = min(tm, rpb)
    tn = min(tn, n_out)
    tpb = rpb // tm
    in_specs, args, scratch = [], [], []
    est = 2 * _nbytes((tm, tn), out_dtype) + 2 * _nbytes((tm, tn), F32) * max(1, len(pairs))
    for a, w, layer, off in pairs:
        kdim = a.shape[1]
        in_specs.append(pl.BlockSpec((tm, kdim), lambda j, i: (i, 0)))
        in_specs.append(pl.BlockSpec((None, kdim, tn),
                                     lambda j, i, off=off, layer=layer: (layer, 0, j + off)))
        scratch.append(pltpu.VMEM((kdim, tn), BF16))
        args += [a, w]
        est += (2 * _nbytes((tm, kdim), a.dtype) + 2 * _nbytes((kdim, tn), w.dtype)
                + _nbytes((kdim, tn), BF16))
        if a.dtype != BF16:
            est += _nbytes((tm, kdim), BF16)
    kinds = []
    for kind, arr, off in extras:
        kinds.append(kind)
        if kind == "tile":
            in_specs.append(pl.BlockSpec((tm, tn), lambda j, i, off=off: (i, j + off)))
            est += 2 * _nbytes((tm, tn), arr.dtype)
        else:
            in_specs.append(pl.BlockSpec((1, 1, tn), lambda j, i, off=off: (i // tpb, 0, j + off)))
        args.append(arr)
    aliases = {}
    if alias_extra is not None:
        aliases = {2 * len(pairs) + alias_extra: 0}
    return pl.pallas_call(
        functools.partial(_mm_kernel, n_pairs=len(pairs), kinds=tuple(kinds), epilogue=epilogue),
        grid=(n_out // tn, m // tm),
        in_specs=in_specs,
        out_specs=pl.BlockSpec((tm, tn), lambda j, i: (i, j)),
        out_shape=jax.ShapeDtypeStruct((m, n_out), out_dtype),
        scratch_shapes=scratch,
        input_output_aliases=aliases,
        compiler_params=_cparams(("parallel", "arbitrary"), est),
        name=name,
    )(*args)


def _s5_operators(lam_re, lam_im, log_dt, b_re, b_im, c_re, c_im, d_skip, t):
    n = OP["s5_group_dim"]
    lr = jnp.minimum(lam_re.astype(F32), OP["lambda_re_max"])
    li = lam_im.astype(F32)
    dt = jnp.exp(log_dt.astype(F32))[..., None]
    g, p = lr.shape[1], lr.shape[2]
    gt = S5_TILE_GROUPS
    j = g // gt
    mag = jnp.exp(lr * dt)
    a_re = mag * jnp.cos(li * dt)
    a_im = mag * jnp.sin(li * dt)
    num_re, num_im = a_re - 1.0, a_im
    den = lr * lr + li * li
    f_re = (num_re * lr + num_im * li) / den
    f_im = (num_im * lr - num_re * li) / den
    br, bi = b_re.astype(F32), b_im.astype(F32)
    bb_re = f_re[..., None] * br - f_im[..., None] * bi
    bb_im = f_re[..., None] * bi + f_im[..., None] * br
    cr, ci = c_re.astype(F32), c_im.astype(F32)

    k = jnp.arange(t + 1, dtype=F32)[:, None, None, None]
    pw_mag = jnp.exp(lr[None] * dt[None] * k)
    pw_re = pw_mag * jnp.cos(li[None] * dt[None] * k)
    pw_im = pw_mag * jnp.sin(li[None] * dt[None] * k)

    ca_re = cr[None] * pw_re[:t, :, :, None, :] - ci[None] * pw_im[:t, :, :, None, :]
    ca_im = cr[None] * pw_im[:t, :, :, None, :] + ci[None] * pw_re[:t, :, :, None, :]
    ca = jnp.concatenate([ca_re, -ca_im], axis=-1)
    ca = ca.transpose(1, 2, 0, 3, 4).reshape(2, g, t * n, 2 * p)
    bbs = jnp.concatenate([bb_re, bb_im], axis=2)
    kk = jnp.einsum("dgxp,dgpm->dgxm", ca, bbs, precision=HI)
    kk = kk.reshape(2, g, t, n, n).transpose(2, 0, 1, 3, 4)
    kf, kb = kk[:, 0], kk[:, 1]
    k0 = kf[0] + kb[0] + d_skip.astype(F32).reshape(g, n)[:, :, None] * jnp.eye(n, dtype=F32)
    kall = jnp.concatenate([kb[1:][::-1], k0[None], kf[1:]], axis=0)
    kc = kall.reshape(2 * t - 1, j, gt, n, n).transpose(1, 0, 4, 2, 3)
    kc = kc.reshape(j, 2 * t - 1, n, gt * n).astype(BF16)
    lane = jnp.arange(gt * n)
    same_group = (lane[:, None] // n) == (lane[None, :] // n)
    bd = jnp.where(same_group, jnp.tile(kc, (1, 1, gt, 1)), 0)

    ps_re = jnp.stack([pw_re[:t][::-1, 0], pw_re[:t, 1]], axis=0)
    ps_im = jnp.stack([pw_im[:t][::-1, 0], pw_im[:t, 1]], axis=0)
    w_re = ps_re[..., None] * bb_re[:, None] - ps_im[..., None] * bb_im[:, None]
    w_im = ps_re[..., None] * bb_im[:, None] + ps_im[..., None] * bb_re[:, None]
    w6 = jnp.stack([w_re, w_im], axis=1).reshape(2, 2, t, j, gt, p, n)
    wc = w6.transpose(3, 2, 6, 0, 1, 4, 5).reshape(j, t, n, 4 * gt * p).astype(BF16)

    po_re = jnp.stack([pw_re[1:, 0], pw_re[1:][::-1, 1]], axis=0)
    po_im = jnp.stack([pw_im[1:, 0], pw_im[1:][::-1, 1]], axis=0)
    co_re = cr[:, None] * po_re[:, :, :, None, :] - ci[:, None] * po_im[:, :, :, None, :]
    co_im = cr[:, None] * po_im[:, :, :, None, :] + ci[:, None] * po_re[:, :, :, None, :]
    o6 = jnp.stack([co_re, -co_im], axis=1).reshape(2, 2, t, j, gt, n, p)
    oc = o6.transpose(3, 0, 1, 6, 2, 4, 5).reshape(j, 4, p, t * gt * n).astype(BF16)
    at = jnp.stack([pw_re[t, 0], pw_im[t, 0], pw_re[t, 1], pw_im[t, 1]], axis=0)
    at = at.reshape(4, j, gt * p).transpose(1, 0, 2).reshape(j, 1, 4 * gt * p)
    return bd, wc, oc, at


def _s5_masks(t):
    n, p, gt = OP["s5_group_dim"], OP["s5_state"], S5_TILE_GROUPS
    col_group = (jnp.arange(4 * gt * p) // p) % gt
    mask_w = ((jnp.arange(gt * n)[:, None] // n) == col_group[None, :]).astype(BF16)
    out_group = (jnp.arange(t * gt * n) // n) % gt
    mask_o = ((jnp.arange(gt * p)[:, None] // p) == out_group[None, :]).astype(BF16)
    return mask_w, mask_o


def _gelu_tanh(y):
    return 0.5 * y * (1.0 + jnp.tanh(0.7978845608028654 * (y + 0.044715 * y * y * y)))


def _s5_kernel(uc_ref, ux_ref, bd_ref, wc_ref, oc_ref, mw_ref, mo_ref, at_ref, oc_out, ox_out,
               mt_scr, wst_scr, wout_scr, s_scr, hin_scr, hend_scr, *, c_ctx, c_lat, half, t, batch):
    lanes = ux_ref.shape[1]
    step = pl.program_id(1)

    @pl.when(step == 0)
    def _():
        for s in range(t):
            for k in range(t):
                mt_scr[s * lanes:(s + 1) * lanes, k * lanes:(k + 1) * lanes] = bd_ref[0, k - s + t - 1]
            rep = lanes // wc_ref.shape[2]
            wst_scr[s * lanes:(s + 1) * lanes, :] = jnp.tile(wc_ref[0, s], (rep, 1)) * mw_ref[...]
        rows = mo_ref.shape[0]
        for q in range(oc_ref.shape[1]):
            rep = rows // oc_ref.shape[2]
            wout_scr[q * rows:(q + 1) * rows, :] = jnp.tile(oc_ref[0, q], (rep, 1)) * mo_ref[...]

    at = at_ref[0]
    afr, afi = at[:, 0:half], at[:, half:2 * half]
    abr, abi = at[:, 2 * half:3 * half], at[:, 3 * half:4 * half]

    def run(u_ref, o_ref, n_seq, n_chunks, h0_of):
        rows = n_seq * n_chunks
        u = jnp.concatenate([u_ref[pl.ds(s, rows, stride=t), :] for s in range(t)],
                            axis=1).astype(BF16)
        s_scr[0:rows, :] = jnp.dot(u, wst_scr[...], preferred_element_type=F32)
        ends = []
        for q in range(n_seq):
            h0 = h0_of(q)
            init = (h0[:, 0:half], h0[:, half:2 * half], h0[:, 2 * half:3 * half],
                    h0[:, 3 * half:4 * half])
            lo = q * n_chunks

            def body(i, carry, lo=lo):
                hfr, hfi, hbr, hbi = carry
                cf = lo + i
                cb = lo + n_chunks - 1 - i
                hin_scr[pl.ds(cf, 1), 0:half] = hfr
                hin_scr[pl.ds(cf, 1), half:2 * half] = hfi
                hin_scr[pl.ds(cb, 1), 2 * half:3 * half] = hbr
                hin_scr[pl.ds(cb, 1), 3 * half:4 * half] = hbi
                sf = s_scr[pl.ds(cf, 1), 0:2 * half]
                sb = s_scr[pl.ds(cb, 1), 2 * half:4 * half]
                nfr = afr * hfr - afi * hfi + sf[:, 0:half]
                nfi = afr * hfi + afi * hfr + sf[:, half:2 * half]
                nbr = abr * hbr - abi * hbi + sb[:, 0:half]
                nbi = abr * hbi + abi * hbr + sb[:, half:2 * half]
                return nfr, nfi, nbr, nbi

            ends.append(jnp.concatenate(lax.fori_loop(0, n_chunks, body, init), axis=1))
        y = (jnp.dot(u, mt_scr[...], preferred_element_type=F32)
             + jnp.dot(hin_scr[0:rows, :].astype(BF16), wout_scr[...], preferred_element_type=F32))
        for k in range(t):
            o_ref[pl.ds(k, rows, stride=t), :] = _gelu_tanh(
                y[:, k * lanes:(k + 1) * lanes]).astype(o_ref.dtype)
        return ends

    @pl.when(step == 0)
    def _():
        zero = jnp.zeros((1, 4 * half), F32)
        ends = run(uc_ref, oc_out, batch, c_ctx, lambda q: zero)
        for q in range(batch):
            hend_scr[q:q + 1, :] = ends[q]

    @pl.when(step > 0)
    def _():
        run(ux_ref, ox_out, 1, c_lat, lambda q: hend_scr[pl.ds(step - 1, 1), :])


def _s5(uc2, ux2, ds5, batch, ops, layer):
    bd, wc, oc, mask_w, mask_o, at = ops
    mc, mx = uc2.shape[0], ux2.shape[0]
    lc, lx = mc // batch, mx // batch
    t = S5_CHUNK
    cc, cx = lc // t, lx // t
    j = ds5 // V7X_LANES
    w = t * V7X_LANES
    sw = wc.shape[4]
    compact = sum(_nbytes(a.shape[2:], BF16) for a in (bd, wc, oc)) + _nbytes(mask_w.shape, BF16) \
        + _nbytes(mask_o.shape, BF16)
    est = (3 * _nbytes((w, sw), BF16) + 2 * compact + 4 * _nbytes((lx + lc, V7X_LANES), F32)
           + 2 * _nbytes((cx, sw), F32) + 3 * _nbytes((cx, w), F32) + _nbytes((cx, sw), BF16)
           + 2 * _nbytes((cx, w), BF16))
    ctx_spec = pl.BlockSpec((mc, V7X_LANES), lambda jj, s: (0, jj))
    lat_spec = pl.BlockSpec((lx, V7X_LANES), lambda jj, s: (jnp.maximum(s - 1, 0), jj))
    scan_rows = max(cx, batch * cc)

    def op_spec(a):
        return pl.BlockSpec((None, 1) + a.shape[2:], lambda jj, s: (layer, jj) + (0,) * (a.ndim - 2))

    return pl.pallas_call(
        functools.partial(_s5_kernel, c_ctx=cc, c_lat=cx, half=sw // 4, t=t, batch=batch),
        grid=(j, 1 + batch),
        in_specs=[ctx_spec, lat_spec, op_spec(bd), op_spec(wc), op_spec(oc),
                  pl.BlockSpec(mask_w.shape, lambda jj, s: (0, 0)),
                  pl.BlockSpec(mask_o.shape, lambda jj, s: (0, 0)),
                  op_spec(at)],
        out_specs=[ctx_spec, lat_spec],
        out_shape=[jax.ShapeDtypeStruct((mc, ds5), F32), jax.ShapeDtypeStruct((mx, ds5), F32)],
        scratch_shapes=[pltpu.VMEM((w, w), BF16), pltpu.VMEM((w, sw), BF16), pltpu.VMEM((sw, w), BF16),
                        pltpu.VMEM((scan_rows, sw), F32), pltpu.VMEM((scan_rows, sw), F32),
                        pltpu.VMEM((-(-batch // 8) * 8, sw), F32)],
        compiler_params=_cparams(("arbitrary", "arbitrary"), est),
        name="s5",
    )(uc2, ux2, bd, wc, oc, mask_w, mask_o, at)


def _dft_mats(n):
    hi = n // V7X_LANES
    j = jnp.arange(n, dtype=jnp.int32)[None, :]
    a1 = ((jnp.arange(V7X_LANES, dtype=jnp.int32)[:, None] * j) % n).astype(F32) * (2.0 * math.pi / n)
    a2 = ((jnp.arange(hi, dtype=jnp.int32)[:, None] * j) % hi).astype(F32) * (2.0 * math.pi / hi)
    c1, s1 = jnp.cos(a1)[None, :, :], jnp.sin(a1)[None, :, :]
    c2, s2 = jnp.cos(a2)[:, None, :], jnp.sin(a2)[:, None, :]
    s = 1.0 / math.sqrt(n)
    cos = ((c2 * c1 - s2 * s1) * s).reshape(n, n).astype(BF16)
    sin = ((s2 * c1 + c2 * s1) * s).reshape(n, n).astype(BF16)
    return cos, sin


def _chan_dft_kernel(v_ref, w_ref, xc_ref, xs_ref, *, gd):
    r = jnp.dot(v_ref[...].astype(BF16), w_ref[...], preferred_element_type=F32)
    xc_ref[...] = r[:, :gd].astype(xc_ref.dtype)
    xs_ref[...] = r[:, gd:].astype(xs_ref.dtype)


def _chan_dft(v2, col_off, dft, wc, tm=1024):
    m = v2.shape[0]
    gd = wc.shape[0]
    tm = min(tm, m)
    goff = col_off // gd
    spec = pl.BlockSpec((tm, gd), lambda i, g: (i, g))
    est = (2 * _nbytes((tm, gd), F32) + 5 * _nbytes((tm, gd), BF16) + 2 * _nbytes((gd, 2 * gd), BF16)
           + 2 * _nbytes((tm, 2 * gd), F32))
    return pl.pallas_call(
        functools.partial(_chan_dft_kernel, gd=gd),
        grid=(m // tm, dft // gd),
        in_specs=[pl.BlockSpec((tm, gd), lambda i, g: (i, goff + g)),
                  pl.BlockSpec((gd, 2 * gd), lambda i, g: (0, 0))],
        out_specs=[spec, spec],
        out_shape=[jax.ShapeDtypeStruct((m, dft), BF16)] * 2,
        compiler_params=_cparams(("parallel", "parallel"), est),
        name="chan_dft",
    )(v2, wc)


def _seq_dft_kernel(cl_ref, sl_ref, xc_ref, xs_ref, o_ref):
    y = (jnp.dot(cl_ref[...], xc_ref[0], preferred_element_type=F32)
         - jnp.dot(sl_ref[...], xs_ref[0], preferred_element_type=F32))
    o_ref[0] = y.astype(o_ref.dtype)


def _seq_dft(xc, xs, cl, sl, batch, tm=512, tn=512):
    m, dft = xc.shape
    l = m // batch
    tm = min(tm, l)
    tn = min(tn, dft)
    xc3, xs3 = xc.reshape(batch, l, dft), xs.reshape(batch, l, dft)
    est = 4 * _nbytes((tm, l), BF16) + 4 * _nbytes((l, tn), BF16) + 4 * _nbytes((tm, tn), F32)
    out = pl.pallas_call(
        _seq_dft_kernel,
        grid=(batch, dft // tn, l // tm),
        in_specs=[
            pl.BlockSpec((tm, l), lambda b, jn, i: (i, 0)),
            pl.BlockSpec((tm, l), lambda b, jn, i: (i, 0)),
            pl.BlockSpec((1, l, tn), lambda b, jn, i: (b, 0, jn)),
            pl.BlockSpec((1, l, tn), lambda b, jn, i: (b, 0, jn)),
        ],
        out_specs=pl.BlockSpec((1, tm, tn), lambda b, jn, i: (b, i, jn)),
        out_shape=jax.ShapeDtypeStruct((batch, l, dft), BF16),
        compiler_params=_cparams(("parallel", "parallel", "parallel"), est),
        name="seq_dft",
    )(cl, sl, xc3, xs3)
    return out.reshape(m, dft)


def _prefix_count(mask_f32, tri):
    e, l = mask_f32.shape
    carry = jnp.zeros((e, 1), F32)
    outs = []
    for k in range(l // CUMSUM_BLOCK):
        blk = mask_f32[:, k * CUMSUM_BLOCK:(k + 1) * CUMSUM_BLOCK]
        outs.append(jnp.dot(blk.astype(BF16), tri, preferred_element_type=F32) + carry)
        carry = carry + jnp.sum(blk, axis=1, keepdims=True)
    return outs[0] if len(outs) == 1 else jnp.concatenate(outs, axis=1)


def _route_kernel(lg_ref, posm_ref, aff_ref, bnd_ref, *, cap, tile):
    lg = lg_ref[...]
    mx = jnp.max(lg, axis=0, keepdims=True)
    ex = jnp.exp(lg - mx)
    aff = ex / jnp.sum(ex, axis=0, keepdims=True)
    bits = pltpu.bitcast(aff, jnp.int32)
    e = lg.shape[0]
    v = jnp.zeros((e, 1), jnp.int32)
    for bit in range(30, -1, -1):
        cand = v | (1 << bit)
        cnt = jnp.sum(jnp.where(bits >= cand, 1.0, 0.0), axis=1, keepdims=True)
        v = jnp.where(cnt >= cap, cand, v)
    gt = bits > v
    eq = bits == v
    n_gt = jnp.sum(gt.astype(F32), axis=1, keepdims=True)
    r = lax.broadcasted_iota(jnp.int32, (CUMSUM_BLOCK, CUMSUM_BLOCK), 0)
    c = lax.broadcasted_iota(jnp.int32, (CUMSUM_BLOCK, CUMSUM_BLOCK), 1)
    tri = jnp.where(r < c, 1.0, 0.0).astype(BF16)
    eq_f = jnp.where(eq, 1.0, 0.0)
    tie_ok = _prefix_count(eq_f, tri) < (cap - n_gt)
    sel_f = jnp.where(gt, 1.0, jnp.where(tie_ok, eq_f, 0.0))
    pos = _prefix_count(sel_f, tri)
    posm_ref[0] = jnp.where(sel_f > 0.5, pos, -1.0).astype(jnp.int32)
    aff_ref[0] = aff
    l = lg.shape[1]
    starts = [pos[:, k * tile:k * tile + 1] for k in range(l // tile)]
    starts.append(jnp.full((e, 1), float(cap), F32))
    bnd_ref[0] = jnp.concatenate(starts, axis=1).astype(jnp.int32)


def _route(lg_t, batch, cap, tile):
    e, m = lg_t.shape
    l = m // batch
    nt = l // tile
    est = 16 * _nbytes((e, l), F32)
    return pl.pallas_call(
        functools.partial(_route_kernel, cap=cap, tile=tile),
        grid=(batch,),
        in_specs=[pl.BlockSpec((e, l), lambda b: (0, b))],
        out_specs=[pl.BlockSpec((1, e, l), lambda b: (b, 0, 0))] * 2
        + [pl.BlockSpec((1, e, nt + 1), lambda b: (b, 0, 0))],
        out_shape=[jax.ShapeDtypeStruct((batch, e, l), jnp.int32),
                   jax.ShapeDtypeStruct((batch, e, l), F32),
                   jax.ShapeDtypeStruct((batch, e, nt + 1), jnp.int32)],
        compiler_params=_cparams(("parallel",), est),
        name="route",
    )(lg_t)


def _slots_kernel(posm_ref, aff_ref, idx_ref, gs_ref, *, cap):
    pm = posm_ref[0, 0]
    l = pm.shape[1]
    slot = lax.broadcasted_iota(jnp.int32, (cap, l), 0)
    tok = lax.broadcasted_iota(jnp.int32, (1, l), 1).astype(F32)
    hit = pm == slot
    idx_ref[0, 0] = jnp.sum(jnp.where(hit, tok, 0.0), axis=1, keepdims=True).astype(jnp.int32)
    gs_ref[0] = jnp.sum(jnp.where(hit, aff_ref[0, 0], 0.0), axis=1, keepdims=True)


def _slots(posm, aff, cap):
    batch, e, l = posm.shape
    est = 6 * _nbytes((cap, l), F32) + 4 * _nbytes((cap, V7X_LANES), F32)
    return pl.pallas_call(
        functools.partial(_slots_kernel, cap=cap),
        grid=(batch, e),
        in_specs=[pl.BlockSpec((1, 1, 1, l), lambda b, ee: (b, ee, 0, 0))] * 2,
        out_specs=[pl.BlockSpec((1, 1, cap, 1), lambda b, ee: (b, ee, 0, 0)),
                   pl.BlockSpec((1, cap, 1), lambda b, ee: (ee, b, 0))],
        out_shape=[jax.ShapeDtypeStruct((batch, e, cap, 1), jnp.int32),
                   jax.ShapeDtypeStruct((e, batch * cap, 1), F32)],
        compiler_params=_cparams(("parallel", "parallel"), est),
        name="moe_slots",
    )(posm.reshape(batch, e, 1, l), aff.reshape(batch, e, 1, l))


GATHER_ISSUE_UNROLL = 8


def _gather_kernel(idx_ref, hp_hbm, o_ref, sem, *, cap, seq, n_exp):
    b, ee = pl.program_id(0), pl.program_id(1)
    base = (b * n_exp + ee) * cap
    row0 = b * seq

    def issue(c, carry):
        for k in range(GATHER_ISSUE_UNROLL):
            s = c * GATHER_ISSUE_UNROLL + k
            pltpu.make_async_copy(hp_hbm.at[pl.ds(row0 + idx_ref[base + s], 1)],
                                  o_ref.at[0, pl.ds(s, 1)], sem).start()
        return carry

    lax.fori_loop(0, cap // GATHER_ISSUE_UNROLL, issue, 0)
    pltpu.make_async_copy(hp_hbm.at[pl.ds(0, cap)], o_ref.at[0], sem).wait()


def _gather(idx, hp, batch, cap):
    e = idx.shape[1]
    m, dh = hp.shape
    est = 2 * _nbytes((cap, dh), jnp.uint32)
    return pl.pallas_call(
        functools.partial(_gather_kernel, cap=cap, seq=m // batch, n_exp=e),
        grid_spec=pltpu.PrefetchScalarGridSpec(
            num_scalar_prefetch=1,
            grid=(batch, e),
            in_specs=[pl.BlockSpec(memory_space=pl.ANY)],
            out_specs=pl.BlockSpec((1, cap, dh), lambda b, ee, ix: (ee, b, 0)),
            scratch_shapes=[pltpu.SemaphoreType.DMA(())],
        ),
        out_shape=jax.ShapeDtypeStruct((e, batch * cap, dh), jnp.uint32),
        compiler_params=_cparams(("arbitrary", "arbitrary"), est),
        name="moe_gather",
    )(idx.reshape(-1), hp)


EXPERT_ROW_TILE = 1024


def _stream_tiles(row_counts):
    tiles, start = [], 0
    for r in row_counts:
        tr = min(EXPERT_ROW_TILE, r)
        assert r % tr == 0
        tiles.append((tr, start, r // tr))
        start += r // tr
    return tiles, start


def _stream_block(lo, nb):
    return lambda ee, cj, rr: (ee, jnp.clip(rr - lo, 0, nb - 1), 0)


def _up_kernel(*refs, tiles):
    n = len(tiles)
    xg_refs, (wg_ref, wu_ref), o_refs = refs[:n], refs[n:n + 2], refs[n + 2:2 * n + 2]
    wg_bf, wu_bf = refs[2 * n + 2:]
    rr = pl.program_id(2)

    @pl.when(rr == 0)
    def _():
        wg_bf[...] = wg_ref[0, 0].astype(BF16)
        wu_bf[...] = wu_ref[0, 0].astype(BF16)

    for k, (_, lo, nb) in enumerate(tiles):
        @pl.when((rr >= lo) & (rr < lo + nb))
        def _(k=k):
            x_lo, x_hi = _unpack_bf16_pairs(xg_refs[k][0])
            half = x_lo.shape[1]
            g = (jnp.dot(x_lo, wg_bf[0:half, :], preferred_element_type=F32)
                 + jnp.dot(x_hi, wg_bf[half:, :], preferred_element_type=F32))
            u = (jnp.dot(x_lo, wu_bf[0:half, :], preferred_element_type=F32)
                 + jnp.dot(x_hi, wu_bf[half:, :], preferred_element_type=F32))
            o_refs[k][0] = (g * jax.nn.sigmoid(g) * u).astype(o_refs[k].dtype)


def _expert_up(xgs, w_gate, w_up, layer, tf=256):
    e, _, dh = xgs[0].shape
    d = 2 * dh
    f = w_gate.shape[3]
    tf = min(tf, f)
    tiles, steps = _stream_tiles([xg.shape[1] for xg in xgs])
    tr_max = max(t[0] for t in tiles)
    est = (4 * _nbytes((d, tf), F32) + 3 * _nbytes((d, tf), BF16)
           + sum(2 * _nbytes((t[0], d), BF16) for t in tiles) + _nbytes((tr_max, d), BF16)
           + 6 * _nbytes((tr_max, tf), F32))
    w_spec = pl.BlockSpec((1, 1, d, tf), lambda ee, fj, rr: (layer, ee, 0, fj))
    return pl.pallas_call(
        functools.partial(_up_kernel, tiles=tiles),
        grid=(e, f // tf, steps),
        in_specs=[pl.BlockSpec((1, tr, dh), _stream_block(lo, nb)) for tr, lo, nb in tiles]
        + [w_spec, w_spec],
        out_specs=[pl.BlockSpec((1, tr, tf), lambda ee, fj, rr, lo=lo, nb=nb:
                                (ee, jnp.clip(rr - lo, 0, nb - 1), fj)) for tr, lo, nb in tiles],
        out_shape=[jax.ShapeDtypeStruct((e, xg.shape[1], f), BF16) for xg in xgs],
        scratch_shapes=[pltpu.VMEM((d, tf), BF16), pltpu.VMEM((d, tf), BF16)],
        compiler_params=_cparams(("parallel", "parallel", "arbitrary"), est),
        name="moe_up",
    )(*xgs, w_gate, w_up)


def _down_kernel(*refs, tiles):
    n = len(tiles)
    h_refs, wd_ref, gs_refs, o_refs = refs[:n], refs[n], refs[n + 1:2 * n + 1], refs[2 * n + 1:3 * n + 1]
    wd_bf = refs[3 * n + 1]
    rr = pl.program_id(2)

    @pl.when(rr == 0)
    def _():
        wd_bf[...] = wd_ref[0, 0].astype(BF16)

    for k, (_, lo, nb) in enumerate(tiles):
        @pl.when((rr >= lo) & (rr < lo + nb))
        def _(k=k):
            y = jnp.dot(h_refs[k][0], wd_bf[...], preferred_element_type=F32)
            o_refs[k][0] = (y * gs_refs[k][0]).astype(o_refs[k].dtype)


def _expert_down(hids, w_down, gslots, layer, td=2048):
    e, _, f = hids[0].shape
    d = w_down.shape[3]
    td = min(td, d)
    tiles, steps = _stream_tiles([h.shape[1] for h in hids])
    tr_max = max(t[0] for t in tiles)
    est = (2 * _nbytes((f, td), F32) + 2 * _nbytes((f, td), BF16)
           + sum(2 * _nbytes((t[0], f), BF16) + 2 * _nbytes((t[0], td), BF16)
                 + 2 * _nbytes((t[0], V7X_LANES), F32) for t in tiles)
           + 2 * _nbytes((tr_max, td), F32))
    return pl.pallas_call(
        functools.partial(_down_kernel, tiles=tiles),
        grid=(e, d // td, steps),
        in_specs=[pl.BlockSpec((1, tr, f), _stream_block(lo, nb)) for tr, lo, nb in tiles]
        + [pl.BlockSpec((1, 1, f, td), lambda ee, dj, rr: (layer, ee, 0, dj))]
        + [pl.BlockSpec((1, tr, 1), _stream_block(lo, nb)) for tr, lo, nb in tiles],
        out_specs=[pl.BlockSpec((1, tr, td), lambda ee, dj, rr, lo=lo, nb=nb:
                                (ee, jnp.clip(rr - lo, 0, nb - 1), dj)) for tr, lo, nb in tiles],
        out_shape=[jax.ShapeDtypeStruct((e, h.shape[1], d), BF16) for h in hids],
        scratch_shapes=[pltpu.VMEM((f, td), BF16)],
        compiler_params=_cparams(("parallel", "parallel", "arbitrary"), est),
        name="moe_down",
    )(*hids, w_down, *gslots)


COMBINE_TILE = 256
DMA_ROW_ALIGN = 16


def combine_window(cap, seq, tile):
    return min(cap, max(DMA_ROW_ALIGN, 2 * tile * cap // seq))


def _combine_kernel(tbl_ref, pt_ref, y_hbm, x_ref, g_ref, *rest, n_exp, cap, win, nt,
                    n_steps, emit_x, post_norm, post_mod, eps):
    stage, extra, acc_scr, sems, xsem = rest[-5:]
    n_post_in = (1 + 2 * post_mod) if post_norm else 0
    post_in = rest[:n_post_in]
    outs = rest[n_post_in:-5]
    n = pl.program_id(0) * nt + pl.program_id(1)
    cur = n % 2
    nxt_step = jnp.minimum(n + 1, n_steps - 1)

    def window_starts(step):
        b, i = step // nt, step % nt
        t0 = (b * (nt + 1) + i) * n_exp
        out = []
        for ee in range(n_exp):
            s0a = jnp.minimum((tbl_ref[t0 + ee] // DMA_ROW_ALIGN) * DMA_ROW_ALIGN, cap - win)
            out.append(pl.multiple_of(s0a, DMA_ROW_ALIGN))
        return b * cap, out, t0

    def issue_windows(step, buf):
        row_base, starts, _ = window_starts(step)
        for ee in range(n_exp):
            pltpu.make_async_copy(y_hbm.at[ee, pl.ds(row_base + starts[ee], win), :],
                                  stage.at[buf, pl.ds(ee * win, win), :], sems.at[buf]).start()

    def wait_windows(buf):
        pltpu.make_async_copy(y_hbm.at[0, pl.ds(0, n_exp * win), :], stage.at[buf],
                              sems.at[buf]).wait()

    @pl.when(n == 0)
    def _():
        issue_windows(0, 0)

    issue_windows(nxt_step, 1 - cur)
    row_base, starts, t0 = window_starts(n)
    pt = pt_ref[...]
    lane = lax.broadcasted_iota(jnp.int32, (pt.shape[0], win), 1)
    p = jnp.concatenate(
        [jnp.where(pt[:, ee:ee + 1] - starts[ee] == lane, 1.0, 0.0).astype(BF16)
         for ee in range(n_exp)], axis=1)
    wait_windows(cur)
    acc_scr[...] = jnp.dot(p, stage[cur], preferred_element_type=F32)

    for ee in range(n_exp):
        first = starts[ee] + win
        n_extra = jnp.maximum(0, (tbl_ref[t0 + n_exp + ee] - first + win - 1) // win)

        def extra_window(w, carry, ee=ee, first=first):
            lo = first + w * win
            src0 = pl.multiple_of(jnp.minimum(lo, cap - win), DMA_ROW_ALIGN)
            cp = pltpu.make_async_copy(y_hbm.at[ee, pl.ds(row_base + src0, win), :], extra, xsem)
            cp.start()
            cp.wait()
            col = pt[:, ee:ee + 1]
            hit = jnp.where(col >= lo, col - src0, -1) == lane
            acc_scr[...] += jnp.dot(jnp.where(hit, 1.0, 0.0).astype(BF16), extra[...],
                                    preferred_element_type=F32)
            return carry

        lax.fori_loop(0, n_extra, extra_window, 0)

    xn = x_ref[...] + g_ref[0] * acc_scr[...]
    if emit_x:
        outs[0][...] = xn
    if post_norm:
        shift, scale = (post_in[1][0], post_in[2][0]) if post_mod else (None, None)
        h_ref = outs[-1]
        h_ref[...] = _rms_modulate(xn, post_in[0][...], shift, scale, eps).astype(h_ref.dtype)

    @pl.when(n == n_steps - 1)
    def _():
        wait_windows(1 - cur)


def _combine(posm_t, bounds, y, x2, gate, batch, cap, post=None, emit_x=True):
    m, d = x2.shape
    l = m // batch
    e = posm_t.shape[1]
    tm = min(COMBINE_TILE, l)
    nt = l // tm
    win = combine_window(cap, l, tm)
    assert cap % DMA_ROW_ALIGN == 0 and win % DMA_ROW_ALIGN == 0 and bounds.shape == (batch, e, nt + 1)
    tbl = bounds.transpose(0, 2, 1).reshape(-1)
    est = (2 * _nbytes((e * win, d), BF16) + _nbytes((win, d), BF16) + 9 * _nbytes((tm, d), F32)
           + 2 * _nbytes((tm, e * win), BF16) + 2 * _nbytes((tm, V7X_LANES), F32))
    row_spec = pl.BlockSpec((tm, d), lambda b, i, t: (b * nt + i, 0))
    vec_spec = pl.BlockSpec((1, 1, d), lambda b, i, t: (b, 0, 0))
    in_specs = [pl.BlockSpec((tm, e), lambda b, i, t: (b * nt + i, 0)),
                pl.BlockSpec(memory_space=pl.ANY), row_spec, vec_spec]
    args = [tbl, posm_t, y, x2, gate]
    out_specs, out_shape = [], []
    if emit_x:
        out_specs.append(row_spec)
        out_shape.append(jax.ShapeDtypeStruct((m, d), F32))
    post_mod = False
    if post is not None:
        p_gain, p_shift, p_scale, p_dtype = post
        post_mod = p_shift is not None
        in_specs.append(pl.BlockSpec((1, d), lambda b, i, t: (0, 0)))
        args.append(p_gain.reshape(1, d))
        if post_mod:
            in_specs += [vec_spec, vec_spec]
            args += [p_shift, p_scale]
        out_specs.append(row_spec)
        out_shape.append(jax.ShapeDtypeStruct((m, d), p_dtype))
        est += 2 * _nbytes((tm, d), p_dtype)
    return pl.pallas_call(
        functools.partial(_combine_kernel, n_exp=e, cap=cap, win=win, nt=nt, n_steps=batch * nt, emit_x=emit_x, post_norm=post is not None,
                          post_mod=post_mod, eps=OP["rms_eps"]),
        grid_spec=pltpu.PrefetchScalarGridSpec(
            num_scalar_prefetch=1,
            grid=(batch, nt),
            in_specs=in_specs,
            out_specs=out_specs,
            scratch_shapes=[pltpu.VMEM((2, e * win, d), BF16), pltpu.VMEM((win, d), BF16),
                            pltpu.VMEM((tm, d), F32), pltpu.SemaphoreType.DMA((2,)),
                            pltpu.SemaphoreType.DMA(())],
        ),
        out_shape=out_shape,
        input_output_aliases={3: 0} if emit_x else {},
        compiler_params=_cparams(("arbitrary", "arbitrary"), est),
        name="moe_combine",
    )(*args)


def _sigmoid(z):
    return jax.nn.sigmoid(z)


def _in_proj(h2, w_in, layer, n_cols):
    return _mm([(h2, w_in, layer, 0)], [], lambda accs, ex: accs[0], n_cols, F32, 1024,
               math.gcd(512, n_cols), name="in_proj")


def _mixer_out(x2, a2, px, g_mix, wts, layer, dfts, batch, rpb):
    w_glu, w_s5o, w_fto, w_out = wts
    wc, cl, sl = dfts
    d = x2.shape[1]
    d_s5 = a2.shape[1]
    d_ft = w_fto.shape[1]
    xc, xs = _chan_dft(px, d_s5, d_ft, wc)
    yf = _seq_dft(xc, xs, cl, sl, batch)
    glu = _mm([(a2, w_glu, layer, 0)], [("tile", a2, 0)],
              lambda accs, ex: ex[0] * _sigmoid(accs[0]),
              d_s5, BF16, 1024, 512, name="glu")
    tn = math.gcd(512, d_s5 + d_ft, d)
    g0 = (d_s5 + d_ft) // tn
    merged = _mm([(glu, w_s5o, layer, 0), (yf, w_fto, layer, 0)],
                 [("tile", px, g0), ("tile", px, g0 + d // tn)],
                 lambda accs, ex: _sigmoid(ex[0]) * accs[0] + _sigmoid(ex[1]) * accs[1],
                 d, BF16, 1024, tn, name="merge")
    return _mm([(merged, w_out, layer, 0)], [("tile", x2, 0), ("row", g_mix, 0)],
               lambda accs, ex: ex[0] + ex[1] * accs[0],
               d, F32, 1024, tn, rows_per_batch=rpb, alias_extra=0, name="out_proj")


def _moe(streams, w_gate, w_up, w_down, layer):
    e = streams[0][4].shape[0]
    caps = [OP["capacity_factor"] * (s[0].shape[0] // s[1]) // e for s in streams]
    routed, xgs, gslots = [], [], []
    for (x2, batch, hp, _, lg_t, _, _), cap in zip(streams, caps):
        tile = min(COMBINE_TILE, x2.shape[0] // batch)
        posm, aff, bounds = _route(lg_t, batch, cap, tile)
        idx, gslot = _slots(posm, aff, cap)
        xgs.append(_gather(idx, hp, batch, cap))
        gslots.append(gslot)
        routed.append((posm, bounds))
    ys = _expert_down(_expert_up(xgs, w_gate, w_up, layer), w_down, gslots, layer)
    outs = []
    for (x2, batch, _, gate, _, post, emit_x), cap, y, (posm, bounds) in zip(
            streams, caps, ys, routed):
        posm_t = posm.transpose(0, 2, 1).reshape(x2.shape[0], e)
        outs.append(_combine(posm_t, bounds, y, x2, gate, batch, cap, post, emit_x))
    return outs


def _grid_posembed(n_tok, d):
    gw = OP["grid_w"]
    rows = n_tok // gw
    quarter = d // 4
    inv_freq = OP["pos_base"] ** (-jnp.arange(quarter, dtype=F32) / quarter)
    ang_r = jnp.arange(rows, dtype=F32)[:, None] * inv_freq
    ang_c = jnp.arange(gw, dtype=F32)[:, None] * inv_freq
    return (jnp.concatenate([jnp.sin(ang_r), jnp.cos(ang_r)], axis=-1),
            jnp.concatenate([jnp.sin(ang_c), jnp.cos(ang_c)], axis=-1))


def kernel(x, c, ctx, c_ctx, ada_w, ada_b, norm_mix_g, norm_ffn_g, w_in, s5_lam_re, s5_lam_im,
           s5_log_dt, s5_b_re, s5_b_im, s5_c_re, s5_c_im, s5_d, w_glu, w_s5_out, w_ft_out, w_out,
           w_router, w_gate, w_up, w_down, norm_final_g):
    batch, seq, d = x.shape
    ctx_len = ctx.shape[1]
    depth = ada_w.shape[0]
    d_s5 = w_glu.shape[1]
    d_ft = w_ft_out.shape[1]
    n_mod = OP["n_mod"]
    m_x, m_c = batch * seq, batch * ctx_len

    x2 = x.reshape(m_x, d)
    pos = tuple(p.astype(x.dtype) for p in _grid_posembed(seq, d))
    c2 = ctx.reshape(m_c, d)

    rows = -(-(batch + 1) // 8) * 8
    c8 = jnp.zeros((rows, d), F32).at[:batch].set(c).at[batch].set(c_ctx)
    mod = _adaln(c8, ada_w, ada_b)

    gd = d_ft // OP["ft_groups"]
    wc_c, wc_s = _dft_mats(gd)
    wc = jnp.concatenate([wc_c, wc_s], axis=1)
    dft_x = (wc,) + _dft_mats(seq)
    dft_c = (wc,) + _dft_mats(ctx_len)

    bd, wc_s5, oc_s5, at = jax.vmap(lambda *p: _s5_operators(*p, S5_CHUNK))(
        s5_lam_re, s5_lam_im, s5_log_dt, s5_b_re, s5_b_im, s5_c_re, s5_c_im, s5_d)
    s5_ops = (bd, wc_s5, oc_s5) + _s5_masks(S5_CHUNK) + (at,)
    d_in = w_in.shape[2]
    mix_w = (w_glu, w_s5_out, w_ft_out, w_out)

    mods_x = [[mod[i, :batch, k * d:(k + 1) * d].reshape(batch, 1, d) for k in range(n_mod)]
              for i in range(depth)]
    mods_c = [[jnp.broadcast_to(mod[i, batch:batch + 1, k * d:(k + 1) * d].reshape(1, 1, d),
                                (batch, 1, d)) for k in range(n_mod)] for i in range(depth)]

    hx = hc = None
    for i in range(depth):
        last = i == depth - 1
        mx, mc = mods_x[i], mods_c[i]
        wr_t = w_router[i].T.astype(BF16)
        if i == 0:
            hc = _norm(c2, norm_mix_g[i], mc[0], mc[1], rows_per_batch=ctx_len)
            hx, x2 = _norm(x2, norm_mix_g[i], mx[0], mx[1], rows_per_batch=seq, pos=pos)
        pc = _in_proj(hc, w_in, i, d_s5 if last else d_in)
        px = _in_proj(hx, w_in, i, d_in)
        ac, ax = _s5(pc, px, d_s5, batch, s5_ops, i)
        x2 = _mixer_out(x2, ax, px, mx[2], mix_w, i, dft_x, batch, seq)
        lgx, hpx = _norm(x2, norm_ffn_g[i], mx[3], mx[4], rows_per_batch=seq, wr_t=wr_t,
                         emit_h=False, pack=True)
        if last:
            post_x, post_c = (norm_final_g, None, None, x.dtype), None
        else:
            post_x = (norm_mix_g[i + 1], mods_x[i + 1][0], mods_x[i + 1][1], BF16)
            post_c = (norm_mix_g[i + 1], mods_c[i + 1][0], mods_c[i + 1][1], BF16)
        streams = [(x2, batch, hpx, mx[5], lgx, post_x, not last)]
        if not last:
            c2 = _mixer_out(c2, ac, pc, mc[2], mix_w, i, dft_c, batch, ctx_len)
            lgc, hpc = _norm(c2, norm_ffn_g[i], mc[3], mc[4], rows_per_batch=ctx_len, wr_t=wr_t,
                             emit_h=False, pack=True)
            streams.append((c2, batch, hpc, mc[5], lgc, post_c, True))
        outs = _moe(streams, w_gate, w_up, w_down, i)
        if last:
            return outs[0][0].reshape(batch, seq, d)
        x2, hx = outs[0]
        c2, hc = outs[1]
```

```python
import functools
import math

import jax
import jax.numpy as jnp
from jax import lax
from jax.experimental import pallas as pl
from jax.experimental.pallas import tpu as pltpu

BF16 = jnp.bfloat16
F32 = jnp.float32
HI = lax.Precision.HIGHEST

OP = dict(
    s5_group_dim=16,
    s5_state=64,
    ft_groups=4,
    capacity_factor=2,
    n_mod=6,
    rms_eps=1e-6,
    pos_base=10000.0,
    grid_w=64,
    lambda_re_max=-1e-4,
)

V7X_VMEM_BYTES = 64 * 1024 * 1024
V7X_LANES = 128
S5_TILE_GROUPS = V7X_LANES // 16
S5_CHUNK = 16
CUMSUM_BLOCK = 256


def _cparams(sem, vmem_est):
    limit = int(min(max(vmem_est * 5 // 4 + (4 << 20), 32 << 20), V7X_VMEM_BYTES - (6 << 20)))
    return pltpu.CompilerParams(dimension_semantics=sem, vmem_limit_bytes=limit)


def _nbytes(shape, dtype):
    return math.prod(shape) * jnp.dtype(dtype).itemsize


def _adaln_kernel(c_ref, w_ref, b_ref, o_ref):
    cv = c_ref[...]
    a = (cv * jax.nn.sigmoid(cv)).astype(BF16)
    o_ref[0] = jnp.dot(a, w_ref[0].astype(BF16), preferred_element_type=F32) + b_ref[0]


def _adaln(c8, ada_w, ada_b, tn=512):
    depth, d, n6 = ada_w.shape
    rows = c8.shape[0]
    tn = min(tn, n6)
    est = 2 * _nbytes((d, tn), F32) + 4 * _nbytes((rows, tn), F32) + _nbytes((rows, d), F32) * 2
    return pl.pallas_call(
        _adaln_kernel,
        grid=(depth, n6 // tn),
        in_specs=[
            pl.BlockSpec((rows, d), lambda l, j: (0, 0)),
            pl.BlockSpec((1, d, tn), lambda l, j: (l, 0, j)),
            pl.BlockSpec((1, 1, tn), lambda l, j: (l, 0, j)),
        ],
        out_specs=pl.BlockSpec((1, rows, tn), lambda l, j: (l, 0, j)),
        out_shape=jax.ShapeDtypeStruct((depth, rows, n6), F32),
        compiler_params=_cparams(("parallel", "parallel"), est),
        name="adaln",
    )(c8, ada_w, ada_b.reshape(depth, 1, n6))


def _rms_modulate(xf, g, shift, scale, eps):
    ms = jnp.mean(xf * xf, axis=-1, keepdims=True)
    y = xf * lax.rsqrt(ms + eps) * g
    if shift is not None:
        y = y * (1.0 + scale) + shift
    return y


def _pack_bf16_pairs(y):
    half = y.shape[1] // 2
    lo = pltpu.bitcast(y[:, :half].astype(BF16).astype(F32), jnp.uint32) >> 16
    hi = pltpu.bitcast(y[:, half:].astype(BF16).astype(F32), jnp.uint32) & jnp.uint32(0xFFFF0000)
    return hi | lo


def _unpack_bf16_pairs(p):
    lo = pltpu.bitcast(p << 16, F32).astype(BF16)
    hi = pltpu.bitcast(p & jnp.uint32(0xFFFF0000), F32).astype(BF16)
    return lo, hi


def _norm_kernel(*refs, eps, modulate, router, add_pos, emit_h, pack):
    x_ref, g_ref = refs[0], refs[1]
    k = 2
    xf = x_ref[...]
    if add_pos:
        pr, pc = refs[k][0], refs[k + 1][...]
        k += 2
        gw = pc.shape[0]
        pos = jnp.concatenate(
            [jnp.concatenate([jnp.broadcast_to(pr[q:q + 1], pc.shape), pc], axis=1)
             for q in range(xf.shape[0] // gw)], axis=0)
        xf = xf + pos
        refs[-1][...] = xf
    shift = scale = None
    if modulate:
        shift, scale = refs[k][0], refs[k + 1][0]
        k += 2
    y = _rms_modulate(xf, g_ref[...], shift, scale, eps)
    if router:
        wr_ref = refs[k]
        k += 1
    if emit_h:
        refs[k][...] = y.astype(refs[k].dtype)
        k += 1
    if router:
        refs[k][...] = lax.dot_general(
            wr_ref[...], y.astype(BF16), (((1,), (1,)), ((), ())), preferred_element_type=F32)
        k += 1
    if pack:
        refs[k][...] = _pack_bf16_pairs(y)


def _norm(x2, g, shift=None, scale=None, rows_per_batch=None, wr_t=None, out_dtype=BF16, tm=256,
          pos=None, emit_h=True, pack=False):
    m, d = x2.shape
    modulate = shift is not None
    router = wr_t is not None
    add_pos = pos is not None
    rpb = rows_per_batch if rows_per_batch is not None else m
    tm = min(tm, rpb)
    tpb = rpb // tm
    in_specs = [pl.BlockSpec((tm, d), lambda i: (i, 0)), pl.BlockSpec((1, d), lambda i: (0, 0))]
    args = [x2, g.reshape(1, d)]
    if add_pos:
        pos_rows, pos_cols = pos
        gw = pos_cols.shape[0]
        assert tm % gw == 0 and rpb % tm == 0
        in_specs.append(pl.BlockSpec((1, tm // gw, d // 2), lambda i: (i % tpb, 0, 0)))
        in_specs.append(pl.BlockSpec((gw, d // 2), lambda i: (0, 0)))
        args += [pos_rows.reshape(rpb // tm, tm // gw, d // 2), pos_cols]
    if modulate:
        in_specs += [pl.BlockSpec((1, 1, d), lambda i: (i // tpb, 0, 0))] * 2
        args += [shift, scale]
    out_specs, out_shape = [], []
    if emit_h:
        out_specs.append(pl.BlockSpec((tm, d), lambda i: (i, 0)))
        out_shape.append(jax.ShapeDtypeStruct((m, d), out_dtype))
    if router:
        e = wr_t.shape[0]
        in_specs.append(pl.BlockSpec((e, d), lambda i: (0, 0)))
        args.append(wr_t)
        out_specs.append(pl.BlockSpec((e, tm), lambda i: (0, i)))
        out_shape.append(jax.ShapeDtypeStruct((e, m), F32))
    est = 2 * _nbytes((tm, d), F32) + 2 * _nbytes((tm, d), out_dtype) + 3 * _nbytes((tm, d), F32)
    if pack:
        out_specs.append(pl.BlockSpec((tm, d // 2), lambda i: (i, 0)))
        out_shape.append(jax.ShapeDtypeStruct((m, d // 2), jnp.uint32))
        est += 4 * _nbytes((tm, d // 2), F32)
    if add_pos:
        out_specs.append(pl.BlockSpec((tm, d), lambda i: (i, 0)))
        out_shape.append(jax.ShapeDtypeStruct((m, d), F32))
        est += 4 * _nbytes((tm, d), F32)
    res = pl.pallas_call(
        functools.partial(_norm_kernel, eps=OP["rms_eps"], modulate=modulate, router=router,
                          add_pos=add_pos, emit_h=emit_h, pack=pack),
        grid=(m // tm,),
        in_specs=in_specs,
        out_specs=out_specs,
        out_shape=out_shape,
        compiler_params=_cparams(("parallel",), est),
        name="rmsnorm",
    )(*args)
    return res if len(res) > 1 else res[0]


def _mm_kernel(*refs, n_pairs, kinds, epilogue):
    n_in = 2 * n_pairs + len(kinds)
    o_ref = refs[n_in]
    w_bf = refs[n_in + 1:]

    @pl.when(pl.program_id(1) == 0)
    def _():
        for k in range(n_pairs):
            w_bf[k][...] = refs[2 * k + 1][...].astype(BF16)

    accs = [jnp.dot(refs[2 * k][...].astype(BF16), w_bf[k][...], preferred_element_type=F32)
            for k in range(n_pairs)]
    ex = []
    for k, kind in enumerate(kinds):
        r = refs[2 * n_pairs + k]
        ex.append(r[0] if kind == "row" else r[...])
    o_ref[...] = epilogue(accs, ex).astype(o_ref.dtype)


def _mm(pairs, extras, epilogue, n_out, out_dtype, tm, tn, rows_per_batch=None, alias_extra=None,
        name="mm"):
    m = pairs[0][0].shape[0]
    rpb = rows_per_batch if rows_per_batch is not None else m
    tm = min(tm, rpb)
    tn = min(tn, n_out)
    tpb = rpb // tm
    in_specs, args, scratch = [], [], []
    est = 2 * _nbytes((tm, tn), out_dtype) + 2 * _nbytes((tm, tn), F32) * max(1, len(pairs))
    for a, w, layer, off in pairs:
        kdim = a.shape[1]
        in_specs.append(pl.BlockSpec((tm, kdim), lambda j, i: (i, 0)))
        in_specs.append(pl.BlockSpec((None, kdim, tn),
                                     lambda j, i, off=off, layer=layer: (layer, 0, j + off)))
        scratch.append(pltpu.VMEM((kdim, tn), BF16))
        args += [a, w]
        est += (2 * _nbytes((tm, kdim), a.dtype) + 2 * _nbytes((kdim, tn), w.dtype)
                + _nbytes((kdim, tn), BF16))
        if a.dtype != BF16:
            est += _nbytes((tm, kdim), BF16)
    kinds = []
    for kind, arr, off in extras:
        kinds.append(kind)
        if kind == "tile":
            in_specs.append(pl.BlockSpec((tm, tn), lambda j, i, off=off: (i, j + off)))
            est += 2 * _nbytes((tm, tn), arr.dtype)
        else:
            in_specs.append(pl.BlockSpec((1, 1, tn), lambda j, i, off=off: (i // tpb, 0, j + off)))
        args.append(arr)
    aliases = {}
    if alias_extra is not None:
        aliases = {2 * len(pairs) + alias_extra: 0}
    return pl.pallas_call(
        functools.partial(_mm_kernel, n_pairs=len(pairs), kinds=tuple(kinds), epilogue=epilogue),
        grid=(n_out // tn, m // tm),
        in_specs=in_specs,
        out_specs=pl.BlockSpec((tm, tn), lambda j, i: (i, j)),
        out_shape=jax.ShapeDtypeStruct((m, n_out), out_dtype),
        scratch_shapes=scratch,
        input_output_aliases=aliases,
        compiler_params=_cparams(("parallel", "arbitrary"), est),
        name=name,
    )(*args)


def _s5_operators(lam_re, lam_im, log_dt, b_re, b_im, c_re, c_im, d_skip, t):
    n = OP["s5_group_dim"]
    lr = jnp.minimum(lam_re.astype(F32), OP["lambda_re_max"])
    li = lam_im.astype(F32)
    dt = jnp.exp(log_dt.astype(F32))[..., None]
    g, p = lr.shape[1], lr.shape[2]
    gt = S5_TILE_GROUPS
    j = g // gt
    mag = jnp.exp(lr * dt)
    a_re = mag * jnp.cos(li * dt)
    a_im = mag * jnp.sin(li * dt)
    num_re, num_im = a_re - 1.0, a_im
    den = lr * lr + li * li
    f_re = (num_re * lr + num_im * li) / den
    f_im = (num_im * lr - num_re * li) / den
    br, bi = b_re.astype(F32), b_im.astype(F32)
    bb_re = f_re[..., None] * br - f_im[..., None] * bi
    bb_im = f_re[..., None] * bi + f_im[..., None] * br
    cr, ci = c_re.astype(F32), c_im.astype(F32)

    k = jnp.arange(t + 1, dtype=F32)[:, None, None, None]
    pw_mag = jnp.exp(lr[None] * dt[None] * k)
    pw_re = pw_mag * jnp.cos(li[None] * dt[None] * k)
    pw_im = pw_mag * jnp.sin(li[None] * dt[None] * k)

    ca_re = cr[None] * pw_re[:t, :, :, None, :] - ci[None] * pw_im[:t, :, :, None, :]
    ca_im = cr[None] * pw_im[:t, :, :, None, :] + ci[None] * pw_re[:t, :, :, None, :]
    ca = jnp.concatenate([ca_re, -ca_im], axis=-1)
    ca = ca.transpose(1, 2, 0, 3, 4).reshape(2, g, t * n, 2 * p)
    bbs = jnp.concatenate([bb_re, bb_im], axis=2)
    kk = jnp.einsum("dgxp,dgpm->dgxm", ca, bbs, precision=HI)
    kk = kk.reshape(2, g, t, n, n).transpose(2, 0, 1, 3, 4)
    kf, kb = kk[:, 0], kk[:, 1]
    k0 = kf[0] + kb[0] + d_skip.astype(F32).reshape(g, n)[:, :, None] * jnp.eye(n, dtype=F32)
    kall = jnp.concatenate([kb[1:][::-1], k0[None], kf[1:]], axis=0)
    kc = kall.reshape(2 * t - 1, j, gt, n, n).transpose(1, 0, 4, 2, 3)
    kc = kc.reshape(j, 2 * t - 1, n, gt * n).astype(BF16)
    lane = jnp.arange(gt * n)
    same_group = (lane[:, None] // n) == (lane[None, :] // n)
    bd = jnp.where(same_group, jnp.tile(kc, (1, 1, gt, 1)), 0)

    ps_re = jnp.stack([pw_re[:t][::-1, 0], pw_re[:t, 1]], axis=0)
    ps_im = jnp.stack([pw_im[:t][::-1, 0], pw_im[:t, 1]], axis=0)
    w_re = ps_re[..., None] * bb_re[:, None] - ps_im[..., None] * bb_im[:, None]
    w_im = ps_re[..., None] * bb_im[:, None] + ps_im[..., None] * bb_re[:, None]
    w6 = jnp.stack([w_re, w_im], axis=1).reshape(2, 2, t, j, gt, p, n)
    wc = w6.transpose(3, 2, 6, 0, 1, 4, 5).reshape(j, t, n, 4 * gt * p).astype(BF16)

    po_re = jnp.stack([pw_re[1:, 0], pw_re[1:][::-1, 1]], axis=0)
    po_im = jnp.stack([pw_im[1:, 0], pw_im[1:][::-1, 1]], axis=0)
    co_re = cr[:, None] * po_re[:, :, :, None, :] - ci[:, None] * po_im[:, :, :, None, :]
    co_im = cr[:, None] * po_im[:, :, :, None, :] + ci[:, None] * po_re[:, :, :, None, :]
    o6 = jnp.stack([co_re, -co_im], axis=1).reshape(2, 2, t, j, gt, n, p)
    oc = o6.transpose(3, 0, 1, 6, 2, 4, 5).reshape(j, 4, p, t * gt * n).astype(BF16)
    at = jnp.stack([pw_re[t, 0], pw_im[t, 0], pw_re[t, 1], pw_im[t, 1]], axis=0)
    at = at.reshape(4, j, gt * p).transpose(1, 0, 2).reshape(j, 1, 4 * gt * p)
    return bd, wc, oc, at


def _s5_masks(t):
    n, p, gt = OP["s5_group_dim"], OP["s5_state"], S5_TILE_GROUPS
    col_group = (jnp.arange(4 * gt * p) // p) % gt
    mask_w = ((jnp.arange(gt * n)[:, None] // n) == col_group[None, :]).astype(BF16)
    out_group = (jnp.arange(t * gt * n) // n) % gt
    mask_o = ((jnp.arange(gt * p)[:, None] // p) == out_group[None, :]).astype(BF16)
    return mask_w, mask_o


def _gelu_tanh(y):
    return 0.5 * y * (1.0 + jnp.tanh(0.7978845608028654 * (y + 0.044715 * y * y * y)))


def _s5_kernel(uc_ref, ux_ref, bd_ref, wc_ref, oc_ref, mw_ref, mo_ref, at_ref, oc_out, ox_out,
               mt_scr, wst_scr, wout_scr, s_scr, hin_scr, hend_scr, *, c_ctx, c_lat, half, t, batch):
    lanes = ux_ref.shape[1]
    step = pl.program_id(1)

    @pl.when(step == 0)
    def _():
        for s in range(t):
            for k in range(t):
                mt_scr[s * lanes:(s + 1) * lanes, k * lanes:(k + 1) * lanes] = bd_ref[0, k - s + t - 1]
            rep = lanes // wc_ref.shape[2]
            wst_scr[s * lanes:(s + 1) * lanes, :] = jnp.tile(wc_ref[0, s], (rep, 1)) * mw_ref[...]
        rows = mo_ref.shape[0]
        for q in range(oc_ref.shape[1]):
            rep = rows // oc_ref.shape[2]
            wout_scr[q * rows:(q + 1) * rows, :] = jnp.tile(oc_ref[0, q], (rep, 1)) * mo_ref[...]

    at = at_ref[0]
    afr, afi = at[:, 0:half], at[:, half:2 * half]
    abr, abi = at[:, 2 * half:3 * half], at[:, 3 * half:4 * half]

    def run(u_ref, o_ref, n_seq, n_chunks, h0_of):
        rows = n_seq * n_chunks
        u = jnp.concatenate([u_ref[pl.ds(s, rows, stride=t), :] for s in range(t)],
                            axis=1).astype(BF16)
        s_scr[0:rows, :] = jnp.dot(u, wst_scr[...], preferred_element_type=F32)
        ends = []
        for q in range(n_seq):
            h0 = h0_of(q)
            init = (h0[:, 0:half], h0[:, half:2 * half], h0[:, 2 * half:3 * half],
                    h0[:, 3 * half:4 * half])
            lo = q * n_chunks

            def body(i, carry, lo=lo):
                hfr, hfi, hbr, hbi = carry
                cf = lo + i
                cb = lo + n_chunks - 1 - i
                hin_scr[pl.ds(cf, 1), 0:half] = hfr
                hin_scr[pl.ds(cf, 1), half:2 * half] = hfi
                hin_scr[pl.ds(cb, 1), 2 * half:3 * half] = hbr
                hin_scr[pl.ds(cb, 1), 3 * half:4 * half] = hbi
                sf = s_scr[pl.ds(cf, 1), 0:2 * half]
                sb = s_scr[pl.ds(cb, 1), 2 * half:4 * half]
                nfr = afr * hfr - afi * hfi + sf[:, 0:half]
                nfi = afr * hfi + afi * hfr + sf[:, half:2 * half]
                nbr = abr * hbr - abi * hbi + sb[:, 0:half]
                nbi = abr * hbi + abi * hbr + sb[:, half:2 * half]
                return nfr, nfi, nbr, nbi

            ends.append(jnp.concatenate(lax.fori_loop(0, n_chunks, body, init), axis=1))
        y = (jnp.dot(u, mt_scr[...], preferred_element_type=F32)
             + jnp.dot(hin_scr[0:rows, :].astype(BF16), wout_scr[...], preferred_element_type=F32))
        for k in range(t):
            o_ref[pl.ds(k, rows, stride=t), :] = _gelu_tanh(
                y[:, k * lanes:(k + 1) * lanes]).astype(o_ref.dtype)
        return ends

    @pl.when(step == 0)
    def _():
        zero = jnp.zeros((1, 4 * half), F32)
        ends = run(uc_ref, oc_out, batch, c_ctx, lambda q: zero)
        for q in range(batch):
            hend_scr[q:q + 1, :] = ends[q]

    @pl.when(step > 0)
    def _():
        run(ux_ref, ox_out, 1, c_lat, lambda q: hend_scr[pl.ds(step - 1, 1), :])


def _s5(uc2, ux2, ds5, batch, ops, layer):
    bd, wc, oc, mask_w, mask_o, at = ops
    mc, mx = uc2.shape[0], ux2.shape[0]
    lc, lx = mc // batch, mx // batch
    t = S5_CHUNK
    cc, cx = lc // t, lx // t
    j = ds5 // V7X_LANES
    w = t * V7X_LANES
    sw = wc.shape[4]
    compact = sum(_nbytes(a.shape[2:], BF16) for a in (bd, wc, oc)) + _nbytes(mask_w.shape, BF16) \
        + _nbytes(mask_o.shape, BF16)
    est = (3 * _nbytes((w, sw), BF16) + 2 * compact + 4 * _nbytes((lx + lc, V7X_LANES), F32)
           + 2 * _nbytes((cx, sw), F32) + 3 * _nbytes((cx, w), F32) + _nbytes((cx, sw), BF16)
           + 2 * _nbytes((cx, w), BF16))
    ctx_spec = pl.BlockSpec((mc, V7X_LANES), lambda jj, s: (0, jj))
    lat_spec = pl.BlockSpec((lx, V7X_LANES), lambda jj, s: (jnp.maximum(s - 1, 0), jj))
    scan_rows = max(cx, batch * cc)

    def op_spec(a):
        return pl.BlockSpec((None, 1) + a.shape[2:], lambda jj, s: (layer, jj) + (0,) * (a.ndim - 2))

    return pl.pallas_call(
        functools.partial(_s5_kernel, c_ctx=cc, c_lat=cx, half=sw // 4, t=t, batch=batch),
        grid=(j, 1 + batch),
        in_specs=[ctx_spec, lat_spec, op_spec(bd), op_spec(wc), op_spec(oc),
                  pl.BlockSpec(mask_w.shape, lambda jj, s: (0, 0)),
                  pl.BlockSpec(mask_o.shape, lambda jj, s: (0, 0)),
                  op_spec(at)],
        out_specs=[ctx_spec, lat_spec],
        out_shape=[jax.ShapeDtypeStruct((mc, ds5), F32), jax.ShapeDtypeStruct((mx, ds5), F32)],
        scratch_shapes=[pltpu.VMEM((w, w), BF16), pltpu.VMEM((w, sw), BF16), pltpu.VMEM((sw, w), BF16),
                        pltpu.VMEM((scan_rows, sw), F32), pltpu.VMEM((scan_rows, sw), F32),
                        pltpu.VMEM((-(-batch // 8) * 8, sw), F32)],
        compiler_params=_cparams(("arbitrary", "arbitrary"), est),
        name="s5",
    )(uc2, ux2, bd, wc, oc, mask_w, mask_o, at)


def _dft_mats(n):
    hi = n // V7X_LANES
    j = jnp.arange(n, dtype=jnp.int32)[None, :]
    a1 = ((jnp.arange(V7X_LANES, dtype=jnp.int32)[:, None] * j) % n).astype(F32) * (2.0 * math.pi / n)
    a2 = ((jnp.arange(hi, dtype=jnp.int32)[:, None] * j) % hi).astype(F32) * (2.0 * math.pi / hi)
    c1, s1 = jnp.cos(a1)[None, :, :], jnp.sin(a1)[None, :, :]
    c2, s2 = jnp.cos(a2)[:, None, :], jnp.sin(a2)[:, None, :]
    s = 1.0 / math.sqrt(n)
    cos = ((c2 * c1 - s2 * s1) * s).reshape(n, n).astype(BF16)
    sin = ((s2 * c1 + c2 * s1) * s).reshape(n, n).astype(BF16)
    return cos, sin


def _chan_dft_kernel(v_ref, w_ref, xc_ref, xs_ref, *, gd):
    r = jnp.dot(v_ref[...].astype(BF16), w_ref[...], preferred_element_type=F32)
    xc_ref[...] = r[:, :gd].astype(xc_ref.dtype)
    xs_ref[...] = r[:, gd:].astype(xs_ref.dtype)


def _chan_dft(v2, col_off, dft, wc, tm=1024):
    m = v2.shape[0]
    gd = wc.shape[0]
    tm = min(tm, m)
    goff = col_off // gd
    spec = pl.BlockSpec((tm, gd), lambda i, g: (i, g))
    est = (2 * _nbytes((tm, gd), F32) + 5 * _nbytes((tm, gd), BF16) + 2 * _nbytes((gd, 2 * gd), BF16)
           + 2 * _nbytes((tm, 2 * gd), F32))
    return pl.pallas_call(
        functools.partial(_chan_dft_kernel, gd=gd),
        grid=(m // tm, dft // gd),
        in_specs=[pl.BlockSpec((tm, gd), lambda i, g: (i, goff + g)),
                  pl.BlockSpec((gd, 2 * gd), lambda i, g: (0, 0))],
        out_specs=[spec, spec],
        out_shape=[jax.ShapeDtypeStruct((m, dft), BF16)] * 2,
        compiler_params=_cparams(("parallel", "parallel"), est),
        name="chan_dft",
    )(v2, wc)


def _seq_dft_kernel(cl_ref, sl_ref, xc_ref, xs_ref, o_ref):
    y = (jnp.dot(cl_ref[...], xc_ref[0], preferred_element_type=F32)
         - jnp.dot(sl_ref[...], xs_ref[0], preferred_element_type=F32))
    o_ref[0] = y.astype(o_ref.dtype)


def _seq_dft(xc, xs, cl, sl, batch, tm=512, tn=512):
    m, dft = xc.shape
    l = m // batch
    tm = min(tm, l)
    tn = min(tn, dft)
    xc3, xs3 = xc.reshape(batch, l, dft), xs.reshape(batch, l, dft)
    est = 4 * _nbytes((tm, l), BF16) + 4 * _nbytes((l, tn), BF16) + 4 * _nbytes((tm, tn), F32)
    out = pl.pallas_call(
        _seq_dft_kernel,
        grid=(batch, dft // tn, l // tm),
        in_specs=[
            pl.BlockSpec((tm, l), lambda b, jn, i: (i, 0)),
            pl.BlockSpec((tm, l), lambda b, jn, i: (i, 0)),
            pl.BlockSpec((1, l, tn), lambda b, jn, i: (b, 0, jn)),
            pl.BlockSpec((1, l, tn), lambda b, jn, i: (b, 0, jn)),
        ],
        out_specs=pl.BlockSpec((1, tm, tn), lambda b, jn, i: (b, i, jn)),
        out_shape=jax.ShapeDtypeStruct((batch, l, dft), BF16),
        compiler_params=_cparams(("parallel", "parallel", "parallel"), est),
        name="seq_dft",
    )(cl, sl, xc3, xs3)
    return out.reshape(m, dft)


def _prefix_count(mask_f32, tri):
    e, l = mask_f32.shape
    carry = jnp.zeros((e, 1), F32)
    outs = []
    for k in range(l // CUMSUM_BLOCK):
        blk = mask_f32[:, k * CUMSUM_BLOCK:(k + 1) * CUMSUM_BLOCK]
        outs.append(jnp.dot(blk.astype(BF16), tri, preferred_element_type=F32) + carry)
        carry = carry + jnp.sum(blk, axis=1, keepdims=True)
    return outs[0] if len(outs) == 1 else jnp.concatenate(outs, axis=1)


def _route_kernel(lg_ref, posm_ref, aff_ref, bnd_ref, *, cap, tile):
    lg = lg_ref[...]
    mx = jnp.max(lg, axis=0, keepdims=True)
    ex = jnp.exp(lg - mx)
    aff = ex / jnp.sum(ex, axis=0, keepdims=True)
    bits = pltpu.bitcast(aff, jnp.int32)
    e = lg.shape[0]
    v = jnp.zeros((e, 1), jnp.int32)
    for bit in range(30, -1, -1):
        cand = v | (1 << bit)
        cnt = jnp.sum(jnp.where(bits >= cand, 1.0, 0.0), axis=1, keepdims=True)
        v = jnp.where(cnt >= cap, cand, v)
    gt = bits > v
    eq = bits == v
    n_gt = jnp.sum(gt.astype(F32), axis=1, keepdims=True)
    r = lax.broadcasted_iota(jnp.int32, (CUMSUM_BLOCK, CUMSUM_BLOCK), 0)
    c = lax.broadcasted_iota(jnp.int32, (CUMSUM_BLOCK, CUMSUM_BLOCK), 1)
    tri = jnp.where(r < c, 1.0, 0.0).astype(BF16)
    eq_f = jnp.where(eq, 1.0, 0.0)
    tie_ok = _prefix_count(eq_f, tri) < (cap - n_gt)
    sel_f = jnp.where(gt, 1.0, jnp.where(tie_ok, eq_f, 0.0))
    pos = _prefix_count(sel_f, tri)
    posm_ref[0] = jnp.where(sel_f > 0.5, pos, -1.0).astype(jnp.int32)
    aff_ref[0] = aff
    l = lg.shape[1]
    starts = [pos[:, k * tile:k * tile + 1] for k in range(l // tile)]
    starts.append(jnp.full((e, 1), float(cap), F32))
    bnd_ref[0] = jnp.concatenate(starts, axis=1).astype(jnp.int32)


def _route(lg_t, batch, cap, tile):
    e, m = lg_t.shape
    l = m // batch
    nt = l // tile
    est = 16 * _nbytes((e, l), F32)
    return pl.pallas_call(
        functools.partial(_route_kernel, cap=cap, tile=tile),
        grid=(batch,),
        in_specs=[pl.BlockSpec((e, l), lambda b: (0, b))],
        out_specs=[pl.BlockSpec((1, e, l), lambda b: (b, 0, 0))] * 2
        + [pl.BlockSpec((1, e, nt + 1), lambda b: (b, 0, 0))],
        out_shape=[jax.ShapeDtypeStruct((batch, e, l), jnp.int32),
                   jax.ShapeDtypeStruct((batch, e, l), F32),
                   jax.ShapeDtypeStruct((batch, e, nt + 1), jnp.int32)],
        compiler_params=_cparams(("parallel",), est),
        name="route",
    )(lg_t)


def _slots_kernel(posm_ref, aff_ref, *rest, cap):
    idx_ref, gs_ref = rest[-2], rest[-1]
    pm = posm_ref[0, 0]
    l = pm.shape[1]
    slot = lax.broadcasted_iota(jnp.int32, (cap, l), 0)
    tok = lax.broadcasted_iota(jnp.int32, (1, l), 1).astype(F32)
    hit = pm == slot
    idx_ref[0, 0] = jnp.sum(jnp.where(hit, tok, 0.0), axis=1, keepdims=True).astype(jnp.int32)
    gs_ref[0] = jnp.sum(jnp.where(hit, aff_ref[0, 0], 0.0), axis=1, keepdims=True)


def _slots(posm, aff, cap, n_rows, row_off, gs_all=None):
    batch, e, l = posm.shape
    blk = row_off // cap
    in_specs = [pl.BlockSpec((1, 1, 1, l), lambda b, ee: (b, ee, 0, 0))] * 2
    args = [posm.reshape(batch, e, 1, l), aff.reshape(batch, e, 1, l)]
    aliases = {}
    if gs_all is not None:
        in_specs.append(pl.BlockSpec(memory_space=pl.ANY))
        args.append(gs_all)
        aliases = {2: 1}
    est = 6 * _nbytes((cap, l), F32) + 4 * _nbytes((cap, V7X_LANES), F32)
    return pl.pallas_call(
        functools.partial(_slots_kernel, cap=cap),
        grid=(batch, e),
        in_specs=in_specs,
        out_specs=[pl.BlockSpec((1, 1, cap, 1), lambda b, ee: (b, ee, 0, 0)),
                   pl.BlockSpec((1, cap, 1), lambda b, ee: (ee, blk + b, 0))],
        out_shape=[jax.ShapeDtypeStruct((batch, e, cap, 1), jnp.int32),
                   jax.ShapeDtypeStruct((e, n_rows, 1), F32)],
        input_output_aliases=aliases,
        compiler_params=_cparams(("parallel", "parallel"), est),
        name="moe_slots",
    )(*args)


GATHER_ISSUE_UNROLL = 8


def _gather_kernel(idx_ref, hp_hbm, *rest, cap, seq, n_exp):
    o_ref, sem = rest[-2], rest[-1]
    b, ee = pl.program_id(0), pl.program_id(1)
    base = (b * n_exp + ee) * cap
    row0 = b * seq

    def issue(c, carry):
        for k in range(GATHER_ISSUE_UNROLL):
            s = c * GATHER_ISSUE_UNROLL + k
            pltpu.make_async_copy(hp_hbm.at[pl.ds(row0 + idx_ref[base + s], 1)],
                                  o_ref.at[0, pl.ds(s, 1)], sem).start(priority=k % 2)
        return carry

    lax.fori_loop(0, cap // GATHER_ISSUE_UNROLL, issue, 0)
    pltpu.make_async_copy(hp_hbm.at[pl.ds(0, cap)], o_ref.at[0], sem).wait()


def _gather(idx, hp, batch, cap, n_rows, row_off, xg_all=None):
    e = idx.shape[1]
    m, dh = hp.shape
    seq = m // batch
    blk = row_off // cap
    in_specs = [pl.BlockSpec(memory_space=pl.ANY)]
    args = [idx.reshape(-1), hp]
    aliases = {}
    if xg_all is not None:
        in_specs.append(pl.BlockSpec(memory_space=pl.ANY))
        args.append(xg_all)
        aliases = {2: 0}
    est = 2 * _nbytes((cap, dh), jnp.uint32)
    return pl.pallas_call(
        functools.partial(_gather_kernel, cap=cap, seq=seq, n_exp=e),
        grid_spec=pltpu.PrefetchScalarGridSpec(
            num_scalar_prefetch=1,
            grid=(batch, e),
            in_specs=in_specs,
            out_specs=pl.BlockSpec((1, cap, dh), lambda b, ee, ix: (ee, blk + b, 0)),
            scratch_shapes=[pltpu.SemaphoreType.DMA(())],
        ),
        out_shape=jax.ShapeDtypeStruct((e, n_rows, dh), jnp.uint32),
        input_output_aliases=aliases,
        compiler_params=_cparams(("arbitrary", "arbitrary"), est),
        name="moe_gather",
    )(*args)


EXPERT_ROW_BLOCKS = 2


def _up_kernel(xg_ref, wg_ref, wu_ref, o_ref, wg_bf, wu_bf):
    @pl.when(pl.program_id(2) == 0)
    def _():
        wg_bf[...] = wg_ref[0, 0].astype(BF16)
        wu_bf[...] = wu_ref[0, 0].astype(BF16)

    x_lo, x_hi = _unpack_bf16_pairs(xg_ref[0])
    half = x_lo.shape[1]
    g = (jnp.dot(x_lo, wg_bf[0:half, :], preferred_element_type=F32)
         + jnp.dot(x_hi, wg_bf[half:, :], preferred_element_type=F32))
    u = (jnp.dot(x_lo, wu_bf[0:half, :], preferred_element_type=F32)
         + jnp.dot(x_hi, wu_bf[half:, :], preferred_element_type=F32))
    o_ref[0] = (g * jax.nn.sigmoid(g) * u).astype(o_ref.dtype)


def _expert_up(xg, w_gate, w_up, layer, tf=256):
    e, r, dh = xg.shape
    d = 2 * dh
    f = w_gate.shape[3]
    tf = min(tf, f)
    tr = r // EXPERT_ROW_BLOCKS
    est = (4 * _nbytes((d, tf), F32) + 3 * _nbytes((d, tf), BF16) + 2 * _nbytes((tr, d), BF16)
           + 6 * _nbytes((tr, tf), F32))
    return pl.pallas_call(
        _up_kernel,
        grid=(e, f // tf, EXPERT_ROW_BLOCKS),
        in_specs=[
            pl.BlockSpec((1, tr, dh), lambda ee, fj, rr: (ee, rr, 0)),
            pl.BlockSpec((1, 1, d, tf), lambda ee, fj, rr: (layer, ee, 0, fj)),
            pl.BlockSpec((1, 1, d, tf), lambda ee, fj, rr: (layer, ee, 0, fj)),
        ],
        out_specs=pl.BlockSpec((1, tr, tf), lambda ee, fj, rr: (ee, rr, fj)),
        out_shape=jax.ShapeDtypeStruct((e, r, f), BF16),
        scratch_shapes=[pltpu.VMEM((d, tf), BF16), pltpu.VMEM((d, tf), BF16)],
        compiler_params=_cparams(("parallel", "parallel", "arbitrary"), est),
        name="moe_up",
    )(xg, w_gate, w_up)


def _down_kernel(h_ref, wd_ref, gs_ref, o_ref, wd_bf):
    @pl.when(pl.program_id(2) == 0)
    def _():
        wd_bf[...] = wd_ref[0, 0].astype(BF16)

    y = jnp.dot(h_ref[0], wd_bf[...], preferred_element_type=F32)
    o_ref[0] = (y * gs_ref[0]).astype(o_ref.dtype)


def _expert_down(hid, w_down, gslot, layer, td=2048):
    e, r, f = hid.shape
    d = w_down.shape[3]
    td = min(td, d)
    tr = r // EXPERT_ROW_BLOCKS
    est = (2 * _nbytes((f, td), F32) + 2 * _nbytes((f, td), BF16) + 2 * _nbytes((tr, f), BF16)
           + 4 * _nbytes((tr, td), F32) + 2 * _nbytes((tr, V7X_LANES), F32))
    return pl.pallas_call(
        _down_kernel,
        grid=(e, d // td, EXPERT_ROW_BLOCKS),
        in_specs=[
            pl.BlockSpec((1, tr, f), lambda ee, dj, rr: (ee, rr, 0)),
            pl.BlockSpec((1, 1, f, td), lambda ee, dj, rr: (layer, ee, 0, dj)),
            pl.BlockSpec((1, tr, 1), lambda ee, dj, rr: (ee, rr, 0)),
        ],
        out_specs=pl.BlockSpec((1, tr, td), lambda ee, dj, rr: (ee, rr, dj)),
        out_shape=jax.ShapeDtypeStruct((e, r, d), BF16),
        scratch_shapes=[pltpu.VMEM((f, td), BF16)],
        compiler_params=_cparams(("parallel", "parallel", "arbitrary"), est),
        name="moe_down",
    )(hid, w_down, gslot)


COMBINE_TILE = 256
DMA_ROW_ALIGN = 16


def combine_window(cap, seq, tile):
    return min(cap, max(DMA_ROW_ALIGN, 2 * tile * cap // seq))


def _combine_kernel(tbl_ref, pt_ref, y_hbm, x_ref, g_ref, *rest, n_exp, cap, win, nt, row_off,
                    n_steps, emit_x, post_norm, post_mod, eps):
    stage, extra, acc_scr, sems, xsem = rest[-5:]
    n_post_in = (1 + 2 * post_mod) if post_norm else 0
    post_in = rest[:n_post_in]
    outs = rest[n_post_in:-5]
    n = pl.program_id(0) * nt + pl.program_id(1)
    cur = n % 2
    nxt_step = jnp.minimum(n + 1, n_steps - 1)

    def window_starts(step):
        b, i = step // nt, step % nt
        t0 = (b * (nt + 1) + i) * n_exp
        out = []
        for ee in range(n_exp):
            s0a = jnp.minimum((tbl_ref[t0 + ee] // DMA_ROW_ALIGN) * DMA_ROW_ALIGN, cap - win)
            out.append(pl.multiple_of(s0a, DMA_ROW_ALIGN))
        return row_off + b * cap, out, t0

    def issue_windows(step, buf):
        row_base, starts, _ = window_starts(step)
        for ee in range(n_exp):
            pltpu.make_async_copy(y_hbm.at[ee, pl.ds(row_base + starts[ee], win), :],
                                  stage.at[buf, pl.ds(ee * win, win), :], sems.at[buf]).start()

    def wait_windows(buf):
        pltpu.make_async_copy(y_hbm.at[0, pl.ds(0, n_exp * win), :], stage.at[buf],
                              sems.at[buf]).wait()

    @pl.when(n == 0)
    def _():
        issue_windows(0, 0)

    issue_windows(nxt_step, 1 - cur)
    row_base, starts, t0 = window_starts(n)
    pt = pt_ref[...]
    lane = lax.broadcasted_iota(jnp.int32, (pt.shape[0], win), 1)
    p = jnp.concatenate(
        [jnp.where(pt[:, ee:ee + 1] - starts[ee] == lane, 1.0, 0.0).astype(BF16)
         for ee in range(n_exp)], axis=1)
    wait_windows(cur)
    acc_scr[...] = jnp.dot(p, stage[cur], preferred_element_type=F32)

    for ee in range(n_exp):
        first = starts[ee] + win
        n_extra = jnp.maximum(0, (tbl_ref[t0 + n_exp + ee] - first + win - 1) // win)

        def extra_window(w, carry, ee=ee, first=first):
            lo = first + w * win
            src0 = pl.multiple_of(jnp.minimum(lo, cap - win), DMA_ROW_ALIGN)
            cp = pltpu.make_async_copy(y_hbm.at[ee, pl.ds(row_base + src0, win), :], extra, xsem)
            cp.start()
            cp.wait()
            col = pt[:, ee:ee + 1]
            hit = jnp.where(col >= lo, col - src0, -1) == lane
            acc_scr[...] += jnp.dot(jnp.where(hit, 1.0, 0.0).astype(BF16), extra[...],
                                    preferred_element_type=F32)
            return carry

        lax.fori_loop(0, n_extra, extra_window, 0)

    xn = x_ref[...] + g_ref[0] * acc_scr[...]
    if emit_x:
        outs[0][...] = xn
    if post_norm:
        shift, scale = (post_in[1][0], post_in[2][0]) if post_mod else (None, None)
        h_ref = outs[-1]
        h_ref[...] = _rms_modulate(xn, post_in[0][...], shift, scale, eps).astype(h_ref.dtype)

    @pl.when(n == n_steps - 1)
    def _():
        wait_windows(1 - cur)


def _combine(posm_t, bounds, y, x2, gate, batch, cap, row_off, post=None, emit_x=True):
    m, d = x2.shape
    l = m // batch
    e = posm_t.shape[1]
    tm = min(COMBINE_TILE, l)
    nt = l // tm
    win = combine_window(cap, l, tm)
    assert cap % DMA_ROW_ALIGN == 0 and win % DMA_ROW_ALIGN == 0 and bounds.shape == (batch, e, nt + 1)
    tbl = bounds.transpose(0, 2, 1).reshape(-1)
    est = (2 * _nbytes((e * win, d), BF16) + _nbytes((win, d), BF16) + 9 * _nbytes((tm, d), F32)
           + 2 * _nbytes((tm, e * win), BF16) + 2 * _nbytes((tm, V7X_LANES), F32))
    row_spec = pl.BlockSpec((tm, d), lambda b, i, t: (b * nt + i, 0))
    vec_spec = pl.BlockSpec((1, 1, d), lambda b, i, t: (b, 0, 0))
    in_specs = [pl.BlockSpec((tm, e), lambda b, i, t: (b * nt + i, 0)),
                pl.BlockSpec(memory_space=pl.ANY), row_spec, vec_spec]
    args = [tbl, posm_t, y, x2, gate]
    out_specs, out_shape = [], []
    if emit_x:
        out_specs.append(row_spec)
        out_shape.append(jax.ShapeDtypeStruct((m, d), F32))
    post_mod = False
    if post is not None:
        p_gain, p_shift, p_scale, p_dtype = post
        post_mod = p_shift is not None
        in_specs.append(pl.BlockSpec((1, d), lambda b, i, t: (0, 0)))
        args.append(p_gain.reshape(1, d))
        if post_mod:
            in_specs += [vec_spec, vec_spec]
            args += [p_shift, p_scale]
        out_specs.append(row_spec)
        out_shape.append(jax.ShapeDtypeStruct((m, d), p_dtype))
        est += 2 * _nbytes((tm, d), p_dtype)
    return pl.pallas_call(
        functools.partial(_combine_kernel, n_exp=e, cap=cap, win=win, nt=nt, row_off=row_off,
                          n_steps=batch * nt, emit_x=emit_x, post_norm=post is not None,
                          post_mod=post_mod, eps=OP["rms_eps"]),
        grid_spec=pltpu.PrefetchScalarGridSpec(
            num_scalar_prefetch=1,
            grid=(batch, nt),
            in_specs=in_specs,
            out_specs=out_specs,
            scratch_shapes=[pltpu.VMEM((2, e * win, d), BF16), pltpu.VMEM((win, d), BF16),
                            pltpu.VMEM((tm, d), F32), pltpu.SemaphoreType.DMA((2,)),
                            pltpu.SemaphoreType.DMA(())],
        ),
        out_shape=out_shape,
        input_output_aliases={3: 0} if emit_x else {},
        compiler_params=_cparams(("arbitrary", "arbitrary"), est),
        name="moe_combine",
    )(*args)


def _sigmoid(z):
    return jax.nn.sigmoid(z)


def _in_proj(h2, w_in, layer, n_cols):
    return _mm([(h2, w_in, layer, 0)], [], lambda accs, ex: accs[0], n_cols, F32, 1024,
               math.gcd(512, n_cols), name="in_proj")


def _mixer_out(x2, a2, px, g_mix, wts, layer, dfts, batch, rpb):
    w_glu, w_s5o, w_fto, w_out = wts
    wc, cl, sl = dfts
    d = x2.shape[1]
    d_s5 = a2.shape[1]
    d_ft = w_fto.shape[1]
    xc, xs = _chan_dft(px, d_s5, d_ft, wc)
    yf = _seq_dft(xc, xs, cl, sl, batch)
    glu = _mm([(a2, w_glu, layer, 0)], [("tile", a2, 0)],
              lambda accs, ex: ex[0] * _sigmoid(accs[0]),
              d_s5, BF16, 1024, 512, name="glu")
    tn = math.gcd(512, d_s5 + d_ft, d)
    g0 = (d_s5 + d_ft) // tn
    merged = _mm([(glu, w_s5o, layer, 0), (yf, w_fto, layer, 0)],
                 [("tile", px, g0), ("tile", px, g0 + d // tn)],
                 lambda accs, ex: _sigmoid(ex[0]) * accs[0] + _sigmoid(ex[1]) * accs[1],
                 d, BF16, 1024, tn, name="merge")
    return _mm([(merged, w_out, layer, 0)], [("tile", x2, 0), ("row", g_mix, 0)],
               lambda accs, ex: ex[0] + ex[1] * accs[0],
               d, F32, 1024, tn, rows_per_batch=rpb, alias_extra=0, name="out_proj")


def _moe(streams, w_gate, w_up, w_down, layer):
    e = streams[0][4].shape[0]
    caps = [OP["capacity_factor"] * (s[0].shape[0] // s[1]) // e for s in streams]
    offs, n_rows = [], 0
    for s, cap in zip(streams, caps):
        assert n_rows % cap == 0
        offs.append(n_rows)
        n_rows += s[1] * cap
    xg = gslot = None
    if len(streams) > 1:
        xg = jnp.zeros((e, n_rows, streams[0][2].shape[1]), jnp.uint32)
        gslot = jnp.zeros((e, n_rows, 1), F32)
    routed = []
    for (x2, batch, hp, _, lg_t, _, _), cap, off in zip(streams, caps, offs):
        tile = min(COMBINE_TILE, x2.shape[0] // batch)
        posm, aff, bounds = _route(lg_t, batch, cap, tile)
        idx, gslot = _slots(posm, aff, cap, n_rows, off, gslot)
        xg = _gather(idx, hp, batch, cap, n_rows, off, xg)
        routed.append((posm, bounds))
    y = _expert_down(_expert_up(xg, w_gate, w_up, layer), w_down, gslot, layer)
    outs = []
    for (x2, batch, _, gate, _, post, emit_x), cap, off, (posm, bounds) in zip(
            streams, caps, offs, routed):
        posm_t = posm.transpose(0, 2, 1).reshape(x2.shape[0], e)
        outs.append(_combine(posm_t, bounds, y, x2, gate, batch, cap, off, post, emit_x))
    return outs


def _grid_posembed(n_tok, d):
    gw = OP["grid_w"]
    rows = n_tok // gw
    quarter = d // 4
    inv_freq = OP["pos_base"] ** (-jnp.arange(quarter, dtype=F32) / quarter)
    ang_r = jnp.arange(rows, dtype=F32)[:, None] * inv_freq
    ang_c = jnp.arange(gw, dtype=F32)[:, None] * inv_freq
    return (jnp.concatenate([jnp.sin(ang_r), jnp.cos(ang_r)], axis=-1),
            jnp.concatenate([jnp.sin(ang_c), jnp.cos(ang_c)], axis=-1))


def kernel(x, c, ctx, c_ctx, ada_w, ada_b, norm_mix_g, norm_ffn_g, w_in, s5_lam_re, s5_lam_im,
           s5_log_dt, s5_b_re, s5_b_im, s5_c_re, s5_c_im, s5_d, w_glu, w_s5_out, w_ft_out, w_out,
           w_router, w_gate, w_up, w_down, norm_final_g):
    batch, seq, d = x.shape
    ctx_len = ctx.shape[1]
    depth = ada_w.shape[0]
    d_s5 = w_glu.shape[1]
    d_ft = w_ft_out.shape[1]
    n_mod = OP["n_mod"]
    m_x, m_c = batch * seq, batch * ctx_len

    x2 = x.reshape(m_x, d)
    pos = tuple(p.astype(x.dtype) for p in _grid_posembed(seq, d))
    c2 = ctx.reshape(m_c, d)

    rows = -(-(batch + 1) // 8) * 8
    c8 = jnp.zeros((rows, d), F32).at[:batch].set(c).at[batch].set(c_ctx)
    mod = _adaln(c8, ada_w, ada_b)

    gd = d_ft // OP["ft_groups"]
    wc_c, wc_s = _dft_mats(gd)
    wc = jnp.concatenate([wc_c, wc_s], axis=1)
    dft_x = (wc,) + _dft_mats(seq)
    dft_c = (wc,) + _dft_mats(ctx_len)

    bd, wc_s5, oc_s5, at = jax.vmap(lambda *p: _s5_operators(*p, S5_CHUNK))(
        s5_lam_re, s5_lam_im, s5_log_dt, s5_b_re, s5_b_im, s5_c_re, s5_c_im, s5_d)
    s5_ops = (bd, wc_s5, oc_s5) + _s5_masks(S5_CHUNK) + (at,)
    d_in = w_in.shape[2]
    mix_w = (w_glu, w_s5_out, w_ft_out, w_out)

    mods_x = [[mod[i, :batch, k * d:(k + 1) * d].reshape(batch, 1, d) for k in range(n_mod)]
              for i in range(depth)]
    mods_c = [[jnp.broadcast_to(mod[i, batch:batch + 1, k * d:(k + 1) * d].reshape(1, 1, d),
                                (batch, 1, d)) for k in range(n_mod)] for i in range(depth)]

    hx = hc = None
    for i in range(depth):
        last = i == depth - 1
        mx, mc = mods_x[i], mods_c[i]
        wr_t = w_router[i].T.astype(BF16)
        if i == 0:
            hc = _norm(c2, norm_mix_g[i], mc[0], mc[1], rows_per_batch=ctx_len)
            hx, x2 = _norm(x2, norm_mix_g[i], mx[0], mx[1], rows_per_batch=seq, pos=pos)
        pc = _in_proj(hc, w_in, i, d_s5 if last else d_in)
        px = _in_proj(hx, w_in, i, d_in)
        ac, ax = _s5(pc, px, d_s5, batch, s5_ops, i)
        x2 = _mixer_out(x2, ax, px, mx[2], mix_w, i, dft_x, batch, seq)
        lgx, hpx = _norm(x2, norm_ffn_g[i], mx[3], mx[4], rows_per_batch=seq, wr_t=wr_t,
                         emit_h=False, pack=True)
        if last:
            post_x, post_c = (norm_final_g, None, None, x.dtype), None
        else:
            post_x = (norm_mix_g[i + 1], mods_x[i + 1][0], mods_x[i + 1][1], BF16)
            post_c = (norm_mix_g[i + 1], mods_c[i + 1][0], mods_c[i + 1][1], BF16)
        streams = [(x2, batch, hpx, mx[5], lgx, post_x, not last)]
        if not last:
            c2 = _mixer_out(c2, ac, pc, mc[2], mix_w, i, dft_c, batch, ctx_len)
            lgc, hpc = _norm(c2, norm_ffn_g[i], mc[3], mc[4], rows_per_batch=ctx_len, wr_t=wr_t,
                             emit_h=False, pack=True)
            streams.append((c2, batch, hpc, mc[5], lgc, post_c, True))
        outs = _moe(streams, w_gate, w_up, w_down, i)
        if last:
            return outs[0][0].reshape(batch, seq, d)
        x2, hx = outs[0]
        c2, hc = outs[1]
```
